```python
import math
import jax, jax.numpy as jnp
from jax import lax
import numpy as np

D_MODEL = 1024
BATCH = 4
SEQ = 4096
DEPTH = 4
DEC_BATCH = 128
DEC_SEQ = 1
PAST_LEN = 2048
PAGE_SIZE = 128

N_EVEN = (DEPTH + 1) // 2
N_ODD = DEPTH // 2
H_A = 4
DH_A = 64
E_A = 2 * DH_A
ROT_A = DH_A // 4
ROPE_THETA = 500000.0
Q_BLOCK = 128
SUBLN_EPS = 1e-5
W_B = D_MODEL // 2
SCONV = 3
H_C = 8
DK_C = D_MODEL // H_C
DV_C = D_MODEL // H_C
KDIM_C = H_C * DK_C
VDIM_C = H_C * DV_C
QKV_C = 2 * KDIM_C + VDIM_C
GDN_CONV = 4
CHUNK = 64
D_FF = 2816
FFN_CONV = 3
EPS = 1e-6
ATTN_IN = 3 * H_A * E_A + 3 * W_B
GDN_IN = QKV_C + VDIM_C + 2 * H_C

kernel_name = 'hybrid_diffattn_sconv_gdn_convffn_step'


def rmsnorm(x, g, eps=EPS):
    xf = x.astype(jnp.float32)
    y = xf * lax.rsqrt(jnp.mean(xf * xf, axis=-1, keepdims=True) + eps)
    return (y * g.astype(jnp.float32)).astype(x.dtype)


def causal_dwconv(hist, w):
    width = w.shape[0]
    L = hist.shape[1] - width + 1
    out = hist[:, 0:L] * w[0]
    for j in range(1, width):
        out = out + hist[:, j:j + L] * w[j]
    return out


def rope(x, pos):
    half = ROT_A // 2
    inv = ROPE_THETA ** (-jnp.arange(half, dtype=jnp.float32) * 2.0 / ROT_A)
    ang = pos.astype(jnp.float32)[:, None] * inv[None, :]
    shp = (ang.shape[0],) + (1,) * (x.ndim - 3) + (half,)
    cos = jnp.cos(ang).reshape(shp).astype(x.dtype)
    sin = jnp.sin(ang).reshape(shp).astype(x.dtype)
    x1 = x[..., :half]
    x2 = x[..., half:ROT_A]
    return jnp.concatenate([x1 * cos - x2 * sin, x2 * cos + x1 * sin, x[..., ROT_A:]], axis=-1)


def diff_lambda(lq1, lk1, lq2, lk2, lam_init):
    f = lambda a: a.astype(jnp.float32)
    return jnp.exp(jnp.sum(f(lq1) * f(lk1))) - jnp.exp(jnp.sum(f(lq2) * f(lk2))) + lam_init


def even_project(xn, pos, w_in):
    B, L, _ = xn.shape
    h = xn @ w_in
    n = H_A * E_A
    q = rope(h[..., :n].reshape(B, L, H_A, 2, DH_A), pos)
    k = rope(h[..., n:2 * n].reshape(B, L, H_A, 2, DH_A), pos)
    v = h[..., 2 * n:3 * n].reshape(B, L, H_A, E_A)
    o = 3 * n
    bg = h[..., o:o + W_B]
    cg = h[..., o + W_B:o + 2 * W_B]
    xb = h[..., o + 2 * W_B:o + 3 * W_B]
    return q, k, v, bg, cg * xb


def diff_attn_core(q, k, v, mask, lam):
    s = jnp.einsum('bqhcd,bkhcd->bhcqk', q, k).astype(jnp.float32) * (DH_A ** -0.5)
    s = jnp.where(mask, s, jnp.finfo(jnp.float32).min)
    p = jax.nn.softmax(s, axis=-1)
    a = p[:, :, 0] - lam * p[:, :, 1]
    return jnp.einsum('bhqk,bkhe->bqhe', a, v.astype(jnp.float32)).astype(v.dtype)


def diff_attn_prompt(q, k, v, lam):
    B, S = q.shape[:2]
    nb = S // Q_BLOCK
    qb = q.reshape(B, nb, Q_BLOCK, H_A, 2, DH_A).transpose(1, 0, 2, 3, 4, 5)
    kpos = jnp.arange(S)

    def block(args):
        qi, start = args
        qpos = start + jnp.arange(Q_BLOCK)
        return diff_attn_core(qi, k, v, kpos[None, :] <= qpos[:, None], lam)

    out = lax.map(block, (qb, jnp.arange(nb) * Q_BLOCK))
    return out.transpose(1, 0, 2, 3, 4).reshape(B, S, H_A, E_A)


def diff_attn_sample(q, k_new, v_new, k_pages, v_pages, lam):
    B, L = q.shape[:2]
    k_past = k_pages.reshape(B, -1, H_A, 2, DH_A).astype(k_new.dtype)
    v_past = v_pages.reshape(B, -1, H_A, E_A).astype(v_new.dtype)
    k = jnp.concatenate([k_past, k_new], axis=1)
    v = jnp.concatenate([v_past, v_new], axis=1)
    i = jnp.arange(L)[:, None]
    j = jnp.arange(PAST_LEN + L)[None, :]
    mask = (j < PAST_LEN) | (j - PAST_LEN <= i)
    return diff_attn_core(q, k, v, mask, lam)


def even_output(att, bg, cx_hist, lam_init, subln_w, sconv_w, w_out):
    B, L = att.shape[:2]
    a = rmsnorm(att, subln_w, SUBLN_EPS) * (1.0 - lam_init)
    yb = bg * causal_dwconv(cx_hist, sconv_w)
    return jnp.concatenate([a.reshape(B, L, H_A * E_A), yb], axis=-1) @ w_out


def gdn_chunked(q, k, v, g, beta, S0):
    B, L, H, _ = q.shape
    n = L // CHUNK

    def blk(t):
        return t.reshape(B, n, CHUNK, H, -1).transpose(1, 0, 3, 2, 4)

    q, k, v = blk(q), blk(k), blk(v)
    g = blk(g[..., None])[..., 0]
    beta = blk(beta[..., None])[..., 0]
    gc = jnp.cumsum(g, axis=-1)
    idx = jnp.arange(CHUNK)
    incl = idx[:, None] >= idx[None, :]
    strict = idx[:, None] > idx[None, :]
    diff = gc[..., :, None] - gc[..., None, :]
    decay = jnp.where(incl, jnp.exp(jnp.where(incl, diff, 0.0)), 0.0)
    kb = k * beta[..., None]
    lmat = jnp.where(strict, jnp.einsum('nbhcd,nbhsd->nbhcs', kb, k) * decay, 0.0)
    amat = lmat + jnp.eye(CHUNK, dtype=lmat.dtype)

    def solve(b):
        return lax.linalg.triangular_solve(amat, b, left_side=True, lower=True, unit_diagonal=True)

    u = solve(v * beta[..., None])
    w = solve(kb * jnp.exp(gc)[..., None])
    qk = jnp.where(incl, jnp.einsum('nbhcd,nbhsd->nbhcs', q, k) * decay, 0.0)

    def step(S, xs):
        q_i, k_i, u_i, w_i, qk_i, gc_i = xs
        v_new = u_i - jnp.einsum('bhcd,bhde->bhce', w_i, S)
        o = (jnp.einsum('bhcd,bhde->bhce', q_i * jnp.exp(gc_i)[..., None], S)
             + jnp.einsum('bhcs,bhse->bhce', qk_i, v_new))
        g_last = gc_i[..., -1:]
        S = (S * jnp.exp(g_last)[..., None]
             + jnp.einsum('bhcd,bhce->bhde', k_i * jnp.exp(g_last - gc_i)[..., None], v_new))
        return S, o

    S, o = lax.scan(step, S0, (q, k, u, w, qk, gc))
    return o.transpose(1, 0, 3, 2, 4).reshape(B, L, H, -1), S


def gdn_recurrent(q, k, v, g, beta, S0):
    def step(S, xs):
        q_t, k_t, v_t, g_t, b_t = xs
        S = S * jnp.exp(g_t)[..., None, None]
        delta = (v_t - jnp.einsum('bhd,bhde->bhe', k_t, S)) * b_t[..., None]
        S = S + jnp.einsum('bhd,bhe->bhde', k_t, delta)
        return S, jnp.einsum('bhd,bhde->bhe', q_t, S)

    sw = lambda t: jnp.swapaxes(t, 0, 1)
    S, o = lax.scan(step, S0, (sw(q), sw(k), sw(v), sw(g), sw(beta)))
    return sw(o), S


def gdn_mixer(xn, conv_hist, S0, w_in, conv_w, a_log, dt_bias, norm_w, w_out, chunked):
    B, L, _ = xn.shape
    f32 = jnp.float32
    h = xn @ w_in
    qkv = h[..., :QKV_C]
    z = h[..., QKV_C:QKV_C + VDIM_C]
    b = h[..., QKV_C + VDIM_C:QKV_C + VDIM_C + H_C]
    a = h[..., QKV_C + VDIM_C + H_C:]
    hist = jnp.concatenate([conv_hist.astype(qkv.dtype), qkv], axis=1)
    c = jax.nn.silu(causal_dwconv(hist, conv_w)).astype(f32)
    q = c[..., :KDIM_C].reshape(B, L, H_C, DK_C)
    k = c[..., KDIM_C:2 * KDIM_C].reshape(B, L, H_C, DK_C)
    v = c[..., 2 * KDIM_C:].reshape(B, L, H_C, DV_C)
    l2n = lambda t: t * lax.rsqrt(jnp.sum(t * t, axis=-1, keepdims=True) + EPS)
    q = l2n(q) * (DK_C ** -0.5)
    k = l2n(k)
    beta = jax.nn.sigmoid(b.astype(f32))
    g = -jnp.exp(a_log.astype(f32)) * jax.nn.softplus(a.astype(f32) + dt_bias.astype(f32))
    if chunked:
        o, S = gdn_chunked(q, k, v, g, beta, S0.astype(f32))
    else:
        o, S = gdn_recurrent(q, k, v, g, beta, S0.astype(f32))
    o = rmsnorm(o, norm_w) * jax.nn.silu(z.reshape(B, L, H_C, DV_C).astype(f32))
    y = o.reshape(B, L, VDIM_C).astype(xn.dtype) @ w_out
    return y, hist[:, -(GDN_CONV - 1):], S


def conv_ffn(xn, hist, w_up, conv_w, conv_b, w_down):
    h = xn @ w_up
    gate = h[..., :D_FF]
    up = h[..., D_FF:]
    gh = jnp.concatenate([hist.astype(gate.dtype), gate], axis=1)
    gcv = causal_dwconv(gh, conv_w) + conv_b
    return (jax.nn.silu(gcv) * up) @ w_down, gh[:, -(FFN_CONV - 1):]


def setup_inputs(seed: int = 0) -> dict:
    key = jax.random.key(seed)
    ks = list(jax.random.split(key, 40))
    f32 = jnp.float32

    def nrm(shape, scale):
        return scale * jax.random.normal(ks.pop(), shape, f32)

    n_pages = PAST_LEN // PAGE_SIZE
    n_used = DEC_BATCH * n_pages
    n_phys = n_used + n_used // 4
    page_table = jax.random.permutation(ks.pop(), n_phys)[:n_used].reshape(DEC_BATCH, n_pages).astype(jnp.int32)
    dt = jnp.exp(jax.random.uniform(ks.pop(), (N_ODD, H_C), f32, math.log(1e-3), math.log(1e-1)))
    dt_bias = dt + jnp.log(-jnp.expm1(-dt))
    a_log = jnp.log(jax.random.uniform(ks.pop(), (N_ODD, H_C), f32, 1.0, 16.0))
    return dict(
        x_prompt=nrm((BATCH, SEQ, D_MODEL), 1.0),
        x_sample=nrm((DEC_BATCH, DEC_SEQ, D_MODEL), 1.0),
        cache_k=nrm((N_EVEN, n_phys, PAGE_SIZE, H_A, E_A), 1.0),
        cache_v=nrm((N_EVEN, n_phys, PAGE_SIZE, H_A, E_A), 1.0),
        state_sconv=nrm((N_EVEN, DEC_BATCH, SCONV - 1, W_B), 1.0),
        state_gdn_conv=nrm((N_ODD, DEC_BATCH, GDN_CONV - 1, QKV_C), 1.0),
        state_gdn=nrm((N_ODD, DEC_BATCH, H_C, DK_C, DV_C), 0.1),
        state_ffn_conv=nrm((DEPTH, DEC_BATCH, FFN_CONV - 1, D_FF), 1.0),
        page_table=page_table,
        norm_mix=1.0 + nrm((DEPTH, D_MODEL), 0.01),
        norm_ffn=1.0 + nrm((DEPTH, D_MODEL), 0.01),
        norm_final=1.0 + nrm((D_MODEL,), 0.01),
        w_in_even=nrm((N_EVEN, D_MODEL, ATTN_IN), D_MODEL ** -0.5),
        w_out_even=nrm((N_EVEN, H_A * E_A + W_B, D_MODEL), (H_A * E_A + W_B) ** -0.5),
        lambda_q1=nrm((N_EVEN, DH_A), 0.1),
        lambda_k1=nrm((N_EVEN, DH_A), 0.1),
        lambda_q2=nrm((N_EVEN, DH_A), 0.1),
        lambda_k2=nrm((N_EVEN, DH_A), 0.1),
        subln_w=1.0 + nrm((N_EVEN, E_A), 0.01),
        sconv_w=nrm((N_EVEN, SCONV, W_B), SCONV ** -0.5),
        w_in_odd=nrm((N_ODD, D_MODEL, GDN_IN), D_MODEL ** -0.5),
        gdn_conv_w=nrm((N_ODD, GDN_CONV, QKV_C), GDN_CONV ** -0.5),
        a_log=a_log,
        dt_bias=dt_bias,
        gdn_norm_w=1.0 + nrm((N_ODD, DV_C), 0.01),
        w_out_odd=nrm((N_ODD, VDIM_C, D_MODEL), VDIM_C ** -0.5),
        w_up=nrm((DEPTH, D_MODEL, 2 * D_FF), D_MODEL ** -0.5),
        ffn_conv_w=nrm((DEPTH, FFN_CONV, D_FF), FFN_CONV ** -0.5),
        ffn_conv_b=nrm((DEPTH, D_FF), 0.01),
        w_down=nrm((DEPTH, D_FF, D_MODEL), D_FF ** -0.5),
    )


def reference(x_prompt, x_sample, cache_k, cache_v, state_sconv, state_gdn_conv, state_gdn,
              state_ffn_conv, page_table, norm_mix, norm_ffn, norm_final, w_in_even, w_out_even,
              lambda_q1, lambda_k1, lambda_q2, lambda_k2, subln_w, sconv_w, w_in_odd, gdn_conv_w,
              a_log, dt_bias, gdn_norm_w, w_out_odd, w_up, ffn_conv_w, ffn_conv_b, w_down):
    pos_p = jnp.arange(SEQ)
    pos_s = PAST_LEN + jnp.arange(DEC_SEQ)
    xp, xs = x_prompt, x_sample
    kp, vp, ks, vs, scp, scs = [], [], [], [], [], []
    gcp, gcs, gsp, gss, fcp, fcs = [], [], [], [], [], []
    for l in range(DEPTH):
        if l % 2 == 0:
            e = l // 2
            lam_init = 0.8 - 0.6 * math.exp(-0.3 * l)
            lam = diff_lambda(lambda_q1[e], lambda_k1[e], lambda_q2[e], lambda_k2[e], lam_init)
            q, k, v, bg, cx = even_project(rmsnorm(xp, norm_mix[l]), pos_p, w_in_even[e])
            att = diff_attn_prompt(q, k, v, lam)
            hist = jnp.concatenate([jnp.zeros((BATCH, SCONV - 1, W_B), cx.dtype), cx], axis=1)
            xp = xp + even_output(att, bg, hist, lam_init, subln_w[e], sconv_w[e], w_out_even[e])
            kp.append(k.reshape(BATCH, SEQ, H_A, E_A))
            vp.append(v)
            scp.append(hist[:, -(SCONV - 1):])
            q, k, v, bg, cx = even_project(rmsnorm(xs, norm_mix[l]), pos_s, w_in_even[e])
            att = diff_attn_sample(q, k, v, cache_k[e, page_table], cache_v[e, page_table], lam)
            hist = jnp.concatenate([state_sconv[e].astype(cx.dtype), cx], axis=1)
            xs = xs + even_output(att, bg, hist, lam_init, subln_w[e], sconv_w[e], w_out_even[e])
            ks.append(k.reshape(DEC_BATCH, DEC_SEQ, H_A, E_A))
            vs.append(v)
            scs.append(hist[:, -(SCONV - 1):])
        else:
            o = l // 2
            gp = (w_in_odd[o], gdn_conv_w[o], a_log[o], dt_bias[o], gdn_norm_w[o], w_out_odd[o])
            y, c_new, s_new = gdn_mixer(rmsnorm(xp, norm_mix[l]),
                                        jnp.zeros((BATCH, GDN_CONV - 1, QKV_C), xp.dtype),
                                        jnp.zeros((BATCH, H_C, DK_C, DV_C), jnp.float32),
                                        *gp, chunked=True)
            xp = xp + y
            gcp.append(c_new)
            gsp.append(s_new)
            y, c_new, s_new = gdn_mixer(rmsnorm(xs, norm_mix[l]), state_gdn_conv[o], state_gdn[o],
                                        *gp, chunked=False)
            xs = xs + y
            gcs.append(c_new)
            gss.append(s_new)
        fp = (w_up[l], ffn_conv_w[l], ffn_conv_b[l], w_down[l])
        y, c_new = conv_ffn(rmsnorm(xp, norm_ffn[l]), jnp.zeros((BATCH, FFN_CONV - 1, D_FF), xp.dtype), *fp)
        xp = xp + y
        fcp.append(c_new)
        y, c_new = conv_ffn(rmsnorm(xs, norm_ffn[l]), state_ffn_conv[l], *fp)
        xs = xs + y
        fcs.append(c_new)
    y_prompt = rmsnorm(xp, norm_final)
    y_sample = rmsnorm(xs, norm_final)
    return (y_prompt, y_sample,
            jnp.stack(kp), jnp.stack(vp), jnp.stack(ks), jnp.stack(vs),
            jnp.stack(scp), jnp.stack(scs),
            jnp.stack(gcp), jnp.stack(gcs), jnp.stack(gsp), jnp.stack(gss),
            jnp.stack(fcp), jnp.stack(fcs))
```

```python
import functools
import math

import jax
import jax.numpy as jnp
from jax import lax
from jax.experimental import pallas as pl
from jax.experimental.pallas import tpu as pltpu

F32 = jnp.float32
BF16 = jnp.bfloat16

EPS = 1e-6
SUBLN_EPS = 1e-5
ROPE_THETA = 500000.0
GDN_CHUNK = 64
V7X_VMEM_LIMIT = 56 * 1024 * 1024
LANES = 128
HIST_ROWS = 8
BF16_ROWS = 16


def _cparams(*sem):
    return pltpu.CompilerParams(dimension_semantics=sem, vmem_limit_bytes=V7X_VMEM_LIMIT)


def _const_spec(shape):
    nd = len(shape)
    return pl.BlockSpec(shape, lambda i: (0,) * nd, pipeline_mode=pl.Buffered(1))


def _rms(x, g, eps):
    return x * lax.rsqrt(jnp.mean(x * x, axis=-1, keepdims=True) + eps) * g


def _sigmoid(x):
    return 1.0 / (1.0 + jnp.exp(-x))


def _silu(x):
    return x * _sigmoid(x)


def _softplus(x):
    return jnp.maximum(x, 0.0) + jnp.log1p(jnp.exp(-jnp.abs(x)))


def _dot(a, b):
    return jnp.dot(a, b, preferred_element_type=F32)


def _dot_nt(a, b):
    return lax.dot_general(a, b, (((1,), (1,)), ((), ())), preferred_element_type=F32)


def _dot_tn(a, b):
    return lax.dot_general(a, b, (((0,), (0,)), ((), ())), preferred_element_type=F32)


def _split2(a):
    hi = a.astype(BF16)
    lo = (a - hi.astype(F32)).astype(BF16)
    return hi, lo


def _split3(a):
    hi = a.astype(BF16)
    r = a - hi.astype(F32)
    mid = r.astype(BF16)
    lo = (r - mid.astype(F32)).astype(BF16)
    return hi, mid, lo


def _dot3(a, b):
    ah, al = _split2(a)
    bh, bl = _split2(b)
    return _dot(ah, bh) + (_dot(ah, bl) + _dot(al, bh))


def _even_proj_kernel(x_ref, g_ref, w_ref, c_ref, s1_ref, s2_ref,
                      q_ref, k_ref, v_ref, kb_ref, vb_ref, bg_ref, cx_ref,
                      *, n_heads, e_a, w_b, half, qscale):
    xn = _rms(x_ref[...], g_ref[...], EPS).astype(BF16)
    n = n_heads * e_a
    c, s1, s2 = c_ref[...], s1_ref[...], s2_ref[...]

    def rope(t):
        return t * c + pltpu.roll(t, e_a - half, 1) * s1 + pltpu.roll(t, half, 1) * s2

    hqk = _dot(xn, w_ref[:, :2 * n])
    for h in range(n_heads):
        sl = slice(h * e_a, (h + 1) * e_a)
        q_ref[:, sl] = (rope(hqk[:, sl]) * qscale).astype(BF16)
        kr = rope(hqk[:, n + h * e_a:n + (h + 1) * e_a])
        k_ref[:, sl] = kr
        kb_ref[:, sl] = kr.astype(BF16)
    hv = _dot(xn, w_ref[:, 2 * n:3 * n])
    v_ref[...] = hv
    vb_ref[...] = hv.astype(BF16)
    o = 3 * n
    bg_ref[...] = _dot(xn, w_ref[:, o:o + w_b])
    hc = _dot(xn, w_ref[:, o + w_b:o + 3 * w_b])
    cx_ref[...] = hc[:, :w_b] * hc[:, w_b:]


def _even_proj(x, g, w, rope_tabs, *, tm, n_heads, e_a, w_b, half):
    t, d = x.shape
    n = n_heads * e_a
    c, s1, s2 = rope_tabs
    row = lambda width: pl.BlockSpec((tm, width), lambda i: (i, 0))
    kern = functools.partial(_even_proj_kernel, n_heads=n_heads, e_a=e_a, w_b=w_b, half=half,
                             qscale=(e_a // 2) ** -0.5)
    return pl.pallas_call(
        kern,
        grid=(t // tm,),
        in_specs=[row(d), _const_spec((1, d)), _const_spec(w.shape), row(e_a), row(e_a), row(e_a)],
        out_specs=[row(n), row(n), row(n), row(n), row(n), row(w_b), row(w_b)],
        out_shape=[jax.ShapeDtypeStruct((t, n), BF16), jax.ShapeDtypeStruct((t, n), F32),
                   jax.ShapeDtypeStruct((t, n), F32), jax.ShapeDtypeStruct((t, n), BF16),
                   jax.ShapeDtypeStruct((t, n), BF16), jax.ShapeDtypeStruct((t, w_b), F32),
                   jax.ShapeDtypeStruct((t, w_b), F32)],
        compiler_params=_cparams("arbitrary"),
        name="even_proj",
    )(x, g, w, c, s1, s2)


def _rope_tables(pos, e_a, rot, dtype=F32):
    half = rot // 2
    dh = e_a // 2
    inv = ROPE_THETA ** (-jnp.arange(half, dtype=F32) * 2.0 / rot)
    ang = pos.astype(F32)[:, None] * inv[None, :]
    cos, sin = jnp.cos(ang).astype(dtype), jnp.sin(ang).astype(dtype)
    t = pos.shape[0]
    ones = jnp.ones((t, dh - rot), dtype)
    zeros_h = jnp.zeros((t, half), dtype)
    zeros_r = jnp.zeros((t, dh - rot), dtype)
    c = jnp.concatenate([cos, cos, ones], axis=1)
    s1 = jnp.concatenate([-sin, zeros_h, zeros_r], axis=1)
    s2 = jnp.concatenate([zeros_h, sin, zeros_r], axis=1)
    rep = lambda a: jnp.concatenate([a, a], axis=1)
    return rep(c), rep(s1), rep(s2)


def _lambda(lp_ref, lam_init):
    lp = lp_ref[...]
    a = jnp.sum(lp[0:1] * lp[1:2], axis=-1, keepdims=True)
    b = jnp.sum(lp[2:3] * lp[3:4], axis=-1, keepdims=True)
    return jnp.exp(a) - jnp.exp(b) + lam_init


def _attn_prompt_kernel(q_ref, k_ref, v_ref, lp_ref, sw_ref, o_ref, *, blk, dh, lam_init):
    i = pl.program_id(2)
    q = q_ref[...].astype(F32)
    lane = lax.broadcasted_iota(jnp.int32, q.shape, 1)
    qs = jnp.concatenate([jnp.where(lane < dh, q, 0.0), jnp.where(lane >= dh, q, 0.0)],
                         axis=0).astype(BF16)

    def step(kj, vj, carry, mask):
        m, l, acc = carry
        s = _dot_nt(qs, kj)
        if mask is not None:
            s = jnp.where(mask, s, jnp.finfo(F32).min)
        m_new = jnp.maximum(m, jnp.max(s, axis=-1, keepdims=True))
        p = jnp.exp(s - m_new)
        alpha = jnp.exp(m - m_new)
        l = alpha * l + jnp.sum(p, axis=-1, keepdims=True)
        acc = alpha * acc + _dot(p.astype(BF16), vj)
        return m_new, l, acc

    def body(j, carry):
        off = pl.multiple_of(j * blk, blk)
        return step(k_ref[pl.ds(off, blk), :], v_ref[pl.ds(off, blk), :], carry, None)

    init = (jnp.full((2 * blk, 1), -1e30, F32), jnp.zeros((2 * blk, 1), F32),
            jnp.zeros((2 * blk, q.shape[1]), F32))
    carry = lax.fori_loop(0, i, body, init)
    off = pl.multiple_of(i * blk, blk)
    r = lax.broadcasted_iota(jnp.int32, (2 * blk, blk), 0)
    cidx = lax.broadcasted_iota(jnp.int32, (2 * blk, blk), 1)
    r = jnp.where(r >= blk, r - blk, r)
    _, l, acc = step(k_ref[pl.ds(off, blk), :], v_ref[pl.ds(off, blk), :], carry, cidx <= r)
    o = acc / l
    lam = _lambda(lp_ref, lam_init)
    att = o[:blk] - lam * o[blk:]
    o_ref[...] = (_rms(att, sw_ref[...], SUBLN_EPS) * (1.0 - lam_init)).astype(o_ref.dtype)


def _attn_prompt(q, kb, vb, lam_p, subln_w, *, batch, seq, n_heads, e_a, blk, lam_init):
    t = q.shape[0]
    nq = seq // blk
    kern = functools.partial(_attn_prompt_kernel, blk=blk, dh=e_a // 2, lam_init=lam_init)
    return pl.pallas_call(
        kern,
        grid=(batch, n_heads, nq),
        in_specs=[pl.BlockSpec((blk, e_a), lambda b, h, i: (b * nq + i, h)),
                  pl.BlockSpec((seq, e_a), lambda b, h, i: (b, h)),
                  pl.BlockSpec((seq, e_a), lambda b, h, i: (b, h)),
                  pl.BlockSpec(lam_p.shape, lambda b, h, i: (0, 0)),
                  pl.BlockSpec(subln_w.shape, lambda b, h, i: (0, 0))],
        out_specs=pl.BlockSpec((blk, e_a), lambda b, h, i: (b * nq + i, h)),
        out_shape=jax.ShapeDtypeStruct((t, n_heads * e_a), BF16),
        compiler_params=_cparams("arbitrary", "arbitrary", "arbitrary"),
        name="attn_prompt",
    )(q, kb, vb, lam_p, subln_w)


def _attn_sample_kernel(pt_ref, q_ref, kn_ref, vn_ref, lp_ref, sw_ref, *rest,
                        n_pages, n_heads, e_a, lam_init):
    del pt_ref
    k_refs = rest[:n_pages]
    v_refs = rest[n_pages:2 * n_pages]
    o_ref = rest[2 * n_pages]
    dh = e_a // 2
    n = n_heads * e_a
    nm = max(2 * n_heads, BF16_ROWS)
    q = q_ref[0]
    rowi = lax.broadcasted_iota(jnp.int32, (nm, n), 0)
    lane = lax.broadcasted_iota(jnp.int32, (nm, n), 1)
    sel = (lane // dh) == rowi
    qm = jnp.where(sel, jnp.broadcast_to(q.astype(F32), (nm, n)), 0.0).astype(BF16)
    m = jnp.full((nm, 1), -1e30, F32)
    l = jnp.zeros((nm, 1), F32)
    acc = jnp.zeros((nm, n), F32)
    for p in range(n_pages):
        kp = k_refs[p][0].astype(BF16)
        vp = v_refs[p][0].astype(BF16)
        s = _dot_nt(qm, kp)
        m_new = jnp.maximum(m, jnp.max(s, axis=-1, keepdims=True))
        pr = jnp.exp(s - m_new)
        alpha = jnp.exp(m - m_new)
        l = alpha * l + jnp.sum(pr, axis=-1, keepdims=True)
        acc = alpha * acc + _dot(pr.astype(BF16), vp)
        m = m_new
    kn = kn_ref[0].astype(BF16).astype(F32)
    vn = vn_ref[0].astype(BF16).astype(F32)
    s = jnp.sum(qm.astype(F32) * kn, axis=-1, keepdims=True)
    m_new = jnp.maximum(m, s)
    pr = jnp.exp(s - m_new)
    alpha = jnp.exp(m - m_new)
    l = alpha * l + pr
    acc = alpha * acc + pr.astype(BF16).astype(F32) * vn
    o = acc / l
    lam = _lambda(lp_ref, lam_init)
    sw = sw_ref[...]
    for h in range(n_heads):
        sl = slice(h * e_a, (h + 1) * e_a)
        att = o[2 * h:2 * h + 1, sl] - lam * o[2 * h + 1:2 * h + 2, sl]
        o_ref[0, :, sl] = (_rms(att, sw, SUBLN_EPS) * (1.0 - lam_init)).astype(o_ref.dtype)


def _attn_sample(q, k_new, v_new, cache_k, cache_v, layer, page_table, lam_p, subln_w,
                 *, n_heads, e_a, lam_init):
    b, n = q.shape
    n_pages = page_table.shape[1]
    n_phys, page = cache_k.shape[1], cache_k.shape[2]
    ck = cache_k.reshape(cache_k.shape[0] * n_phys, page, n)
    cv = cache_v.reshape(cache_v.shape[0] * n_phys, page, n)
    base = layer * n_phys
    tok = pl.BlockSpec((1, 1, n), lambda i, pt: (i, 0, 0))
    page_specs = [pl.BlockSpec((1, page, n), functools.partial(
        lambda i, pt, p: (base + pt[i, p], 0, 0), p=p)) for p in range(n_pages)]
    kern = functools.partial(_attn_sample_kernel, n_pages=n_pages, n_heads=n_heads, e_a=e_a,
                             lam_init=lam_init)
    grid_spec = pltpu.PrefetchScalarGridSpec(
        num_scalar_prefetch=1,
        grid=(b,),
        in_specs=[tok, tok, tok,
                  pl.BlockSpec(lam_p.shape, lambda i, pt: (0, 0)),
                  pl.BlockSpec(subln_w.shape, lambda i, pt: (0, 0))] + page_specs + page_specs,
        out_specs=tok,
    )
    out = pl.pallas_call(
        kern,
        grid_spec=grid_spec,
        out_shape=jax.ShapeDtypeStruct((b, 1, n), BF16),
        compiler_params=_cparams("arbitrary"),
        name="attn_sample",
    )(page_table, q.reshape(b, 1, n), k_new.reshape(b, 1, n), v_new.reshape(b, 1, n), lam_p, subln_w,
      *([ck] * n_pages), *([cv] * n_pages))
    return out.reshape(b, n)


def _seq_hist(buf_ref, cur, tiles_per_seq, tm):
    first = (pl.program_id(0) % tiles_per_seq) == 0

    @pl.when(first)
    def _():
        buf_ref[0:HIST_ROWS, :] = jnp.zeros((HIST_ROWS, buf_ref.shape[1]), buf_ref.dtype)

    @pl.when(jnp.logical_not(first))
    def _():
        buf_ref[0:HIST_ROWS, :] = buf_ref[tm:tm + HIST_ROWS, :]

    buf_ref[HIST_ROWS:HIST_ROWS + tm, :] = cur


def _even_out_seq_kernel(att_ref, bg_ref, cx_ref, x_ref, cw_ref, w_ref, o_ref, buf_ref,
                         *, tm, tiles_per_seq, n_att):
    cx = cx_ref[...]
    _seq_hist(buf_ref, cx, tiles_per_seq, tm)
    cw = cw_ref[...]
    width = cw.shape[0]
    conv = cw[width - 1:width] * cx
    for j in range(width - 1):
        conv = conv + cw[j:j + 1] * buf_ref[pl.ds(HIST_ROWS - (width - 1) + j, tm), :]
    yb = (bg_ref[...] * conv).astype(BF16)
    o_ref[...] = x_ref[...] + _dot(att_ref[...], w_ref[:n_att, :]) + _dot(yb, w_ref[n_att:, :])


def _even_out_step_kernel(att_ref, bg_ref, cx_ref, x_ref, cw_ref, w_ref, *rest, n_att):
    hist_refs, o_ref = rest[:-1], rest[-1]
    cw = cw_ref[...]
    width = cw.shape[0]
    conv = cw[width - 1:width] * cx_ref[...]
    for j in range(width - 1):
        conv = conv + cw[j:j + 1] * hist_refs[j][...]
    yb = (bg_ref[...] * conv).astype(BF16)
    o_ref[...] = x_ref[...] + _dot(att_ref[...], w_ref[:n_att, :]) + _dot(yb, w_ref[n_att:, :])


def _even_out(att, bg, cx, x, conv_w, w_out, *, tm, seq=None, hist=None):
    t, d = x.shape
    n_att, w_b = att.shape[1], bg.shape[1]
    row = lambda width: pl.BlockSpec((tm, width), lambda i: (i, 0))
    common = [row(n_att), row(w_b), row(w_b), row(d), _const_spec(conv_w.shape), _const_spec(w_out.shape)]
    if hist is None:
        kern = functools.partial(_even_out_seq_kernel, tm=tm, tiles_per_seq=seq // tm, n_att=n_att)
        extra, extra_specs = [], []
        scratch = [pltpu.VMEM((tm + HIST_ROWS, w_b), F32)]
    else:
        kern = functools.partial(_even_out_step_kernel, n_att=n_att)
        extra, extra_specs = list(hist), [row(w_b)] * len(hist)
        scratch = []
    return pl.pallas_call(
        kern,
        grid=(t // tm,),
        in_specs=common + extra_specs,
        out_specs=row(d),
        out_shape=jax.ShapeDtypeStruct((t, d), F32),
        scratch_shapes=scratch,
        compiler_params=_cparams("arbitrary"),
        name="even_out",
    )(att, bg, cx, x, conv_w, w_out, *extra)


def _gdn_post_conv(conv, xn, wz_ref, wba_ref, al_ref, dtb_ref, q_ref, k_ref, v_ref, z_ref, bga_ref,
                   *, n_heads, dk):
    c = _silu(conv)
    kdim = n_heads * dk
    for h in range(n_heads):
        sl = slice(h * dk, (h + 1) * dk)
        tq = c[:, sl]
        q_ref[:, sl] = tq * lax.rsqrt(jnp.sum(tq * tq, axis=-1, keepdims=True) + EPS) * (dk ** -0.5)
        tk = c[:, kdim + h * dk:kdim + (h + 1) * dk]
        k_ref[:, sl] = tk * lax.rsqrt(jnp.sum(tk * tk, axis=-1, keepdims=True) + EPS)
    v_ref[...] = c[:, 2 * kdim:]
    z_ref[...] = _dot(xn, wz_ref[...])
    ba = _dot(xn, wba_ref[...])
    beta = _sigmoid(ba)
    g = -jnp.exp(al_ref[...]) * _softplus(ba + dtb_ref[...])
    lane = lax.broadcasted_iota(jnp.int32, ba.shape, 1)
    bga_ref[...] = jnp.where(lane < n_heads, beta, g)


def _gdn_proj_seq_kernel(x_ref, g_ref, wqkv_ref, wz_ref, wba_ref, cw_ref, al_ref, dtb_ref,
                         q_ref, k_ref, v_ref, z_ref, bga_ref, tail_ref, buf_ref,
                         *, tm, tiles_per_seq, n_heads, dk):
    xn = _rms(x_ref[...], g_ref[...], EPS).astype(BF16)
    hm = _dot(xn, wqkv_ref[...])
    _seq_hist(buf_ref, hm, tiles_per_seq, tm)
    cw = cw_ref[...]
    width = cw.shape[0]
    conv = cw[width - 1:width] * hm
    for j in range(width - 1):
        conv = conv + cw[j:j + 1] * buf_ref[pl.ds(HIST_ROWS - (width - 1) + j, tm), :]
    tail_ref[0] = buf_ref[tm:tm + HIST_ROWS, :]
    _gdn_post_conv(conv, xn, wz_ref, wba_ref, al_ref, dtb_ref, q_ref, k_ref, v_ref, z_ref, bga_ref,
                   n_heads=n_heads, dk=dk)


def _gdn_proj_step_kernel(x_ref, g_ref, wqkv_ref, wz_ref, wba_ref, cw_ref, al_ref, dtb_ref, *rest,
                          n_heads, dk, n_hist):
    hist_refs = rest[:n_hist]
    q_ref, k_ref, v_ref, z_ref, bga_ref, raw_ref = rest[n_hist:]
    xn = _rms(x_ref[...], g_ref[...], EPS).astype(BF16)
    hm = _dot(xn, wqkv_ref[...])
    raw_ref[...] = hm
    cw = cw_ref[...]
    width = cw.shape[0]
    conv = cw[width - 1:width] * hm
    for j in range(width - 1):
        conv = conv + cw[j:j + 1] * hist_refs[j][...]
    _gdn_post_conv(conv, xn, wz_ref, wba_ref, al_ref, dtb_ref, q_ref, k_ref, v_ref, z_ref, bga_ref,
                   n_heads=n_heads, dk=dk)


def _gdn_proj(x, g, wqkv, wz, wba, conv_w, al_pad, dtb_pad, *, tm, n_heads, dk, seq=None, hist=None):
    t, d = x.shape
    qkv_c = wqkv.shape[1]
    vdim = wz.shape[1]
    row = lambda width: pl.BlockSpec((tm, width), lambda i: (i, 0))
    common = [row(d), _const_spec((1, d)), _const_spec(wqkv.shape), _const_spec(wz.shape),
              _const_spec(wba.shape), _const_spec(conv_w.shape), _const_spec(al_pad.shape),
              _const_spec(dtb_pad.shape)]
    outs = [row(vdim), row(vdim), row(vdim), row(vdim), row(LANES)]
    out_shapes = [jax.ShapeDtypeStruct((t, vdim), F32)] * 4 + [jax.ShapeDtypeStruct((t, LANES), F32)]
    if hist is None:
        tps = seq // tm
        kern = functools.partial(_gdn_proj_seq_kernel, tm=tm, tiles_per_seq=tps, n_heads=n_heads, dk=dk)
        extra, extra_specs = [], []
        outs.append(pl.BlockSpec((1, HIST_ROWS, qkv_c), lambda i: (i // tps, 0, 0)))
        out_shapes.append(jax.ShapeDtypeStruct((t // seq, HIST_ROWS, qkv_c), F32))
        scratch = [pltpu.VMEM((tm + HIST_ROWS, qkv_c), F32)]
    else:
        hist_arr, n_hist = hist
        kern = functools.partial(_gdn_proj_step_kernel, n_heads=n_heads, dk=dk, n_hist=n_hist)
        extra = [hist_arr] * n_hist
        extra_specs = [pl.BlockSpec((tm, qkv_c), functools.partial(lambda i, j: (i, j), j=j))
                       for j in range(n_hist)]
        outs.append(row(qkv_c))
        out_shapes.append(jax.ShapeDtypeStruct((t, qkv_c), F32))
        scratch = []
    return pl.pallas_call(
        kern,
        grid=(t // tm,),
        in_specs=common + extra_specs,
        out_specs=outs,
        out_shape=out_shapes,
        scratch_shapes=scratch,
        compiler_params=_cparams("arbitrary"),
        name="gdn_proj",
    )(x, g, wqkv, wz, wba, conv_w, al_pad, dtb_pad, *extra)


def _gdn_prep_kernel(q_ref, k_ref, v_ref, bga_ref, u_ref, w_ref, qe_ref, ke_ref, qk_ref, gc_ref,
                     *, n_heads, dk, chunk):
    bga = bga_ref[...]
    row = lax.broadcasted_iota(jnp.int32, (chunk, chunk), 0)
    col = lax.broadcasted_iota(jnp.int32, (chunk, chunk), 1)
    incl = row >= col
    strict = row > col
    tril = jnp.where(incl, 1.0, 0.0).astype(BF16)
    b3 = _split3(bga)
    gc = _dot(tril, b3[0]) + (_dot(tril, b3[1]) + _dot(tril, b3[2]))
    gc_ref[...] = gc
    g3 = _split3(gc)
    selr = lax.broadcasted_iota(jnp.int32, (BF16_ROWS, LANES), 0)
    sell = lax.broadcasted_iota(jnp.int32, (BF16_ROWS, LANES), 1)
    sel = jnp.where(sell == selr + n_heads, 1.0, 0.0).astype(BF16)
    grow = _dot_nt(sel, g3[0]) + (_dot_nt(sel, g3[1]) + _dot_nt(sel, g3[2]))
    eye = jnp.where(row == col, 1.0, 0.0)
    for h in range(n_heads):
        sl = slice(h * dk, (h + 1) * dk)
        qh, kh, vh = q_ref[:, sl], k_ref[:, sl], v_ref[:, sl]
        gcol = gc[:, n_heads + h:n_heads + h + 1]
        bcol = bga[:, h:h + 1]
        diff = gcol - grow[h:h + 1, :]
        decay = jnp.where(incl, jnp.exp(jnp.where(incl, diff, 0.0)), 0.0)
        kb = kh * bcol
        khb = kh.astype(BF16)
        lmat = jnp.where(strict, _dot_nt(kb.astype(BF16), khb) * decay, 0.0)
        t_inv = eye - jnp.where((row >> 1) == (col >> 1), lmat, 0.0)
        s = 1
        while 2 * (1 << s) <= chunk:
            joins = jnp.logical_and((row >> (s + 1)) == (col >> (s + 1)), (row >> s) != (col >> s))
            nmat = jnp.where(joins, lmat, 0.0)
            t_inv = t_inv - _dot3(_dot3(t_inv, nmat), t_inv)
            s += 1
        egc = jnp.exp(gcol)
        u_ref[:, sl] = _dot3(t_inv, vh * bcol)
        w_ref[:, sl] = _dot3(t_inv, kb * egc).astype(BF16)
        qk = jnp.where(incl, _dot_nt(qh.astype(BF16), khb) * decay, 0.0)
        qk_ref[:, sl] = jnp.concatenate([qk, jnp.zeros((chunk, dk - chunk), F32)], axis=1).astype(BF16)
        qe_ref[:, sl] = (qh * egc).astype(BF16)
        glast = gc[chunk - 1:chunk, n_heads + h:n_heads + h + 1]
        ke_ref[:, sl] = (kh * jnp.exp(glast - gcol)).astype(BF16)


def _gdn_prep(q, k, v, bga, *, n_heads, dk):
    t, vdim = q.shape
    chunk = GDN_CHUNK
    row = lambda width: pl.BlockSpec((chunk, width), lambda i: (i, 0))
    kern = functools.partial(_gdn_prep_kernel, n_heads=n_heads, dk=dk, chunk=chunk)
    return pl.pallas_call(
        kern,
        grid=(t // chunk,),
        in_specs=[row(vdim), row(vdim), row(vdim), row(LANES)],
        out_specs=[row(vdim), row(vdim), row(vdim), row(vdim), row(vdim), row(LANES)],
        out_shape=[jax.ShapeDtypeStruct((t, vdim), F32)] + [jax.ShapeDtypeStruct((t, vdim), BF16)] * 4
                  + [jax.ShapeDtypeStruct((t, LANES), F32)],
        compiler_params=_cparams("arbitrary"),
        name="gdn_prep",
    )(q, k, v, bga)


def _gdn_scan_kernel(u_ref, w_ref, qe_ref, ke_ref, qk_ref, gc_ref, o_ref, sout_ref, s_ref,
                     *, n_heads, dk, chunk, n_sub):
    j = pl.program_id(1)

    @pl.when(j == 0)
    def _():
        s_ref[...] = jnp.zeros(s_ref.shape, s_ref.dtype)

    def body(c, carry):
        r0 = pl.multiple_of(c * chunk, chunk)
        rows = pl.ds(r0, chunk)
        glast = jnp.exp(gc_ref[pl.ds(r0 + chunk - 1, 1), :])
        for h in range(n_heads):
            sl = slice(h * dk, (h + 1) * dk)
            st = s_ref[h]
            sb = st.astype(BF16)
            v_new = u_ref[rows, sl] - _dot(w_ref[rows, sl], sb)
            vb = v_new.astype(BF16)
            qk = qk_ref[rows, sl]
            o_ref[rows, sl] = _dot(qe_ref[rows, sl], sb) + _dot(qk[:, :chunk], vb)
            s_ref[h] = st * glast[:, n_heads + h:n_heads + h + 1] + _dot_tn(ke_ref[rows, sl], vb)
        return carry

    lax.fori_loop(0, n_sub, body, 0)

    @pl.when(j == pl.num_programs(1) - 1)
    def _():
        sout_ref[0] = s_ref[...]


def _gdn_scan(u, w, qe, ke, qk, gc, *, batch, seq, n_heads, dk, rows):
    t, vdim = u.shape
    chunk = GDN_CHUNK
    nb = seq // rows
    row = lambda width: pl.BlockSpec((rows, width), lambda b, j: (b * nb + j, 0))
    kern = functools.partial(_gdn_scan_kernel, n_heads=n_heads, dk=dk, chunk=chunk, n_sub=rows // chunk)
    return pl.pallas_call(
        kern,
        grid=(batch, nb),
        in_specs=[row(vdim)] * 5 + [row(LANES)],
        out_specs=[row(vdim), pl.BlockSpec((1, n_heads, dk, dk), lambda b, j: (b, 0, 0, 0))],
        out_shape=[jax.ShapeDtypeStruct((t, vdim), F32),
                   jax.ShapeDtypeStruct((batch, n_heads, dk, dk), F32)],
        scratch_shapes=[pltpu.VMEM((n_heads, dk, dk), F32)],
        compiler_params=_cparams("arbitrary", "arbitrary"),
        name="gdn_scan",
    )(u, w, qe, ke, qk, gc)


def _gdn_step_kernel(q_ref, k_ref, v_ref, bga_ref, s_ref, o_ref, sout_ref, *, n_heads, dk, nb):
    pad = jnp.zeros((dk - n_heads, dk), F32)

    def body(b, carry):
        qt = jnp.concatenate([q_ref[b], pad], axis=0).T
        kt = jnp.concatenate([k_ref[b], pad], axis=0).T
        vb = v_ref[b]
        bg = bga_ref[pl.ds(b, 1), :]
        eg = jnp.exp(bg)
        rows = []
        for h in range(n_heads):
            st = s_ref[b, h] * eg[:, n_heads + h:n_heads + h + 1]
            kc = kt[:, h:h + 1]
            ks = jnp.sum(kc * st, axis=0, keepdims=True)
            delta = (vb[h:h + 1, :] - ks) * bg[:, h:h + 1]
            st = st + kc * delta
            sout_ref[b, h] = st
            rows.append(jnp.sum(qt[:, h:h + 1] * st, axis=0, keepdims=True))
        o_ref[b] = jnp.concatenate(rows, axis=0)
        return carry

    lax.fori_loop(0, nb, body, 0)


def _gdn_step(q, k, v, bga, state, *, n_heads, dk, nb):
    b = q.shape[0]
    q3, k3, v3 = (a.reshape(b, n_heads, dk) for a in (q, k, v))
    vec = pl.BlockSpec((nb, n_heads, dk), lambda i: (i, 0, 0))
    st = pl.BlockSpec((nb, n_heads, dk, dk), lambda i: (i, 0, 0, 0))
    kern = functools.partial(_gdn_step_kernel, n_heads=n_heads, dk=dk, nb=nb)
    o, s_new = pl.pallas_call(
        kern,
        grid=(b // nb,),
        in_specs=[vec, vec, vec, pl.BlockSpec((nb, LANES), lambda i: (i, 0)), st],
        out_specs=[vec, st],
        out_shape=[jax.ShapeDtypeStruct((b, n_heads, dk), F32), jax.ShapeDtypeStruct(state.shape, F32)],
        compiler_params=_cparams("arbitrary"),
        name="gdn_step",
    )(q3, k3, v3, bga, state)
    return o.reshape(b, n_heads * dk), s_new


def _gdn_out_kernel(o_ref, z_ref, x_ref, nw_ref, w_ref, y_ref, *, n_heads, dk):
    nw = nw_ref[...]
    parts = []
    for h in range(n_heads):
        sl = slice(h * dk, (h + 1) * dk)
        parts.append((_rms(o_ref[:, sl], nw, EPS) * _silu(z_ref[:, sl])).astype(BF16))
    y_ref[...] = x_ref[...] + _dot(jnp.concatenate(parts, axis=1), w_ref[...])


def _gdn_out(o, z, x, norm_w, w_out, *, tm, n_heads, dk):
    t, d = x.shape
    vdim = o.shape[1]
    row = lambda width: pl.BlockSpec((tm, width), lambda i: (i, 0))
    kern = functools.partial(_gdn_out_kernel, n_heads=n_heads, dk=dk)
    return pl.pallas_call(
        kern,
        grid=(t // tm,),
        in_specs=[row(vdim), row(vdim), row(d), _const_spec(norm_w.shape), _const_spec(w_out.shape)],
        out_specs=row(d),
        out_shape=jax.ShapeDtypeStruct((t, d), F32),
        compiler_params=_cparams("arbitrary"),
        name="gdn_out",
    )(o, z, x, norm_w, w_out)


def _ffn_tail(x, conv, up, cb_ref, wd_ref, gf_ref, o_ref, final_norm):
    act = (_silu(conv + cb_ref[...]) * up).astype(BF16)
    y = x + _dot(act, wd_ref[...])
    if final_norm:
        y = _rms(y, gf_ref[...], EPS)
    o_ref[...] = y


def _ffn_seq_kernel(x_ref, g_ref, wg_ref, wu_ref, cw_ref, cb_ref, wd_ref, gf_ref, o_ref, tail_ref, buf_ref,
                    *, tm, tiles_per_seq, final_norm):
    x = x_ref[...]
    xn = _rms(x, g_ref[...], EPS).astype(BF16)
    gate = _dot(xn, wg_ref[...])
    _seq_hist(buf_ref, gate, tiles_per_seq, tm)
    cw = cw_ref[...]
    width = cw.shape[0]
    conv = cw[width - 1:width] * gate
    for j in range(width - 1):
        conv = conv + cw[j:j + 1] * buf_ref[pl.ds(HIST_ROWS - (width - 1) + j, tm), :]
    tail_ref[0] = buf_ref[tm:tm + HIST_ROWS, :]
    _ffn_tail(x, conv, _dot(xn, wu_ref[...]), cb_ref, wd_ref, gf_ref, o_ref, final_norm)


def _ffn_step_kernel(x_ref, g_ref, wg_ref, wu_ref, cw_ref, cb_ref, wd_ref, gf_ref, *rest,
                     n_hist, final_norm):
    hist_refs = rest[:n_hist]
    o_ref, raw_ref = rest[n_hist:]
    x = x_ref[...]
    xn = _rms(x, g_ref[...], EPS).astype(BF16)
    gate = _dot(xn, wg_ref[...])
    raw_ref[...] = gate
    cw = cw_ref[...]
    width = cw.shape[0]
    conv = cw[width - 1:width] * gate
    for j in range(width - 1):
        conv = conv + cw[j:j + 1] * hist_refs[j][...]
    _ffn_tail(x, conv, _dot(xn, wu_ref[...]), cb_ref, wd_ref, gf_ref, o_ref, final_norm)


def _ffn(x, g, wg, wu, conv_w, conv_b, wd, gf, *, tm, final_norm, seq=None, hist=None):
    t, d = x.shape
    d_ff = wg.shape[1]
    row = lambda width: pl.BlockSpec((tm, width), lambda i: (i, 0))
    common = [row(d), _const_spec((1, d)), _const_spec(wg.shape), _const_spec(wu.shape),
              _const_spec(conv_w.shape), _const_spec(conv_b.shape), _const_spec(wd.shape),
              _const_spec((1, d))]
    if hist is None:
        tps = seq // tm
        kern = functools.partial(_ffn_seq_kernel, tm=tm, tiles_per_seq=tps, final_norm=final_norm)
        extra, extra_specs = [], []
        outs = [row(d), pl.BlockSpec((1, HIST_ROWS, d_ff), lambda i: (i // tps, 0, 0))]
        out_shapes = [jax.ShapeDtypeStruct((t, d), F32),
                      jax.ShapeDtypeStruct((t // seq, HIST_ROWS, d_ff), F32)]
        scratch = [pltpu.VMEM((tm + HIST_ROWS, d_ff), F32)]
    else:
        hist_arr, n_hist = hist
        kern = functools.partial(_ffn_step_kernel, n_hist=n_hist, final_norm=final_norm)
        extra = [hist_arr] * n_hist
        extra_specs = [pl.BlockSpec((tm, d_ff), functools.partial(lambda i, j: (i, j), j=j))
                       for j in range(n_hist)]
        outs = [row(d), row(d_ff)]
        out_shapes = [jax.ShapeDtypeStruct((t, d), F32), jax.ShapeDtypeStruct((t, d_ff), F32)]
        scratch = []
    return pl.pallas_call(
        kern,
        grid=(t // tm,),
        in_specs=common + extra_specs,
        out_specs=outs,
        out_shape=out_shapes,
        scratch_shapes=scratch,
        compiler_params=_cparams("arbitrary"),
        name="conv_ffn",
    )(x, g, wg, wu, conv_w, conv_b, wd, gf, *extra)


def _tile(n, pref):
    t = min(n, pref)
    while n % t:
        t //= 2
    return t


def kernel(x_prompt, x_sample, cache_k, cache_v, state_sconv, state_gdn_conv, state_gdn, state_ffn_conv, page_table, norm_mix, norm_ffn, norm_final, w_in_even, w_out_even, lambda_q1, lambda_k1, lambda_q2, lambda_k2, subln_w, sconv_w, w_in_odd, gdn_conv_w, a_log, dt_bias, gdn_norm_w, w_out_odd, w_up, ffn_conv_w, ffn_conv_b, w_down):
    batch, seq, d = x_prompt.shape
    dec_b = x_sample.shape[0]
    depth = norm_mix.shape[0]
    n_heads_a, e_a = cache_k.shape[3], cache_k.shape[4]
    n_att = n_heads_a * e_a
    rot = (e_a // 2) // 4
    w_b = state_sconv.shape[-1]
    sconv = sconv_w.shape[1]
    page = cache_k.shape[2]
    past_len = page_table.shape[1] * page
    n_heads_c, dk = state_gdn.shape[2], state_gdn.shape[3]
    kdim = n_heads_c * dk
    qkv_c = gdn_conv_w.shape[-1]
    gdn_conv = gdn_conv_w.shape[1]
    d_ff = ffn_conv_w.shape[-1]
    ffn_conv = ffn_conv_w.shape[1]
    t_p = batch * seq

    tm_p = _tile(seq, 256)
    blk = _tile(seq, 256)
    scan_rows = _tile(seq, 512)

    xp = x_prompt.reshape(t_p, d)
    xs = x_sample.reshape(dec_b, d)
    rope_p = _rope_tables(jnp.tile(jnp.arange(seq), batch), e_a, rot)
    rope_s = _rope_tables(jnp.full((dec_b,), past_len), e_a, rot)
    gf = norm_final.reshape(1, d)

    kp, vp, ks, vs, scp, scs = [], [], [], [], [], []
    gcp, gcs, gsp, gss, fcp, fcs = [], [], [], [], [], []
    for l in range(depth):
        gm = norm_mix[l].reshape(1, d)
        if l % 2 == 0:
            e = l // 2
            lam_init = 0.8 - 0.6 * math.exp(-0.3 * l)
            w_in = w_in_even[e].astype(BF16)
            w_out = w_out_even[e].astype(BF16)
            lam_p = jnp.stack([lambda_q1[e], lambda_k1[e], lambda_q2[e], lambda_k2[e]])
            sw = subln_w[e].reshape(1, e_a)
            proj = functools.partial(_even_proj, n_heads=n_heads_a, e_a=e_a, w_b=w_b, half=rot // 2)
            q, k, v, kb, vb, bg, cx = proj(xp, gm, w_in, rope_p, tm=tm_p)
            att = _attn_prompt(q, kb, vb, lam_p, sw, batch=batch, seq=seq, n_heads=n_heads_a, e_a=e_a,
                               blk=blk, lam_init=lam_init)
            xp = _even_out(att, bg, cx, xp, sconv_w[e], w_out, tm=tm_p, seq=seq)
            kp.append(k.reshape(batch, seq, n_heads_a, e_a))
            vp.append(v.reshape(batch, seq, n_heads_a, e_a))
            cx3 = cx.reshape(batch, seq, w_b)
            if seq >= sconv - 1:
                scp.append(cx3[:, seq - (sconv - 1):])
            else:
                scp.append(jnp.concatenate([jnp.zeros((batch, sconv - 1 - seq, w_b), F32), cx3], axis=1))
            q, k, v, kb, vb, bg, cx = proj(xs, gm, w_in, rope_s, tm=dec_b)
            att = _attn_sample(q, k, v, cache_k, cache_v, e, page_table, lam_p, sw,
                               n_heads=n_heads_a, e_a=e_a, lam_init=lam_init)
            hist = [state_sconv[e, :, j] for j in range(sconv - 1)]
            xs = _even_out(att, bg, cx, xs, sconv_w[e], w_out, tm=dec_b, hist=hist)
            ks.append(k.reshape(dec_b, 1, n_heads_a, e_a))
            vs.append(v.reshape(dec_b, 1, n_heads_a, e_a))
            scs.append(jnp.concatenate([state_sconv[e, :, 1:], cx[:, None, :]], axis=1))
        else:
            o = l // 2
            w_in = w_in_odd[o]
            wqkv = w_in[:, :qkv_c].astype(BF16)
            wz = w_in[:, qkv_c:qkv_c + kdim].astype(BF16)
            wba = jnp.pad(w_in[:, qkv_c + kdim:], ((0, 0), (0, LANES - 2 * n_heads_c))).astype(BF16)
            w_out = w_out_odd[o].astype(BF16)
            lane_pad = lambda a: jnp.pad(a, (n_heads_c, LANES - 2 * n_heads_c)).reshape(1, LANES)
            al_pad, dtb_pad = lane_pad(a_log[o]), lane_pad(dt_bias[o])
            nw = gdn_norm_w[o].reshape(1, dk)
            proj = functools.partial(_gdn_proj, n_heads=n_heads_c, dk=dk)
            q, k, v, z, bga, tail = proj(xp, gm, wqkv, wz, wba, gdn_conv_w[o], al_pad, dtb_pad,
                                         tm=tm_p, seq=seq)
            u, w, qe, ke, qk, gc = _gdn_prep(q, k, v, bga, n_heads=n_heads_c, dk=dk)
            og, s_new = _gdn_scan(u, w, qe, ke, qk, gc, batch=batch, seq=seq, n_heads=n_heads_c, dk=dk,
                                  rows=scan_rows)
            xp = _gdn_out(og, z, xp, nw, w_out, tm=tm_p, n_heads=n_heads_c, dk=dk)
            gcp.append(tail[:, HIST_ROWS - (gdn_conv - 1):])
            gsp.append(s_new)
            hist_arr = state_gdn_conv[o].reshape(dec_b, (gdn_conv - 1) * qkv_c)
            q, k, v, z, bga, raw = proj(xs, gm, wqkv, wz, wba, gdn_conv_w[o], al_pad, dtb_pad,
                                        tm=dec_b, hist=(hist_arr, gdn_conv - 1))
            og, s_new = _gdn_step(q, k, v, bga, state_gdn[o], n_heads=n_heads_c, dk=dk, nb=_tile(dec_b, 8))
            xs = _gdn_out(og, z, xs, nw, w_out, tm=dec_b, n_heads=n_heads_c, dk=dk)
            gcs.append(jnp.concatenate([state_gdn_conv[o, :, 1:], raw[:, None, :]], axis=1))
            gss.append(s_new)
        gn = norm_ffn[l].reshape(1, d)
        wg = w_up[l, :, :d_ff].astype(BF16)
        wu = w_up[l, :, d_ff:].astype(BF16)
        wd = w_down[l].astype(BF16)
        cb = ffn_conv_b[l].reshape(1, d_ff)
        last = l == depth - 1
        xp, tail = _ffn(xp, gn, wg, wu, ffn_conv_w[l], cb, wd, gf, tm=tm_p, final_norm=last, seq=seq)
        fcp.append(tail[:, HIST_ROWS - (ffn_conv - 1):])
        hist_arr = state_ffn_conv[l].reshape(dec_b, (ffn_conv - 1) * d_ff)
        xs, raw = _ffn(xs, gn, wg, wu, ffn_conv_w[l], cb, wd, gf, tm=dec_b, final_norm=last,
                       hist=(hist_arr, ffn_conv - 1))
        fcs.append(jnp.concatenate([state_ffn_conv[l, :, 1:], raw[:, None, :]], axis=1))
    return (xp.reshape(batch, seq, d), xs.reshape(dec_b, 1, d),
            jnp.stack(kp), jnp.stack(vp), jnp.stack(ks), jnp.stack(vs),
            jnp.stack(scp), jnp.stack(scs),
            jnp.stack(gcp), jnp.stack(gcs), jnp.stack(gsp), jnp.stack(gss),
            jnp.stack(fcp), jnp.stack(fcs))
```

```python
import functools
import math

import jax
import jax.numpy as jnp
from jax import lax
from jax.experimental import pallas as pl
from jax.experimental.pallas import tpu as pltpu

F32 = jnp.float32
BF16 = jnp.bfloat16

EPS = 1e-6
SUBLN_EPS = 1e-5
ROPE_THETA = 500000.0
GDN_CHUNK = 64
V7X_VMEM_LIMIT = 56 * 1024 * 1024
LANES = 128
HIST_ROWS = 8
BF16_ROWS = 16


def _cparams(*sem):
    return pltpu.CompilerParams(dimension_semantics=sem, vmem_limit_bytes=V7X_VMEM_LIMIT)


def _const_spec(shape):
    nd = len(shape)
    return pl.BlockSpec(shape, lambda i: (0,) * nd, pipeline_mode=pl.Buffered(1))


def _rms(x, g, eps):
    return x * lax.rsqrt(jnp.mean(x * x, axis=-1, keepdims=True) + eps) * g


def _sigmoid(x):
    return 1.0 / (1.0 + jnp.exp(-x))


def _silu(x):
    return x * _sigmoid(x)


def _softplus(x):
    return jnp.maximum(x, 0.0) + jnp.log1p(jnp.exp(-jnp.abs(x)))


def _dot(a, b):
    return jnp.dot(a, b, preferred_element_type=F32)


def _dot_nt(a, b):
    return lax.dot_general(a, b, (((1,), (1,)), ((), ())), preferred_element_type=F32)


def _dot_tn(a, b):
    return lax.dot_general(a, b, (((0,), (0,)), ((), ())), preferred_element_type=F32)


def _split3(a):
    hi = a.astype(BF16)
    r = a - hi.astype(F32)
    mid = r.astype(BF16)
    lo = (r - mid.astype(F32)).astype(BF16)
    return hi, mid, lo


def _even_proj_kernel(x_ref, g_ref, w_ref, c_ref, s1_ref, s2_ref,
                      q_ref, k_ref, v_ref, kb_ref, vb_ref, bg_ref, cx_ref,
                      *, n_heads, e_a, w_b, half, qscale):
    xn = _rms(x_ref[...], g_ref[...], EPS).astype(BF16)
    n = n_heads * e_a
    c, s1, s2 = c_ref[...], s1_ref[...], s2_ref[...]

    def rope(t):
        return t * c + pltpu.roll(t, e_a - half, 1) * s1 + pltpu.roll(t, half, 1) * s2

    hqk = _dot(xn, w_ref[:, :2 * n])
    for h in range(n_heads):
        sl = slice(h * e_a, (h + 1) * e_a)
        q_ref[:, sl] = (rope(hqk[:, sl]) * qscale).astype(BF16)
        kr = rope(hqk[:, n + h * e_a:n + (h + 1) * e_a])
        k_ref[:, sl] = kr
        kb_ref[:, sl] = kr.astype(BF16)
    hv = _dot(xn, w_ref[:, 2 * n:3 * n])
    v_ref[...] = hv
    vb_ref[...] = hv.astype(BF16)
    o = 3 * n
    bg_ref[...] = _dot(xn, w_ref[:, o:o + w_b])
    hc = _dot(xn, w_ref[:, o + w_b:o + 3 * w_b])
    cx_ref[...] = hc[:, :w_b] * hc[:, w_b:]


def _even_proj(x, g, w, rope_tabs, *, tm, n_heads, e_a, w_b, half):
    t, d = x.shape
    n = n_heads * e_a
    c, s1, s2 = rope_tabs
    row = lambda width: pl.BlockSpec((tm, width), lambda i: (i, 0))
    kern = functools.partial(_even_proj_kernel, n_heads=n_heads, e_a=e_a, w_b=w_b, half=half,
                             qscale=(e_a // 2) ** -0.5)
    return pl.pallas_call(
        kern,
        grid=(t // tm,),
        in_specs=[row(d), _const_spec((1, d)), _const_spec(w.shape), row(e_a), row(e_a), row(e_a)],
        out_specs=[row(n), row(n), row(n), row(n), row(n), row(w_b), row(w_b)],
        out_shape=[jax.ShapeDtypeStruct((t, n), BF16), jax.ShapeDtypeStruct((t, n), F32),
                   jax.ShapeDtypeStruct((t, n), F32), jax.ShapeDtypeStruct((t, n), BF16),
                   jax.ShapeDtypeStruct((t, n), BF16), jax.ShapeDtypeStruct((t, w_b), F32),
                   jax.ShapeDtypeStruct((t, w_b), F32)],
        compiler_params=_cparams("arbitrary"),
        name="even_proj",
    )(x, g, w, c, s1, s2)


def _rope_tables(pos, e_a, rot, dtype=F32):
    half = rot // 2
    dh = e_a // 2
    inv = ROPE_THETA ** (-jnp.arange(half, dtype=F32) * 2.0 / rot)
    ang = pos.astype(F32)[:, None] * inv[None, :]
    cos, sin = jnp.cos(ang).astype(dtype), jnp.sin(ang).astype(dtype)
    t = pos.shape[0]
    ones = jnp.ones((t, dh - rot), dtype)
    zeros_h = jnp.zeros((t, half), dtype)
    zeros_r = jnp.zeros((t, dh - rot), dtype)
    c = jnp.concatenate([cos, cos, ones], axis=1)
    s1 = jnp.concatenate([-sin, zeros_h, zeros_r], axis=1)
    s2 = jnp.concatenate([zeros_h, sin, zeros_r], axis=1)
    rep = lambda a: jnp.concatenate([a, a], axis=1)
    return rep(c), rep(s1), rep(s2)


def _lambda(lp_ref, lam_init):
    lp = lp_ref[...]
    a = jnp.sum(lp[0:1] * lp[1:2], axis=-1, keepdims=True)
    b = jnp.sum(lp[2:3] * lp[3:4], axis=-1, keepdims=True)
    return jnp.exp(a) - jnp.exp(b) + lam_init


def _attn_prompt_kernel(q_ref, k_ref, v_ref, lp_ref, sw_ref, o_ref, *, blk, dh, lam_init):
    i = pl.program_id(2)
    q = q_ref[...].astype(F32)
    lane = lax.broadcasted_iota(jnp.int32, q.shape, 1)
    qs = jnp.concatenate([jnp.where(lane < dh, q, 0.0), jnp.where(lane >= dh, q, 0.0)],
                         axis=0).astype(BF16)

    def step(kj, vj, carry, mask):
        m, l, acc = carry
        s = _dot_nt(qs, kj)
        if mask is not None:
            s = jnp.where(mask, s, jnp.finfo(F32).min)
        m_new = jnp.maximum(m, jnp.max(s, axis=-1, keepdims=True))
        p = jnp.exp(s - m_new)
        alpha = jnp.exp(m - m_new)
        l = alpha * l + jnp.sum(p, axis=-1, keepdims=True)
        acc = alpha * acc + _dot(p.astype(BF16), vj)
        return m_new, l, acc

    def body(j, carry):
        off = pl.multiple_of(j * blk, blk)
        return step(k_ref[pl.ds(off, blk), :], v_ref[pl.ds(off, blk), :], carry, None)

    init = (jnp.full((2 * blk, 1), -1e30, F32), jnp.zeros((2 * blk, 1), F32),
            jnp.zeros((2 * blk, q.shape[1]), F32))
    carry = lax.fori_loop(0, i, body, init)
    off = pl.multiple_of(i * blk, blk)
    r = lax.broadcasted_iota(jnp.int32, (2 * blk, blk), 0)
    cidx = lax.broadcasted_iota(jnp.int32, (2 * blk, blk), 1)
    r = jnp.where(r >= blk, r - blk, r)
    _, l, acc = step(k_ref[pl.ds(off, blk), :], v_ref[pl.ds(off, blk), :], carry, cidx <= r)
    o = acc / l
    lam = _lambda(lp_ref, lam_init)
    att = o[:blk] - lam * o[blk:]
    o_ref[...] = (_rms(att, sw_ref[...], SUBLN_EPS) * (1.0 - lam_init)).astype(o_ref.dtype)


def _attn_prompt(q, kb, vb, lam_p, subln_w, *, batch, seq, n_heads, e_a, blk, lam_init):
    t = q.shape[0]
    nq = seq // blk
    kern = functools.partial(_attn_prompt_kernel, blk=blk, dh=e_a // 2, lam_init=lam_init)
    return pl.pallas_call(
        kern,
        grid=(batch, n_heads, nq),
        in_specs=[pl.BlockSpec((blk, e_a), lambda b, h, i: (b * nq + i, h)),
                  pl.BlockSpec((seq, e_a), lambda b, h, i: (b, h)),
                  pl.BlockSpec((seq, e_a), lambda b, h, i: (b, h)),
                  pl.BlockSpec(lam_p.shape, lambda b, h, i: (0, 0)),
                  pl.BlockSpec(subln_w.shape, lambda b, h, i: (0, 0))],
        out_specs=pl.BlockSpec((blk, e_a), lambda b, h, i: (b * nq + i, h)),
        out_shape=jax.ShapeDtypeStruct((t, n_heads * e_a), BF16),
        compiler_params=_cparams("arbitrary", "arbitrary", "arbitrary"),
        name="attn_prompt",
    )(q, kb, vb, lam_p, subln_w)


def _attn_sample_kernel(pt_ref, q_ref, kn_ref, vn_ref, lp_ref, sw_ref, *rest,
                        n_pages, n_heads, e_a, lam_init):
    del pt_ref
    k_refs = rest[:n_pages]
    v_refs = rest[n_pages:2 * n_pages]
    o_ref = rest[2 * n_pages]
    dh = e_a // 2
    nm = max(2 * n_heads, BF16_ROWS)
    rows_pp = k_refs[0].shape[1]
    rowi = lax.broadcasted_iota(jnp.int32, (nm, e_a), 0)
    lane = lax.broadcasted_iota(jnp.int32, (nm, e_a), 1)

    def per_head_rows(ref):
        t = ref[0]
        out = jnp.zeros((nm, e_a), F32)
        for h in range(n_heads):
            out = jnp.where((rowi >> 1) == h, jnp.broadcast_to(t[h:h + 1].astype(F32), (nm, e_a)), out)
        return out

    qm_f = jnp.where((lane // dh) == (rowi & 1), per_head_rows(q_ref), 0.0)
    qm = qm_f.astype(BF16)
    colh = lax.broadcasted_iota(jnp.int32, (nm, rows_pp), 1) % n_heads
    valid = colh == (lax.broadcasted_iota(jnp.int32, (nm, rows_pp), 0) >> 1)
    ss = [jnp.where(valid, _dot_nt(qm, k_refs[p][0].astype(BF16)), jnp.finfo(F32).min)
          for p in range(n_pages)]
    kn = per_head_rows(kn_ref).astype(BF16).astype(F32)
    vn = per_head_rows(vn_ref).astype(BF16).astype(F32)
    s_new = jnp.sum(qm.astype(F32) * kn, axis=-1, keepdims=True)
    m = s_new
    for s in ss:
        m = jnp.maximum(m, jnp.max(s, axis=-1, keepdims=True))
    p_new = jnp.exp(s_new - m)
    l = p_new
    acc = p_new.astype(BF16).astype(F32) * vn
    for p in range(n_pages):
        pr = jnp.exp(ss[p] - m)
        l = l + jnp.sum(pr, axis=-1, keepdims=True)
        acc = acc + _dot(pr.astype(BF16), v_refs[p][0].astype(BF16))
    o = acc / l
    lam = _lambda(lp_ref, lam_init)
    sw = sw_ref[...]
    for h in range(n_heads):
        att = o[2 * h:2 * h + 1] - lam * o[2 * h + 1:2 * h + 2]
        o_ref[0, h:h + 1, :] = (_rms(att, sw, SUBLN_EPS) * (1.0 - lam_init)).astype(o_ref.dtype)


def _attn_sample(q, k_new, v_new, cache_k, cache_v, layer, page_table, lam_p, subln_w,
                 *, n_heads, e_a, lam_init):
    b, n = q.shape
    n_pages = page_table.shape[1]
    n_phys, page = cache_k.shape[1], cache_k.shape[2]
    ck = cache_k.reshape(cache_k.shape[0] * n_phys, page * n_heads, e_a)
    cv = cache_v.reshape(cache_v.shape[0] * n_phys, page * n_heads, e_a)
    base = layer * n_phys
    tok = pl.BlockSpec((1, n_heads, e_a), lambda i, pt: (i, 0, 0))
    page_specs = [pl.BlockSpec((1, page * n_heads, e_a), functools.partial(
        lambda i, pt, p: (base + pt[i, p], 0, 0), p=p)) for p in range(n_pages)]
    kern = functools.partial(_attn_sample_kernel, n_pages=n_pages, n_heads=n_heads, e_a=e_a,
                             lam_init=lam_init)
    grid_spec = pltpu.PrefetchScalarGridSpec(
        num_scalar_prefetch=1,
        grid=(b,),
        in_specs=[tok, tok, tok,
                  pl.BlockSpec(lam_p.shape, lambda i, pt: (0, 0)),
                  pl.BlockSpec(subln_w.shape, lambda i, pt: (0, 0))] + page_specs + page_specs,
        out_specs=tok,
    )
    tok3 = lambda a: a.reshape(b, n_heads, e_a)
    out = pl.pallas_call(
        kern,
        grid_spec=grid_spec,
        out_shape=jax.ShapeDtypeStruct((b, n_heads, e_a), F32),
        compiler_params=_cparams("arbitrary"),
        name="attn_sample",
    )(page_table, tok3(q).astype(F32), tok3(k_new), tok3(v_new), lam_p, subln_w,
      *([ck] * n_pages), *([cv] * n_pages))
    return out.reshape(b, n)


def _seq_hist(buf_ref, cur, tiles_per_seq, tm):
    first = (pl.program_id(0) % tiles_per_seq) == 0

    @pl.when(first)
    def _():
        buf_ref[0:HIST_ROWS, :] = jnp.zeros((HIST_ROWS, buf_ref.shape[1]), buf_ref.dtype)

    @pl.when(jnp.logical_not(first))
    def _():
        buf_ref[0:HIST_ROWS, :] = buf_ref[tm:tm + HIST_ROWS, :]

    buf_ref[HIST_ROWS:HIST_ROWS + tm, :] = cur


def _even_out_seq_kernel(att_ref, bg_ref, cx_ref, x_ref, cw_ref, w_ref, o_ref, buf_ref,
                         *, tm, tiles_per_seq, n_att):
    cx = cx_ref[...]
    _seq_hist(buf_ref, cx, tiles_per_seq, tm)
    cw = cw_ref[...]
    width = cw.shape[0]
    conv = cw[width - 1:width] * cx
    for j in range(width - 1):
        conv = conv + cw[j:j + 1] * buf_ref[pl.ds(HIST_ROWS - (width - 1) + j, tm), :]
    yb = (bg_ref[...] * conv).astype(BF16)
    o_ref[...] = (x_ref[...] + _dot(att_ref[...].astype(BF16), w_ref[:n_att, :])
                  + _dot(yb, w_ref[n_att:, :]))


def _even_out_step_kernel(att_ref, bg_ref, cx_ref, x_ref, cw_ref, w_ref, *rest, n_att):
    hist_refs, o_ref = rest[:-1], rest[-1]
    cw = cw_ref[...]
    width = cw.shape[0]
    conv = cw[width - 1:width] * cx_ref[...]
    for j in range(width - 1):
        conv = conv + cw[j:j + 1] * hist_refs[j][...]
    yb = (bg_ref[...] * conv).astype(BF16)
    o_ref[...] = (x_ref[...] + _dot(att_ref[...].astype(BF16), w_ref[:n_att, :])
                  + _dot(yb, w_ref[n_att:, :]))


def _even_out(att, bg, cx, x, conv_w, w_out, *, tm, seq=None, hist=None):
    t, d = x.shape
    n_att, w_b = att.shape[1], bg.shape[1]
    row = lambda width: pl.BlockSpec((tm, width), lambda i: (i, 0))
    common = [row(n_att), row(w_b), row(w_b), row(d), _const_spec(conv_w.shape), _const_spec(w_out.shape)]
    if hist is None:
        kern = functools.partial(_even_out_seq_kernel, tm=tm, tiles_per_seq=seq // tm, n_att=n_att)
        extra, extra_specs = [], []
        scratch = [pltpu.VMEM((tm + HIST_ROWS, w_b), F32)]
    else:
        kern = functools.partial(_even_out_step_kernel, n_att=n_att)
        extra, extra_specs = list(hist), [row(w_b)] * len(hist)
        scratch = []
    return pl.pallas_call(
        kern,
        grid=(t // tm,),
        in_specs=common + extra_specs,
        out_specs=row(d),
        out_shape=jax.ShapeDtypeStruct((t, d), F32),
        scratch_shapes=scratch,
        compiler_params=_cparams("arbitrary"),
        name="even_out",
    )(att, bg, cx, x, conv_w, w_out, *extra)


def _gdn_post_conv(conv, xn, wz_ref, wba_ref, al_ref, dtb_ref, q_ref, k_ref, v_ref, z_ref, bga_ref,
                   *, n_heads, dk):
    c = _silu(conv)
    kdim = n_heads * dk
    for h in range(n_heads):
        sl = slice(h * dk, (h + 1) * dk)
        tq = c[:, sl]
        q_ref[:, sl] = tq * lax.rsqrt(jnp.sum(tq * tq, axis=-1, keepdims=True) + EPS) * (dk ** -0.5)
        tk = c[:, kdim + h * dk:kdim + (h + 1) * dk]
        k_ref[:, sl] = tk * lax.rsqrt(jnp.sum(tk * tk, axis=-1, keepdims=True) + EPS)
    v_ref[...] = c[:, 2 * kdim:]
    z_ref[...] = _dot(xn, wz_ref[...])
    ba = _dot(xn, wba_ref[...])
    beta = _sigmoid(ba)
    g = -jnp.exp(al_ref[...]) * _softplus(ba + dtb_ref[...])
    lane = lax.broadcasted_iota(jnp.int32, ba.shape, 1)
    bga_ref[...] = jnp.where(lane < n_heads, beta, g)


def _gdn_proj_seq_kernel(x_ref, g_ref, wqkv_ref, wz_ref, wba_ref, cw_ref, al_ref, dtb_ref,
                         q_ref, k_ref, v_ref, z_ref, bga_ref, tail_ref, buf_ref,
                         *, tm, tiles_per_seq, n_heads, dk):
    xn = _rms(x_ref[...], g_ref[...], EPS).astype(BF16)
    hm = _dot(xn, wqkv_ref[...])
    _seq_hist(buf_ref, hm, tiles_per_seq, tm)
    cw = cw_ref[...]
    width = cw.shape[0]
    conv = cw[width - 1:width] * hm
    for j in range(width - 1):
        conv = conv + cw[j:j + 1] * buf_ref[pl.ds(HIST_ROWS - (width - 1) + j, tm), :]
    tail_ref[0] = buf_ref[tm:tm + HIST_ROWS, :]
    _gdn_post_conv(conv, xn, wz_ref, wba_ref, al_ref, dtb_ref, q_ref, k_ref, v_ref, z_ref, bga_ref,
                   n_heads=n_heads, dk=dk)


def _gdn_proj_step_kernel(x_ref, g_ref, wqkv_ref, wz_ref, wba_ref, cw_ref, al_ref, dtb_ref, *rest,
                          n_heads, dk, n_hist):
    hist_refs = rest[:n_hist]
    q_ref, k_ref, v_ref, z_ref, bga_ref, raw_ref = rest[n_hist:]
    xn = _rms(x_ref[...], g_ref[...], EPS).astype(BF16)
    hm = _dot(xn, wqkv_ref[...])
    raw_ref[...] = hm
    cw = cw_ref[...]
    width = cw.shape[0]
    conv = cw[width - 1:width] * hm
    for j in range(width - 1):
        conv = conv + cw[j:j + 1] * hist_refs[j][...]
    _gdn_post_conv(conv, xn, wz_ref, wba_ref, al_ref, dtb_ref, q_ref, k_ref, v_ref, z_ref, bga_ref,
                   n_heads=n_heads, dk=dk)


def _gdn_proj(x, g, wqkv, wz, wba, conv_w, al_pad, dtb_pad, *, tm, n_heads, dk, seq=None, hist=None):
    t, d = x.shape
    qkv_c = wqkv.shape[1]
    vdim = wz.shape[1]
    row = lambda width: pl.BlockSpec((tm, width), lambda i: (i, 0))
    common = [row(d), _const_spec((1, d)), _const_spec(wqkv.shape), _const_spec(wz.shape),
              _const_spec(wba.shape), _const_spec(conv_w.shape), _const_spec(al_pad.shape),
              _const_spec(dtb_pad.shape)]
    outs = [row(vdim), row(vdim), row(vdim), row(vdim), row(LANES)]
    out_shapes = [jax.ShapeDtypeStruct((t, vdim), F32)] * 4 + [jax.ShapeDtypeStruct((t, LANES), F32)]
    if hist is None:
        tps = seq // tm
        kern = functools.partial(_gdn_proj_seq_kernel, tm=tm, tiles_per_seq=tps, n_heads=n_heads, dk=dk)
        extra, extra_specs = [], []
        outs.append(pl.BlockSpec((1, HIST_ROWS, qkv_c), lambda i: (i // tps, 0, 0)))
        out_shapes.append(jax.ShapeDtypeStruct((t // seq, HIST_ROWS, qkv_c), F32))
        scratch = [pltpu.VMEM((tm + HIST_ROWS, qkv_c), F32)]
    else:
        hist_arr, n_hist = hist
        kern = functools.partial(_gdn_proj_step_kernel, n_heads=n_heads, dk=dk, n_hist=n_hist)
        extra = [hist_arr] * n_hist
        extra_specs = [pl.BlockSpec((tm, qkv_c), functools.partial(lambda i, j: (i, j), j=j))
                       for j in range(n_hist)]
        outs.append(row(qkv_c))
        out_shapes.append(jax.ShapeDtypeStruct((t, qkv_c), F32))
        scratch = []
    return pl.pallas_call(
        kern,
        grid=(t // tm,),
        in_specs=common + extra_specs,
        out_specs=outs,
        out_shape=out_shapes,
        scratch_shapes=scratch,
        compiler_params=_cparams("arbitrary"),
        name="gdn_proj",
    )(x, g, wqkv, wz, wba, conv_w, al_pad, dtb_pad, *extra)


def _pair_blockdiag(a, b):
    z = jnp.zeros_like(a)
    return jnp.concatenate([jnp.concatenate([a, z], axis=1), jnp.concatenate([z, b], axis=1)], axis=0)


def _gdn_prep_kernel(q_ref, k_ref, v_ref, bga_ref, u_ref, w_ref, qe_ref, ke_ref, qk_ref, gc_ref,
                     *, n_heads, dk, chunk, n_sub):
    n_pairs = n_heads // 2
    rows_blk = chunk * n_sub
    bga = bga_ref[...]
    r = lax.broadcasted_iota(jnp.int32, (rows_blk, rows_blk), 0)
    c = lax.broadcasted_iota(jnp.int32, (rows_blk, rows_blk), 1)
    sh = chunk.bit_length() - 1
    tril = jnp.where(jnp.logical_and(r >= c, (r >> sh) == (c >> sh)), 1.0, 0.0).astype(BF16)
    b3 = _split3(bga)
    gc = _dot(tril, b3[0]) + (_dot(tril, b3[1]) + _dot(tril, b3[2]))
    gc_ref[...] = gc
    lane = lax.broadcasted_iota(jnp.int32, (rows_blk, LANES), 1)
    is_g = jnp.logical_and(lane >= n_heads, lane < 2 * n_heads)
    odd = ((lane - n_heads) & 1) == 1
    gce = jnp.where(jnp.logical_and(is_g, jnp.logical_not(odd)), gc, 0.0)
    gco = jnp.where(jnp.logical_and(is_g, odd), gc, 0.0)
    ar = lax.broadcasted_iota(jnp.int32, (BF16_ROWS, LANES), 0)
    al = lax.broadcasted_iota(jnp.int32, (BF16_ROWS, LANES), 1)
    asel = jnp.where(jnp.logical_and(jnp.logical_and(al >= n_heads, ar < n_pairs),
                                     ((al - n_heads) >> 1) == ar), 1.0, 0.0).astype(BF16)
    prow = lax.broadcasted_iota(jnp.int32, (chunk, 2 * chunk), 0)
    plane = lax.broadcasted_iota(jnp.int32, (chunk, 2 * chunk), 1)
    pcol = plane & (chunk - 1)
    lo_half = plane < chunk
    incl = prow >= pcol
    strict = prow > pcol

    def bd(m):
        return jnp.concatenate([jnp.where(lo_half, m, 0.0), jnp.where(lo_half, 0.0, m)], axis=0).astype(BF16)

    items = [(ci, p) for ci in range(n_sub) for p in range(n_pairs)]
    lmat, tinv, vbn, kbe = {}, {}, {}, {}
    for ci, p in items:
        rs = slice(ci * chunk, (ci + 1) * chunk)
        cs = slice(2 * p * dk, 2 * (p + 1) * dk)
        g3 = _split3(jnp.concatenate([gce[rs], gco[rs]], axis=0))
        grow = _dot_nt(asel, g3[0]) + (_dot_nt(asel, g3[1]) + _dot_nt(asel, g3[2]))
        k2, q2, v2 = k_ref[rs, cs], q_ref[rs, cs], v_ref[rs, cs]
        ha, hb = 2 * p, 2 * p + 1

        def nat(a, b, rows=chunk):
            return jnp.concatenate([jnp.broadcast_to(a, (rows, dk)), jnp.broadcast_to(b, (rows, dk))], axis=1)

        gca, gcb = gc[rs, n_heads + ha:n_heads + ha + 1], gc[rs, n_heads + hb:n_heads + hb + 1]
        bnat = nat(bga[rs, ha:ha + 1], bga[rs, hb:hb + 1])
        gnat = nat(gca, gcb)
        glast = nat(gca[chunk - 1:chunk], gcb[chunk - 1:chunk], rows=1)
        egn = jnp.exp(gnat)
        kb2 = k2 * bnat
        k2b = k2.astype(BF16)
        rhs_t = _pair_blockdiag(k2b[:, :dk], k2b[:, dk:])
        decay = jnp.where(incl, jnp.exp(jnp.where(incl, jnp.where(lo_half, gca, gcb) - grow[p:p + 1], 0.0)), 0.0)
        lm = jnp.where(strict, _dot_nt(kb2.astype(BF16), rhs_t) * decay, 0.0)
        qk_ref[rs, 2 * p * chunk:2 * (p + 1) * chunk] = jnp.where(
            incl, _dot_nt(q2.astype(BF16), rhs_t) * decay, 0.0).astype(BF16)
        qe_ref[rs, cs] = (q2 * egn).astype(BF16)
        ke_ref[rs, cs] = (k2 * jnp.exp(glast - gnat)).astype(BF16)
        lmat[ci, p] = lm
        tinv[ci, p] = jnp.where(prow == pcol, 1.0, 0.0) - jnp.where((prow >> 1) == (pcol >> 1), lm, 0.0)
        vbn[ci, p] = (v2 * bnat).astype(BF16)
        kbe[ci, p] = (kb2 * egn).astype(BF16)
    s = 1
    while 2 * (1 << s) <= chunk:
        joins = jnp.logical_and((prow >> (s + 1)) == (pcol >> (s + 1)), (prow >> s) != (pcol >> s))
        x = {it: _dot(tinv[it].astype(BF16), bd(jnp.where(joins, lmat[it], 0.0))) for it in items}
        tinv = {it: tinv[it] - _dot(x[it].astype(BF16), bd(tinv[it])) for it in items}
        s += 1
    for ci, p in items:
        rs = slice(ci * chunk, (ci + 1) * chunk)
        cs = slice(2 * p * dk, 2 * (p + 1) * dk)
        tb = tinv[ci, p].astype(BF16)
        u_ref[rs, cs] = _dot(tb, _pair_blockdiag(vbn[ci, p][:, :dk], vbn[ci, p][:, dk:]))
        w_ref[rs, cs] = _dot(tb, _pair_blockdiag(kbe[ci, p][:, :dk], kbe[ci, p][:, dk:])).astype(BF16)


def _gdn_prep(q, k, v, bga, *, n_heads, dk, n_sub):
    t, vdim = q.shape
    chunk = GDN_CHUNK
    assert 2 * chunk == LANES and n_heads % 2 == 0 and 2 * n_heads <= LANES
    rows = chunk * n_sub
    qk_w = n_heads * chunk
    row = lambda width: pl.BlockSpec((rows, width), lambda i: (i, 0))
    kern = functools.partial(_gdn_prep_kernel, n_heads=n_heads, dk=dk, chunk=chunk, n_sub=n_sub)
    return pl.pallas_call(
        kern,
        grid=(t // rows,),
        in_specs=[row(vdim), row(vdim), row(vdim), row(LANES)],
        out_specs=[row(vdim), row(vdim), row(vdim), row(vdim), row(qk_w), row(LANES)],
        out_shape=[jax.ShapeDtypeStruct((t, vdim), F32)] + [jax.ShapeDtypeStruct((t, vdim), BF16)] * 3
                  + [jax.ShapeDtypeStruct((t, qk_w), BF16), jax.ShapeDtypeStruct((t, LANES), F32)],
        compiler_params=_cparams("arbitrary"),
        name="gdn_prep",
    )(q, k, v, bga)


def _gdn_scan_kernel(u_ref, w_ref, qe_ref, ke_ref, qk_ref, gc_ref, o_ref, sout_ref, s_ref,
                     *, n_heads, dk, chunk, n_sub):
    j = pl.program_id(1)

    @pl.when(j == 0)
    def _():
        s_ref[...] = jnp.zeros(s_ref.shape, s_ref.dtype)

    heads = range(n_heads)
    sls = [slice(h * dk, (h + 1) * dk) for h in heads]

    def body(c, carry):
        r0 = pl.multiple_of(c * chunk, chunk)
        rows = pl.ds(r0, chunk)
        eg = jnp.exp(gc_ref[pl.ds(r0 + chunk - 1, 1), :])
        st = [s_ref[h] for h in heads]
        sb = [x.astype(BF16) for x in st]
        ws = [_dot(jnp.concatenate([w_ref[rows, sls[h]], qe_ref[rows, sls[h]]], axis=0), sb[h]) for h in heads]
        vb = [(u_ref[rows, sls[h]] - ws[h][:chunk]).astype(BF16) for h in heads]
        for p in range(n_heads // 2):
            ha, hb = 2 * p, 2 * p + 1
            intra = _dot(qk_ref[rows, 2 * p * chunk:2 * (p + 1) * chunk], _pair_blockdiag(vb[ha], vb[hb]))
            o_ref[rows, 2 * p * dk:2 * (p + 1) * dk] = (
                jnp.concatenate([ws[ha][chunk:], ws[hb][chunk:]], axis=1) + intra)
        for h in heads:
            s_ref[h] = st[h] * eg[:, n_heads + h:n_heads + h + 1] + _dot_tn(ke_ref[rows, sls[h]], vb[h])
        return carry

    lax.fori_loop(0, n_sub, body, 0)

    @pl.when(j == pl.num_programs(1) - 1)
    def _():
        sout_ref[0] = s_ref[...]


def _gdn_scan(u, w, qe, ke, qk, gc, *, batch, seq, n_heads, dk, rows):
    t, vdim = u.shape
    chunk = GDN_CHUNK
    nb = seq // rows
    row = lambda width: pl.BlockSpec((rows, width), lambda b, j: (b * nb + j, 0))
    kern = functools.partial(_gdn_scan_kernel, n_heads=n_heads, dk=dk, chunk=chunk, n_sub=rows // chunk)
    return pl.pallas_call(
        kern,
        grid=(batch, nb),
        in_specs=[row(vdim)] * 4 + [row(qk.shape[1]), row(LANES)],
        out_specs=[row(vdim), pl.BlockSpec((1, n_heads, dk, dk), lambda b, j: (b, 0, 0, 0))],
        out_shape=[jax.ShapeDtypeStruct((t, vdim), F32),
                   jax.ShapeDtypeStruct((batch, n_heads, dk, dk), F32)],
        scratch_shapes=[pltpu.VMEM((n_heads, dk, dk), F32)],
        compiler_params=_cparams("arbitrary", "arbitrary"),
        name="gdn_scan",
    )(u, w, qe, ke, qk, gc)


def _gdn_step_kernel(q_ref, k_ref, v_ref, bga_ref, s_ref, o_ref, sout_ref, *, n_heads, dk, nb):
    pad = jnp.zeros((dk - n_heads, dk), F32)

    def body(b, carry):
        qt = jnp.concatenate([q_ref[b], pad], axis=0).T
        kt = jnp.concatenate([k_ref[b], pad], axis=0).T
        vb = v_ref[b]
        bg = bga_ref[pl.ds(b, 1), :]
        eg = jnp.exp(bg)
        rows = []
        for h in range(n_heads):
            st = s_ref[b, h] * eg[:, n_heads + h:n_heads + h + 1]
            kc = kt[:, h:h + 1]
            ks = jnp.sum(kc * st, axis=0, keepdims=True)
            delta = (vb[h:h + 1, :] - ks) * bg[:, h:h + 1]
            st = st + kc * delta
            sout_ref[b, h] = st
            rows.append(jnp.sum(qt[:, h:h + 1] * st, axis=0, keepdims=True))
        o_ref[b] = jnp.concatenate(rows, axis=0)
        return carry

    lax.fori_loop(0, nb, body, 0)


def _gdn_step(q, k, v, bga, state, *, n_heads, dk, nb):
    b = q.shape[0]
    q3, k3, v3 = (a.reshape(b, n_heads, dk) for a in (q, k, v))
    vec = pl.BlockSpec((nb, n_heads, dk), lambda i: (i, 0, 0))
    st = pl.BlockSpec((nb, n_heads, dk, dk), lambda i: (i, 0, 0, 0))
    kern = functools.partial(_gdn_step_kernel, n_heads=n_heads, dk=dk, nb=nb)
    o, s_new = pl.pallas_call(
        kern,
        grid=(b // nb,),
        in_specs=[vec, vec, vec, pl.BlockSpec((nb, LANES), lambda i: (i, 0)), st],
        out_specs=[vec, st],
        out_shape=[jax.ShapeDtypeStruct((b, n_heads, dk), F32), jax.ShapeDtypeStruct(state.shape, F32)],
        compiler_params=_cparams("arbitrary"),
        name="gdn_step",
    )(q3, k3, v3, bga, state)
    return o.reshape(b, n_heads * dk), s_new


def _gdn_out_kernel(o_ref, z_ref, x_ref, nw_ref, w_ref, y_ref, *, n_heads, dk):
    nw = nw_ref[...]
    parts = []
    for h in range(n_heads):
        sl = slice(h * dk, (h + 1) * dk)
        parts.append((_rms(o_ref[:, sl], nw, EPS) * _silu(z_ref[:, sl])).astype(BF16))
    y_ref[...] = x_ref[...] + _dot(jnp.concatenate(parts, axis=1), w_ref[...])


def _gdn_out(o, z, x, norm_w, w_out, *, tm, n_heads, dk):
    t, d = x.shape
    vdim = o.shape[1]
    row = lambda width: pl.BlockSpec((tm, width), lambda i: (i, 0))
    kern = functools.partial(_gdn_out_kernel, n_heads=n_heads, dk=dk)
    return pl.pallas_call(
        kern,
        grid=(t // tm,),
        in_specs=[row(vdim), row(vdim), row(d), _const_spec(norm_w.shape), _const_spec(w_out.shape)],
        out_specs=row(d),
        out_shape=jax.ShapeDtypeStruct((t, d), F32),
        compiler_params=_cparams("arbitrary"),
        name="gdn_out",
    )(o, z, x, norm_w, w_out)


def _ffn_tail(x, conv, up, cb_ref, wd_ref, gf_ref, o_ref, final_norm):
    act = (_silu(conv + cb_ref[...]) * up).astype(BF16)
    y = x + _dot(act, wd_ref[...])
    if final_norm:
        y = _rms(y, gf_ref[...], EPS)
    o_ref[...] = y


def _ffn_seq_kernel(x_ref, g_ref, wg_ref, wu_ref, cw_ref, cb_ref, wd_ref, gf_ref, o_ref, tail_ref, buf_ref,
                    *, tm, tiles_per_seq, final_norm):
    x = x_ref[...]
    xn = _rms(x, g_ref[...], EPS).astype(BF16)
    gate = _dot(xn, wg_ref[...])
    _seq_hist(buf_ref, gate, tiles_per_seq, tm)
    cw = cw_ref[...]
    width = cw.shape[0]
    conv = cw[width - 1:width] * gate
    for j in range(width - 1):
        conv = conv + cw[j:j + 1] * buf_ref[pl.ds(HIST_ROWS - (width - 1) + j, tm), :]
    tail_ref[0] = buf_ref[tm:tm + HIST_ROWS, :]
    _ffn_tail(x, conv, _dot(xn, wu_ref[...]), cb_ref, wd_ref, gf_ref, o_ref, final_norm)


def _ffn_step_kernel(x_ref, g_ref, wg_ref, wu_ref, cw_ref, cb_ref, wd_ref, gf_ref, *rest,
                     n_hist, final_norm):
    hist_refs = rest[:n_hist]
    o_ref, raw_ref = rest[n_hist:]
    x = x_ref[...]
    xn = _rms(x, g_ref[...], EPS).astype(BF16)
    gate = _dot(xn, wg_ref[...])
    raw_ref[...] = gate
    cw = cw_ref[...]
    width = cw.shape[0]
    conv = cw[width - 1:width] * gate
    for j in range(width - 1):
        conv = conv + cw[j:j + 1] * hist_refs[j][...]
    _ffn_tail(x, conv, _dot(xn, wu_ref[...]), cb_ref, wd_ref, gf_ref, o_ref, final_norm)


def _ffn(x, g, wg, wu, conv_w, conv_b, wd, gf, *, tm, final_norm, seq=None, hist=None):
    t, d = x.shape
    d_ff = wg.shape[1]
    row = lambda width: pl.BlockSpec((tm, width), lambda i: (i, 0))
    common = [row(d), _const_spec((1, d)), _const_spec(wg.shape), _const_spec(wu.shape),
              _const_spec(conv_w.shape), _const_spec(conv_b.shape), _const_spec(wd.shape),
              _const_spec((1, d))]
    if hist is None:
        tps = seq // tm
        kern = functools.partial(_ffn_seq_kernel, tm=tm, tiles_per_seq=tps, final_norm=final_norm)
        extra, extra_specs = [], []
        outs = [row(d), pl.BlockSpec((1, HIST_ROWS, d_ff), lambda i: (i // tps, 0, 0))]
        out_shapes = [jax.ShapeDtypeStruct((t, d), F32),
                      jax.ShapeDtypeStruct((t // seq, HIST_ROWS, d_ff), F32)]
        scratch = [pltpu.VMEM((tm + HIST_ROWS, d_ff), F32)]
    else:
        hist_arr, n_hist = hist
        kern = functools.partial(_ffn_step_kernel, n_hist=n_hist, final_norm=final_norm)
        extra = [hist_arr] * n_hist
        extra_specs = [pl.BlockSpec((tm, d_ff), functools.partial(lambda i, j: (i, j), j=j))
                       for j in range(n_hist)]
        outs = [row(d), row(d_ff)]
        out_shapes = [jax.ShapeDtypeStruct((t, d), F32), jax.ShapeDtypeStruct((t, d_ff), F32)]
        scratch = []
    return pl.pallas_call(
        kern,
        grid=(t // tm,),
        in_specs=common + extra_specs,
        out_specs=outs,
        out_shape=out_shapes,
        scratch_shapes=scratch,
        compiler_params=_cparams("arbitrary"),
        name="conv_ffn",
    )(x, g, wg, wu, conv_w, conv_b, wd, gf, *extra)


def _tile(n, pref):
    t = min(n, pref)
    while n % t:
        t //= 2
    return t


def kernel(x_prompt, x_sample, cache_k, cache_v, state_sconv, state_gdn_conv, state_gdn, state_ffn_conv, page_table, norm_mix, norm_ffn, norm_final, w_in_even, w_out_even, lambda_q1, lambda_k1, lambda_q2, lambda_k2, subln_w, sconv_w, w_in_odd, gdn_conv_w, a_log, dt_bias, gdn_norm_w, w_out_odd, w_up, ffn_conv_w, ffn_conv_b, w_down):
    batch, seq, d = x_prompt.shape
    dec_b = x_sample.shape[0]
    depth = norm_mix.shape[0]
    n_heads_a, e_a = cache_k.shape[3], cache_k.shape[4]
    n_att = n_heads_a * e_a
    rot = (e_a // 2) // 4
    w_b = state_sconv.shape[-1]
    sconv = sconv_w.shape[1]
    page = cache_k.shape[2]
    past_len = page_table.shape[1] * page
    n_heads_c, dk = state_gdn.shape[2], state_gdn.shape[3]
    kdim = n_heads_c * dk
    qkv_c = gdn_conv_w.shape[-1]
    gdn_conv = gdn_conv_w.shape[1]
    d_ff = ffn_conv_w.shape[-1]
    ffn_conv = ffn_conv_w.shape[1]
    t_p = batch * seq

    tm_p = _tile(seq, 256)
    blk = _tile(seq, 512)
    scan_rows = _tile(seq, 512)

    xp = x_prompt.reshape(t_p, d)
    xs = x_sample.reshape(dec_b, d)
    rope_p = _rope_tables(jnp.tile(jnp.arange(seq), batch), e_a, rot)
    rope_s = _rope_tables(jnp.full((dec_b,), past_len), e_a, rot)
    gf = norm_final.reshape(1, d)

    kp, vp, ks, vs, scp, scs = [], [], [], [], [], []
    gcp, gcs, gsp, gss, fcp, fcs = [], [], [], [], [], []
    for l in range(depth):
        gm = norm_mix[l].reshape(1, d)
        if l % 2 == 0:
            e = l // 2
            lam_init = 0.8 - 0.6 * math.exp(-0.3 * l)
            w_in = w_in_even[e].astype(BF16)
            w_out = w_out_even[e].astype(BF16)
            lam_p = jnp.stack([lambda_q1[e], lambda_k1[e], lambda_q2[e], lambda_k2[e]])
            sw = subln_w[e].reshape(1, e_a)
            proj = functools.partial(_even_proj, n_heads=n_heads_a, e_a=e_a, w_b=w_b, half=rot // 2)
            q, k, v, kb, vb, bg, cx = proj(xp, gm, w_in, rope_p, tm=tm_p)
            att = _attn_prompt(q, kb, vb, lam_p, sw, batch=batch, seq=seq, n_heads=n_heads_a, e_a=e_a,
                               blk=blk, lam_init=lam_init)
            xp = _even_out(att, bg, cx, xp, sconv_w[e], w_out, tm=tm_p, seq=seq)
            kp.append(k.reshape(batch, seq, n_heads_a, e_a))
            vp.append(v.reshape(batch, seq, n_heads_a, e_a))
            cx3 = cx.reshape(batch, seq, w_b)
            if seq >= sconv - 1:
                scp.append(cx3[:, seq - (sconv - 1):])
            else:
                scp.append(jnp.concatenate([jnp.zeros((batch, sconv - 1 - seq, w_b), F32), cx3], axis=1))
            q, k, v, kb, vb, bg, cx = proj(xs, gm, w_in, rope_s, tm=dec_b)
            att = _attn_sample(q, k, v, cache_k, cache_v, e, page_table, lam_p, sw,
                               n_heads=n_heads_a, e_a=e_a, lam_init=lam_init)
            hist = [state_sconv[e, :, j] for j in range(sconv - 1)]
            xs = _even_out(att, bg, cx, xs, sconv_w[e], w_out, tm=dec_b, hist=hist)
            ks.append(k.reshape(dec_b, 1, n_heads_a, e_a))
            vs.append(v.reshape(dec_b, 1, n_heads_a, e_a))
            scs.append(jnp.concatenate([state_sconv[e, :, 1:], cx[:, None, :]], axis=1))
        else:
            o = l // 2
            w_in = w_in_odd[o]
            wqkv = w_in[:, :qkv_c].astype(BF16)
            wz = w_in[:, qkv_c:qkv_c + kdim].astype(BF16)
            wba = jnp.pad(w_in[:, qkv_c + kdim:], ((0, 0), (0, LANES - 2 * n_heads_c))).astype(BF16)
            w_out = w_out_odd[o].astype(BF16)
            lane_pad = lambda a: jnp.pad(a, (n_heads_c, LANES - 2 * n_heads_c)).reshape(1, LANES)
            al_pad, dtb_pad = lane_pad(a_log[o]), lane_pad(dt_bias[o])
            nw = gdn_norm_w[o].reshape(1, dk)
            proj = functools.partial(_gdn_proj, n_heads=n_heads_c, dk=dk)
            q, k, v, z, bga, tail = proj(xp, gm, wqkv, wz, wba, gdn_conv_w[o], al_pad, dtb_pad,
                                         tm=tm_p, seq=seq)
            u, w, qe, ke, qk, gc = _gdn_prep(q, k, v, bga, n_heads=n_heads_c, dk=dk,
                                             n_sub=_tile(seq // GDN_CHUNK, 2))
            og, s_new = _gdn_scan(u, w, qe, ke, qk, gc, batch=batch, seq=seq, n_heads=n_heads_c, dk=dk,
                                  rows=scan_rows)
            xp = _gdn_out(og, z, xp, nw, w_out, tm=tm_p, n_heads=n_heads_c, dk=dk)
            gcp.append(tail[:, HIST_ROWS - (gdn_conv - 1):])
            gsp.append(s_new)
            hist_arr = state_gdn_conv[o].reshape(dec_b, (gdn_conv - 1) * qkv_c)
            q, k, v, z, bga, raw = proj(xs, gm, wqkv, wz, wba, gdn_conv_w[o], al_pad, dtb_pad,
                                        tm=dec_b, hist=(hist_arr, gdn_conv - 1))
            og, s_new = _gdn_step(q, k, v, bga, state_gdn[o], n_heads=n_heads_c, dk=dk, nb=_tile(dec_b, 8))
            xs = _gdn_out(og, z, xs, nw, w_out, tm=dec_b, n_heads=n_heads_c, dk=dk)
            gcs.append(jnp.concatenate([state_gdn_conv[o, :, 1:], raw[:, None, :]], axis=1))
            gss.append(s_new)
        gn = norm_ffn[l].reshape(1, d)
        wg = w_up[l, :, :d_ff].astype(BF16)
        wu = w_up[l, :, d_ff:].astype(BF16)
        wd = w_down[l].astype(BF16)
        cb = ffn_conv_b[l].reshape(1, d_ff)
        last = l == depth - 1
        xp, tail = _ffn(xp, gn, wg, wu, ffn_conv_w[l], cb, wd, gf, tm=tm_p, final_norm=last, seq=seq)
        fcp.append(tail[:, HIST_ROWS - (ffn_conv - 1):])
        hist_arr = state_ffn_conv[l].reshape(dec_b, (ffn_conv - 1) * d_ff)
        xs, raw = _ffn(xs, gn, wg, wu, ffn_conv_w[l], cb, wd, gf, tm=dec_b, final_norm=last,
                       hist=(hist_arr, ffn_conv - 1))
        fcs.append(jnp.concatenate([state_ffn_conv[l, :, 1:], raw[:, None, :]], axis=1))
    return (xp.reshape(batch, seq, d), xs.reshape(dec_b, 1, d),
            jnp.stack(kp), jnp.stack(vp), jnp.stack(ks), jnp.stack(vs),
            jnp.stack(scp), jnp.stack(scs),
            jnp.stack(gcp), jnp.stack(gcs), jnp.stack(gsp), jnp.stack(gss),
            jnp.stack(fcp), jnp.stack(fcs))
```

```python
import functools
import math

import jax
import jax.numpy as jnp
from jax import lax
from jax.experimental import pallas as pl
from jax.experimental.pallas import tpu as pltpu

F32 = jnp.float32
BF16 = jnp.bfloat16

EPS = 1e-6
SUBLN_EPS = 1e-5
ROPE_THETA = 500000.0
GDN_CHUNK = 64
V7X_VMEM_LIMIT = 56 * 1024 * 1024
LANES = 128
HIST_ROWS = 8
BF16_ROWS = 16
ATTN_SUB = 2


def _cparams(*sem):
    return pltpu.CompilerParams(dimension_semantics=sem, vmem_limit_bytes=V7X_VMEM_LIMIT)


def _const_spec(shape):
    nd = len(shape)
    return pl.BlockSpec(shape, lambda i: (0,) * nd, pipeline_mode=pl.Buffered(1))


def _rms(x, g, eps):
    return x * lax.rsqrt(jnp.mean(x * x, axis=-1, keepdims=True) + eps) * g


def _sigmoid(x):
    return 0.5 * jnp.tanh(0.5 * x) + 0.5


def _silu(x):
    h = 0.5 * x
    return h * jnp.tanh(h) + h


def _softplus(x):
    return jnp.maximum(x, 0.0) + jnp.log1p(jnp.exp(-jnp.abs(x)))


def _dot(a, b):
    return jnp.dot(a, b, preferred_element_type=F32)


def _dot_nt(a, b):
    return lax.dot_general(a, b, (((1,), (1,)), ((), ())), preferred_element_type=F32)


def _dot_tn(a, b):
    return lax.dot_general(a, b, (((0,), (0,)), ((), ())), preferred_element_type=F32)


def _split3(a):
    hi = a.astype(BF16)
    r = a - hi.astype(F32)
    mid = r.astype(BF16)
    lo = (r - mid.astype(F32)).astype(BF16)
    return hi, mid, lo


def _even_proj_kernel(x_ref, g_ref, w_ref, wvt_ref, c_ref, s1_ref, s2_ref,
                      q_ref, k_ref, v_ref, kb_ref, vt_ref, bg_ref, cx_ref,
                      *, tm, n_heads, e_a, w_b, half, qscale):
    xn = _rms(x_ref[...], g_ref[...], EPS).astype(BF16)
    n = n_heads * e_a
    c, s1, s2 = c_ref[...], s1_ref[...], s2_ref[...]

    def rope(t):
        return t * c + pltpu.roll(t, e_a - half, 1) * s1 + pltpu.roll(t, half, 1) * s2

    hqk = _dot(xn, w_ref[:, :2 * n])
    for h in range(n_heads):
        sl = slice(h * e_a, (h + 1) * e_a)
        q_ref[:, sl] = (rope(hqk[:, sl]) * qscale).astype(BF16)
        kr = rope(hqk[:, n + h * e_a:n + (h + 1) * e_a])
        k_ref[pl.ds(h, tm, stride=n_heads), :] = kr
        kb_ref[:, sl] = kr.astype(BF16)
    hv = _dot(xn, w_ref[:, 2 * n:3 * n])
    for h in range(n_heads):
        v_ref[pl.ds(h, tm, stride=n_heads), :] = hv[:, h * e_a:(h + 1) * e_a]
    vt_ref[...] = _dot_nt(wvt_ref[...], xn).astype(BF16)
    o = 3 * n
    bg_ref[...] = _dot(xn, w_ref[:, o:o + w_b])
    hc = _dot(xn, w_ref[:, o + w_b:o + 3 * w_b])
    cx_ref[...] = hc[:, :w_b] * hc[:, w_b:]


def _even_proj(x, g, w, wvt, rope_tabs, *, tm, n_heads, e_a, w_b, half):
    t, d = x.shape
    n = n_heads * e_a
    c, s1, s2 = rope_tabs
    row = lambda width: pl.BlockSpec((tm, width), lambda i: (i, 0))
    tab_tiles = c.shape[0] // tm
    tab = pl.BlockSpec((tm, e_a), lambda i: (i % tab_tiles, 0))
    tok = pl.BlockSpec((tm * n_heads, e_a), lambda i: (i, 0))
    kern = functools.partial(_even_proj_kernel, tm=tm, n_heads=n_heads, e_a=e_a, w_b=w_b, half=half,
                             qscale=(e_a // 2) ** -0.5)
    return pl.pallas_call(
        kern,
        grid=(t // tm,),
        in_specs=[row(d), _const_spec((1, d)), _const_spec(w.shape), _const_spec(wvt.shape),
                  tab, tab, tab],
        out_specs=[row(n), tok, tok, row(n), pl.BlockSpec((n, tm), lambda i: (0, i)),
                   row(w_b), row(w_b)],
        out_shape=[jax.ShapeDtypeStruct((t, n), BF16), jax.ShapeDtypeStruct((t * n_heads, e_a), F32),
                   jax.ShapeDtypeStruct((t * n_heads, e_a), F32), jax.ShapeDtypeStruct((t, n), BF16),
                   jax.ShapeDtypeStruct((n, t), BF16), jax.ShapeDtypeStruct((t, w_b), F32),
                   jax.ShapeDtypeStruct((t, w_b), F32)],
        compiler_params=_cparams("arbitrary"),
        name="even_proj",
    )(x, g, w, wvt, c, s1, s2)


def _rope_tables(pos, e_a, rot, dtype=F32):
    half = rot // 2
    dh = e_a // 2
    inv = ROPE_THETA ** (-jnp.arange(half, dtype=F32) * 2.0 / rot)
    ang = pos.astype(F32)[:, None] * inv[None, :]
    cos, sin = jnp.cos(ang).astype(dtype), jnp.sin(ang).astype(dtype)
    t = pos.shape[0]
    ones = jnp.ones((t, dh - rot), dtype)
    zeros_h = jnp.zeros((t, half), dtype)
    zeros_r = jnp.zeros((t, dh - rot), dtype)
    c = jnp.concatenate([cos, cos, ones], axis=1)
    s1 = jnp.concatenate([-sin, zeros_h, zeros_r], axis=1)
    s2 = jnp.concatenate([zeros_h, sin, zeros_r], axis=1)
    rep = lambda a: jnp.concatenate([a, a], axis=1)
    return rep(c), rep(s1), rep(s2)


def _lambda(lp_ref, lam_init):
    lp = lp_ref[...]
    a = jnp.sum(lp[0:1] * lp[1:2], axis=-1, keepdims=True)
    b = jnp.sum(lp[2:3] * lp[3:4], axis=-1, keepdims=True)
    return jnp.exp(a) - jnp.exp(b) + lam_init


def _attn_prompt_kernel(q_ref, k_ref, vt_ref, lp_ref, swc_ref, o_ref, *, blk, n_sub, dh, lam_init):
    i = pl.program_id(2)
    qt = q_ref[...].astype(F32).T
    feat = lax.broadcasted_iota(jnp.int32, qt.shape, 0)
    qs = jnp.concatenate([jnp.where(feat < dh, qt, 0.0), jnp.where(feat >= dh, qt, 0.0)],
                         axis=1).astype(BF16)
    e_a = qt.shape[0]
    sub = blk // n_sub

    def step(off, carry, diag):
        def scores(t):
            s = _dot(k_ref[pl.ds(off + t * sub, sub), :], qs)
            if diag:
                key = lax.broadcasted_iota(jnp.int32, s.shape, 0) + t * sub
                qry = lax.broadcasted_iota(jnp.int32, s.shape, 1)
                qry = jnp.where(qry >= blk, qry - blk, qry)
                s = jnp.where(key <= qry, s, jnp.finfo(F32).min)
            return s

        ss = [scores(t) for t in range(n_sub)]
        m, l, acc = carry
        for t in range(n_sub):
            m_new = jnp.maximum(m, jnp.max(ss[t], axis=0, keepdims=True))
            p = jnp.exp(ss[t] - m_new)
            alpha = jnp.exp(m - m_new)
            l = alpha * l + jnp.sum(p, axis=0, keepdims=True)
            acc = alpha * acc + _dot(vt_ref[:, pl.ds(off + t * sub, sub)], p.astype(BF16))
            m = m_new
        return m, l, acc

    def body(j, carry):
        return step(pl.multiple_of(j * blk, blk), carry, False)

    init = (jnp.full((1, 2 * blk), -1e30, F32), jnp.zeros((1, 2 * blk), F32),
            jnp.zeros((e_a, 2 * blk), F32))
    carry = lax.fori_loop(0, i, body, init)
    _, l, acc = step(pl.multiple_of(i * blk, blk), carry, True)
    o = acc / l
    lam = _lambda(lp_ref, lam_init)
    att = o[:, :blk] - lam * o[:, blk:]
    y = att * lax.rsqrt(jnp.mean(att * att, axis=0, keepdims=True) + SUBLN_EPS) * swc_ref[...]
    o_ref[...] = (y * (1.0 - lam_init)).T.astype(o_ref.dtype)


def _attn_prompt(q, kb, vt, lam_p, subln_w, *, batch, seq, n_heads, e_a, blk, lam_init):
    t = q.shape[0]
    nq = seq // blk
    swc = subln_w.reshape(e_a, 1)
    n_sub = ATTN_SUB if blk % (ATTN_SUB * LANES) == 0 else 1
    kern = functools.partial(_attn_prompt_kernel, blk=blk, n_sub=n_sub, dh=e_a // 2, lam_init=lam_init)
    return pl.pallas_call(
        kern,
        grid=(batch, n_heads, nq),
        in_specs=[pl.BlockSpec((blk, e_a), lambda b, h, i: (b * nq + i, h)),
                  pl.BlockSpec((seq, e_a), lambda b, h, i: (b, h)),
                  pl.BlockSpec((e_a, seq), lambda b, h, i: (h, b)),
                  pl.BlockSpec(lam_p.shape, lambda b, h, i: (0, 0)),
                  pl.BlockSpec(swc.shape, lambda b, h, i: (0, 0))],
        out_specs=pl.BlockSpec((blk, e_a), lambda b, h, i: (b * nq + i, h)),
        out_shape=jax.ShapeDtypeStruct((t, n_heads * e_a), BF16),
        compiler_params=_cparams("arbitrary", "arbitrary", "arbitrary"),
        name="attn_prompt",
    )(q, kb, vt, lam_p, swc)


def _attn_sample_kernel(pt_ref, q_ref, kn_ref, vn_ref, lp_ref, sw_ref, *rest,
                        n_pages, n_heads, e_a, lam_init):
    del pt_ref
    k_refs = rest[:n_pages]
    v_refs = rest[n_pages:2 * n_pages]
    o_ref = rest[2 * n_pages]
    dh = e_a // 2
    nm = max(2 * n_heads, BF16_ROWS)
    rows_pp = k_refs[0].shape[1]
    rowi = lax.broadcasted_iota(jnp.int32, (nm, e_a), 0)
    lane = lax.broadcasted_iota(jnp.int32, (nm, e_a), 1)

    def per_head_rows(ref):
        t = ref[0]
        out = jnp.zeros((nm, e_a), F32)
        for h in range(n_heads):
            out = jnp.where((rowi >> 1) == h, jnp.broadcast_to(t[h:h + 1].astype(F32), (nm, e_a)), out)
        return out

    qm_f = jnp.where((lane // dh) == (rowi & 1), per_head_rows(q_ref), 0.0)
    qm = qm_f.astype(BF16)
    colh = lax.broadcasted_iota(jnp.int32, (nm, rows_pp), 1) % n_heads
    valid = colh == (lax.broadcasted_iota(jnp.int32, (nm, rows_pp), 0) >> 1)
    ss = [jnp.where(valid, _dot_nt(qm, k_refs[p][0].astype(BF16)), jnp.finfo(F32).min)
          for p in range(n_pages)]
    kn = per_head_rows(kn_ref).astype(BF16).astype(F32)
    vn = per_head_rows(vn_ref).astype(BF16).astype(F32)
    s_new = jnp.sum(qm.astype(F32) * kn, axis=-1, keepdims=True)
    m = s_new
    for s in ss:
        m = jnp.maximum(m, jnp.max(s, axis=-1, keepdims=True))
    p_new = jnp.exp(s_new - m)
    l = p_new
    acc = p_new.astype(BF16).astype(F32) * vn
    for p in range(n_pages):
        pr = jnp.exp(ss[p] - m)
        l = l + jnp.sum(pr, axis=-1, keepdims=True)
        acc = acc + _dot(pr.astype(BF16), v_refs[p][0].astype(BF16))
    o = acc / l
    lam = _lambda(lp_ref, lam_init)
    sw = sw_ref[...]
    for h in range(n_heads):
        att = o[2 * h:2 * h + 1] - lam * o[2 * h + 1:2 * h + 2]
        o_ref[0, h:h + 1, :] = (_rms(att, sw, SUBLN_EPS) * (1.0 - lam_init)).astype(o_ref.dtype)


def _attn_sample(q, k_new, v_new, cache_k, cache_v, layer, page_table, lam_p, subln_w,
                 *, n_heads, e_a, lam_init):
    b, n = q.shape
    n_pages = page_table.shape[1]
    n_phys, page = cache_k.shape[1], cache_k.shape[2]
    ck = cache_k.reshape(cache_k.shape[0] * n_phys, page * n_heads, e_a)
    cv = cache_v.reshape(cache_v.shape[0] * n_phys, page * n_heads, e_a)
    base = layer * n_phys
    tok = pl.BlockSpec((1, n_heads, e_a), lambda i, pt: (i, 0, 0))
    page_specs = [pl.BlockSpec((1, page * n_heads, e_a), functools.partial(
        lambda i, pt, p: (base + pt[i, p], 0, 0), p=p)) for p in range(n_pages)]
    kern = functools.partial(_attn_sample_kernel, n_pages=n_pages, n_heads=n_heads, e_a=e_a,
                             lam_init=lam_init)
    grid_spec = pltpu.PrefetchScalarGridSpec(
        num_scalar_prefetch=1,
        grid=(b,),
        in_specs=[tok, tok, tok,
                  pl.BlockSpec(lam_p.shape, lambda i, pt: (0, 0)),
                  pl.BlockSpec(subln_w.shape, lambda i, pt: (0, 0))] + page_specs + page_specs,
        out_specs=tok,
    )
    tok3 = lambda a: a.reshape(b, n_heads, e_a)
    out = pl.pallas_call(
        kern,
        grid_spec=grid_spec,
        out_shape=jax.ShapeDtypeStruct((b, n_heads, e_a), F32),
        compiler_params=_cparams("arbitrary"),
        name="attn_sample",
    )(page_table, tok3(q).astype(F32), tok3(k_new), tok3(v_new), lam_p, subln_w,
      *([ck] * n_pages), *([cv] * n_pages))
    return out.reshape(b, n)


def _seq_hist(buf_ref, cur, tiles_per_seq, tm):
    first = (pl.program_id(0) % tiles_per_seq) == 0

    @pl.when(first)
    def _():
        buf_ref[0:HIST_ROWS, :] = jnp.zeros((HIST_ROWS, buf_ref.shape[1]), buf_ref.dtype)

    @pl.when(jnp.logical_not(first))
    def _():
        buf_ref[0:HIST_ROWS, :] = buf_ref[tm:tm + HIST_ROWS, :]

    buf_ref[HIST_ROWS:HIST_ROWS + tm, :] = cur


def _even_out_seq_kernel(att_ref, bg_ref, cx_ref, x_ref, cw_ref, w_ref, o_ref, buf_ref,
                         *, tm, tiles_per_seq, n_att):
    cx = cx_ref[...]
    _seq_hist(buf_ref, cx, tiles_per_seq, tm)
    cw = cw_ref[...]
    width = cw.shape[0]
    conv = cw[width - 1:width] * cx
    for j in range(width - 1):
        conv = conv + cw[j:j + 1] * buf_ref[pl.ds(HIST_ROWS - (width - 1) + j, tm), :]
    yb = (bg_ref[...] * conv).astype(BF16)
    o_ref[...] = (x_ref[...] + _dot(att_ref[...].astype(BF16), w_ref[:n_att, :])
                  + _dot(yb, w_ref[n_att:, :]))


def _even_out_step_kernel(att_ref, bg_ref, cx_ref, x_ref, cw_ref, w_ref, *rest, n_att):
    hist_refs, o_ref = rest[:-1], rest[-1]
    cw = cw_ref[...]
    width = cw.shape[0]
    conv = cw[width - 1:width] * cx_ref[...]
    for j in range(width - 1):
        conv = conv + cw[j:j + 1] * hist_refs[j][...]
    yb = (bg_ref[...] * conv).astype(BF16)
    o_ref[...] = (x_ref[...] + _dot(att_ref[...].astype(BF16), w_ref[:n_att, :])
                  + _dot(yb, w_ref[n_att:, :]))


def _even_out(att, bg, cx, x, conv_w, w_out, *, tm, seq=None, hist=None):
    t, d = x.shape
    n_att, w_b = att.shape[1], bg.shape[1]
    row = lambda width: pl.BlockSpec((tm, width), lambda i: (i, 0))
    common = [row(n_att), row(w_b), row(w_b), row(d), _const_spec(conv_w.shape), _const_spec(w_out.shape)]
    if hist is None:
        kern = functools.partial(_even_out_seq_kernel, tm=tm, tiles_per_seq=seq // tm, n_att=n_att)
        extra, extra_specs = [], []
        scratch = [pltpu.VMEM((tm + HIST_ROWS, w_b), F32)]
    else:
        kern = functools.partial(_even_out_step_kernel, n_att=n_att)
        extra, extra_specs = list(hist), [row(w_b)] * len(hist)
        scratch = []
    return pl.pallas_call(
        kern,
        grid=(t // tm,),
        in_specs=common + extra_specs,
        out_specs=row(d),
        out_shape=jax.ShapeDtypeStruct((t, d), F32),
        scratch_shapes=scratch,
        compiler_params=_cparams("arbitrary"),
        name="even_out",
    )(att, bg, cx, x, conv_w, w_out, *extra)


def _gdn_gates(xn, wz_ref, wba_ref, al_ref, dtb_ref, z_ref, bga_ref, *, n_heads):
    z_ref[...] = _dot(xn, wz_ref[...])
    ba = _dot(xn, wba_ref[...])
    beta = _sigmoid(ba)
    g = -jnp.exp(al_ref[...]) * _softplus(ba + dtb_ref[...])
    lane = lax.broadcasted_iota(jnp.int32, ba.shape, 1)
    bga_ref[...] = jnp.where(lane < n_heads, beta, g)


def _gdn_qkv_chunk(conv, c0, q_ref, k_ref, v_ref, *, n_heads, dk):
    kdim = n_heads * dk
    act = _silu(conv)
    for i in range(conv.shape[1] // dk):
        t = act[:, i * dk:(i + 1) * dk]
        ch = c0 + i * dk
        if ch < 2 * kdim:
            t = t * lax.rsqrt(jnp.sum(t * t, axis=-1, keepdims=True) + EPS)
        if ch < kdim:
            q_ref[:, ch:ch + dk] = t * (dk ** -0.5)
        elif ch < 2 * kdim:
            k_ref[:, ch - kdim:ch - kdim + dk] = t
        else:
            v_ref[:, ch - 2 * kdim:ch - 2 * kdim + dk] = t


def _gdn_proj_seq_kernel(x_ref, g_ref, wqkv_ref, wz_ref, wba_ref, cw_ref, al_ref, dtb_ref,
                         q_ref, k_ref, v_ref, z_ref, bga_ref, tail_ref, buf_ref,
                         *, tm, tiles_per_seq, n_heads, dk, cwid):
    xn = _rms(x_ref[...], g_ref[...], EPS).astype(BF16)
    first = (pl.program_id(0) % tiles_per_seq) == 0

    @pl.when(first)
    def _():
        buf_ref[0:HIST_ROWS, :] = jnp.zeros((HIST_ROWS, buf_ref.shape[1]), buf_ref.dtype)

    @pl.when(jnp.logical_not(first))
    def _():
        buf_ref[0:HIST_ROWS, :] = buf_ref[tm:tm + HIST_ROWS, :]

    cw = cw_ref[...]
    width = cw.shape[0]
    n_chunks = wqkv_ref.shape[1] // cwid
    mm = lambda c: _dot(xn, wqkv_ref[:, c * cwid:(c + 1) * cwid])
    h_next = mm(0)
    for c in range(n_chunks):
        cs = slice(c * cwid, (c + 1) * cwid)
        hm = h_next
        if c + 1 < n_chunks:
            h_next = mm(c + 1)
        buf_ref[HIST_ROWS:HIST_ROWS + tm, cs] = hm
        conv = cw[width - 1:width, cs] * hm
        for j in range(width - 1):
            conv = conv + cw[j:j + 1, cs] * buf_ref[pl.ds(HIST_ROWS - (width - 1) + j, tm), cs]
        _gdn_qkv_chunk(conv, c * cwid, q_ref, k_ref, v_ref, n_heads=n_heads, dk=dk)
    tail_ref[0] = buf_ref[tm:tm + HIST_ROWS, :]
    _gdn_gates(xn, wz_ref, wba_ref, al_ref, dtb_ref, z_ref, bga_ref, n_heads=n_heads)


def _gdn_proj_step_kernel(x_ref, g_ref, wqkv_ref, wz_ref, wba_ref, cw_ref, al_ref, dtb_ref, *rest,
                          n_heads, dk, n_hist):
    hist_refs = rest[:n_hist]
    q_ref, k_ref, v_ref, z_ref, bga_ref, raw_ref = rest[n_hist:]
    xn = _rms(x_ref[...], g_ref[...], EPS).astype(BF16)
    hm = _dot(xn, wqkv_ref[...])
    raw_ref[...] = hm
    cw = cw_ref[...]
    width = cw.shape[0]
    conv = cw[width - 1:width] * hm
    for j in range(width - 1):
        conv = conv + cw[j:j + 1] * hist_refs[j][...]
    _gdn_qkv_chunk(conv, 0, q_ref, k_ref, v_ref, n_heads=n_heads, dk=dk)
    _gdn_gates(xn, wz_ref, wba_ref, al_ref, dtb_ref, z_ref, bga_ref, n_heads=n_heads)


def _gdn_proj(x, g, wqkv, wz, wba, conv_w, al_pad, dtb_pad, *, tm, n_heads, dk, seq=None, hist=None):
    t, d = x.shape
    qkv_c = wqkv.shape[1]
    vdim = wz.shape[1]
    row = lambda width: pl.BlockSpec((tm, width), lambda i: (i, 0))
    common = [row(d), _const_spec((1, d)), _const_spec(wqkv.shape), _const_spec(wz.shape),
              _const_spec(wba.shape), _const_spec(conv_w.shape), _const_spec(al_pad.shape),
              _const_spec(dtb_pad.shape)]
    outs = [row(vdim), row(vdim), row(vdim), row(vdim), row(LANES)]
    out_shapes = [jax.ShapeDtypeStruct((t, vdim), F32)] * 4 + [jax.ShapeDtypeStruct((t, LANES), F32)]
    if hist is None:
        tps = seq // tm
        kern = functools.partial(_gdn_proj_seq_kernel, tm=tm, tiles_per_seq=tps, n_heads=n_heads, dk=dk,
                                 cwid=2 * dk)
        extra, extra_specs = [], []
        outs.append(pl.BlockSpec((1, HIST_ROWS, qkv_c), lambda i: (i // tps, 0, 0)))
        out_shapes.append(jax.ShapeDtypeStruct((t // seq, HIST_ROWS, qkv_c), F32))
        scratch = [pltpu.VMEM((tm + HIST_ROWS, qkv_c), F32)]
    else:
        hist_arr, n_hist = hist
        kern = functools.partial(_gdn_proj_step_kernel, n_heads=n_heads, dk=dk, n_hist=n_hist)
        extra = [hist_arr] * n_hist
        extra_specs = [pl.BlockSpec((tm, qkv_c), functools.partial(lambda i, j: (i, j), j=j))
                       for j in range(n_hist)]
        outs.append(row(qkv_c))
        out_shapes.append(jax.ShapeDtypeStruct((t, qkv_c), F32))
        scratch = []
    return pl.pallas_call(
        kern,
        grid=(t // tm,),
        in_specs=common + extra_specs,
        out_specs=outs,
        out_shape=out_shapes,
        scratch_shapes=scratch,
        compiler_params=_cparams("arbitrary"),
        name="gdn_proj",
    )(x, g, wqkv, wz, wba, conv_w, al_pad, dtb_pad, *extra)


def _pair_blockdiag(a, b):
    z = jnp.zeros_like(a)
    return jnp.concatenate([jnp.concatenate([a, z], axis=1), jnp.concatenate([z, b], axis=1)], axis=0)


def _gdn_prep_kernel(q_ref, k_ref, v_ref, bga_ref, u_ref, w_ref, qe_ref, ke_ref, qk_ref, gc_ref,
                     *, n_heads, dk, chunk, n_sub):
    n_pairs = n_heads // 2
    rows_blk = chunk * n_sub
    bga = bga_ref[...]
    r = lax.broadcasted_iota(jnp.int32, (rows_blk, rows_blk), 0)
    c = lax.broadcasted_iota(jnp.int32, (rows_blk, rows_blk), 1)
    sh = chunk.bit_length() - 1
    tril = jnp.where(jnp.logical_and(r >= c, (r >> sh) == (c >> sh)), 1.0, 0.0).astype(BF16)
    b3 = _split3(bga)
    gc = _dot(tril, b3[0]) + (_dot(tril, b3[1]) + _dot(tril, b3[2]))
    gc_ref[...] = gc
    lane = lax.broadcasted_iota(jnp.int32, (rows_blk, LANES), 1)
    is_g = jnp.logical_and(lane >= n_heads, lane < 2 * n_heads)
    odd = ((lane - n_heads) & 1) == 1
    gce = jnp.where(jnp.logical_and(is_g, jnp.logical_not(odd)), gc, 0.0)
    gco = jnp.where(jnp.logical_and(is_g, odd), gc, 0.0)
    ar = lax.broadcasted_iota(jnp.int32, (BF16_ROWS, LANES), 0)
    al = lax.broadcasted_iota(jnp.int32, (BF16_ROWS, LANES), 1)
    asel = jnp.where(jnp.logical_and(jnp.logical_and(al >= n_heads, ar < n_pairs),
                                     ((al - n_heads) >> 1) == ar), 1.0, 0.0).astype(BF16)
    prow = lax.broadcasted_iota(jnp.int32, (chunk, 2 * chunk), 0)
    plane = lax.broadcasted_iota(jnp.int32, (chunk, 2 * chunk), 1)
    pcol = plane & (chunk - 1)
    lo_half = plane < chunk
    incl = prow >= pcol
    strict = prow > pcol

    def bd(m):
        return jnp.concatenate([jnp.where(lo_half, m, 0.0), jnp.where(lo_half, 0.0, m)], axis=0).astype(BF16)

    items = [(ci, p) for ci in range(n_sub) for p in range(n_pairs)]
    lmat, tinv, vbn, kbe = {}, {}, {}, {}
    for ci, p in items:
        rs = slice(ci * chunk, (ci + 1) * chunk)
        cs = slice(2 * p * dk, 2 * (p + 1) * dk)
        g3 = _split3(jnp.concatenate([gce[rs], gco[rs]], axis=0))
        grow = _dot_nt(asel, g3[0]) + (_dot_nt(asel, g3[1]) + _dot_nt(asel, g3[2]))
        k2, q2, v2 = k_ref[rs, cs], q_ref[rs, cs], v_ref[rs, cs]
        ha, hb = 2 * p, 2 * p + 1

        def nat(a, b, rows=chunk):
            return jnp.concatenate([jnp.broadcast_to(a, (rows, dk)), jnp.broadcast_to(b, (rows, dk))], axis=1)

        gca, gcb = gc[rs, n_heads + ha:n_heads + ha + 1], gc[rs, n_heads + hb:n_heads + hb + 1]
        bnat = nat(bga[rs, ha:ha + 1], bga[rs, hb:hb + 1])
        gnat = nat(gca, gcb)
        glast = nat(gca[chunk - 1:chunk], gcb[chunk - 1:chunk], rows=1)
        egn = jnp.exp(gnat)
        kb2 = k2 * bnat
        k2b = k2.astype(BF16)
        rhs_t = _pair_blockdiag(k2b[:, :dk], k2b[:, dk:])
        decay = jnp.where(incl, jnp.exp(jnp.where(incl, jnp.where(lo_half, gca, gcb) - grow[p:p + 1], 0.0)), 0.0)
        lm = jnp.where(strict, _dot_nt(kb2.astype(BF16), rhs_t) * decay, 0.0)
        qk_ref[rs, 2 * p * chunk:2 * (p + 1) * chunk] = jnp.where(
            incl, _dot_nt(q2.astype(BF16), rhs_t) * decay, 0.0).astype(BF16)
        qe_ref[rs, cs] = (q2 * egn).astype(BF16)
        ke_ref[rs, cs] = (k2 * jnp.exp(glast - gnat)).astype(BF16)
        lmat[ci, p] = lm
        tinv[ci, p] = jnp.where(prow == pcol, 1.0, 0.0) - jnp.where((prow >> 1) == (pcol >> 1), lm, 0.0)
        vbn[ci, p] = (v2 * bnat).astype(BF16)
        kbe[ci, p] = (kb2 * egn).astype(BF16)
    s = 1
    while 2 * (1 << s) <= chunk:
        joins = jnp.logical_and((prow >> (s + 1)) == (pcol >> (s + 1)), (prow >> s) != (pcol >> s))
        x = {it: _dot(tinv[it].astype(BF16), bd(jnp.where(joins, lmat[it], 0.0))) for it in items}
        tinv = {it: tinv[it] - _dot(x[it].astype(BF16), bd(tinv[it])) for it in items}
        s += 1
    for ci, p in items:
        rs = slice(ci * chunk, (ci + 1) * chunk)
        cs = slice(2 * p * dk, 2 * (p + 1) * dk)
        tb = tinv[ci, p].astype(BF16)
        u_ref[rs, cs] = _dot(tb, _pair_blockdiag(vbn[ci, p][:, :dk], vbn[ci, p][:, dk:]))
        w_ref[rs, cs] = _dot(tb, _pair_blockdiag(kbe[ci, p][:, :dk], kbe[ci, p][:, dk:])).astype(BF16)


def _gdn_prep(q, k, v, bga, *, n_heads, dk, n_sub):
    t, vdim = q.shape
    chunk = GDN_CHUNK
    assert 2 * chunk == LANES and n_heads % 2 == 0 and 2 * n_heads <= LANES
    rows = chunk * n_sub
    qk_w = n_heads * chunk
    row = lambda width: pl.BlockSpec((rows, width), lambda i: (i, 0))
    kern = functools.partial(_gdn_prep_kernel, n_heads=n_heads, dk=dk, chunk=chunk, n_sub=n_sub)
    return pl.pallas_call(
        kern,
        grid=(t // rows,),
        in_specs=[row(vdim), row(vdim), row(vdim), row(LANES)],
        out_specs=[row(vdim), row(vdim), row(vdim), row(vdim), row(qk_w), row(LANES)],
        out_shape=[jax.ShapeDtypeStruct((t, vdim), F32)] + [jax.ShapeDtypeStruct((t, vdim), BF16)] * 3
                  + [jax.ShapeDtypeStruct((t, qk_w), BF16), jax.ShapeDtypeStruct((t, LANES), F32)],
        compiler_params=_cparams("arbitrary"),
        name="gdn_prep",
    )(q, k, v, bga)


def _gdn_scan_kernel(u_ref, w_ref, qe_ref, ke_ref, qk_ref, gc_ref, o_ref, sout_ref, s_ref,
                     *, n_heads, dk, chunk, n_sub):
    j = pl.program_id(1)

    @pl.when(j == 0)
    def _():
        s_ref[...] = jnp.zeros(s_ref.shape, s_ref.dtype)

    heads = range(n_heads)
    sls = [slice(h * dk, (h + 1) * dk) for h in heads]

    def body(c, carry):
        r0 = pl.multiple_of(c * chunk, chunk)
        rows = pl.ds(r0, chunk)
        eg = jnp.exp(gc_ref[pl.ds(r0 + chunk - 1, 1), :])
        st = [s_ref[h] for h in heads]
        sb = [x.astype(BF16) for x in st]
        ws = [_dot(jnp.concatenate([w_ref[rows, sls[h]], qe_ref[rows, sls[h]]], axis=0), sb[h]) for h in heads]
        vb = [(u_ref[rows, sls[h]] - ws[h][:chunk]).astype(BF16) for h in heads]
        for p in range(n_heads // 2):
            ha, hb = 2 * p, 2 * p + 1
            intra = _dot(qk_ref[rows, 2 * p * chunk:2 * (p + 1) * chunk], _pair_blockdiag(vb[ha], vb[hb]))
            o_ref[rows, 2 * p * dk:2 * (p + 1) * dk] = (
                jnp.concatenate([ws[ha][chunk:], ws[hb][chunk:]], axis=1) + intra)
        for h in heads:
            s_ref[h] = st[h] * eg[:, n_heads + h:n_heads + h + 1] + _dot_tn(ke_ref[rows, sls[h]], vb[h])
        return carry

    lax.fori_loop(0, n_sub, body, 0)

    @pl.when(j == pl.num_programs(1) - 1)
    def _():
        sout_ref[0] = s_ref[...]


def _gdn_scan(u, w, qe, ke, qk, gc, *, batch, seq, n_heads, dk, rows):
    t, vdim = u.shape
    chunk = GDN_CHUNK
    nb = seq // rows
    row = lambda width: pl.BlockSpec((rows, width), lambda b, j: (b * nb + j, 0))
    kern = functools.partial(_gdn_scan_kernel, n_heads=n_heads, dk=dk, chunk=chunk, n_sub=rows // chunk)
    return pl.pallas_call(
        kern,
        grid=(batch, nb),
        in_specs=[row(vdim)] * 4 + [row(qk.shape[1]), row(LANES)],
        out_specs=[row(vdim), pl.BlockSpec((1, n_heads, dk, dk), lambda b, j: (b, 0, 0, 0))],
        out_shape=[jax.ShapeDtypeStruct((t, vdim), F32),
                   jax.ShapeDtypeStruct((batch, n_heads, dk, dk), F32)],
        scratch_shapes=[pltpu.VMEM((n_heads, dk, dk), F32)],
        compiler_params=_cparams("arbitrary", "arbitrary"),
        name="gdn_scan",
    )(u, w, qe, ke, qk, gc)


def _gdn_step_kernel(q_ref, k_ref, v_ref, bga_ref, s_ref, o_ref, sout_ref, *, n_heads, dk, nb):
    s_ref = s_ref.at[0]
    pad = jnp.zeros((dk - n_heads, dk), F32)

    def body(b, carry):
        qt = jnp.concatenate([q_ref[b], pad], axis=0).T
        kt = jnp.concatenate([k_ref[b], pad], axis=0).T
        vb = v_ref[b]
        bg = bga_ref[pl.ds(b, 1), :]
        eg = jnp.exp(bg)
        rows = []
        for h in range(n_heads):
            st = s_ref[b, h] * eg[:, n_heads + h:n_heads + h + 1]
            kc = kt[:, h:h + 1]
            ks = jnp.sum(kc * st, axis=0, keepdims=True)
            delta = (vb[h:h + 1, :] - ks) * bg[:, h:h + 1]
            st = st + kc * delta
            sout_ref[b, h] = st
            rows.append(jnp.sum(qt[:, h:h + 1] * st, axis=0, keepdims=True))
        o_ref[b] = jnp.concatenate(rows, axis=0)
        return carry

    lax.fori_loop(0, nb, body, 0)


def _gdn_step(q, k, v, bga, states, layer, *, n_heads, dk, nb):
    b = q.shape[0]
    q3, k3, v3 = (a.reshape(b, n_heads, dk) for a in (q, k, v))
    vec = pl.BlockSpec((nb, n_heads, dk), lambda i: (i, 0, 0))
    st = pl.BlockSpec((nb, n_heads, dk, dk), lambda i: (i, 0, 0, 0))
    kern = functools.partial(_gdn_step_kernel, n_heads=n_heads, dk=dk, nb=nb)
    o, s_new = pl.pallas_call(
        kern,
        grid=(b // nb,),
        in_specs=[vec, vec, vec, pl.BlockSpec((nb, LANES), lambda i: (i, 0)),
                  pl.BlockSpec((1, nb, n_heads, dk, dk), lambda i: (layer, i, 0, 0, 0))],
        out_specs=[vec, st],
        out_shape=[jax.ShapeDtypeStruct((b, n_heads, dk), F32), jax.ShapeDtypeStruct(states.shape[1:], F32)],
        compiler_params=_cparams("arbitrary"),
        name="gdn_step",
    )(q3, k3, v3, bga, states)
    return o.reshape(b, n_heads * dk), s_new


def _gdn_out_kernel(o_ref, z_ref, x_ref, nw_ref, w_ref, y_ref, *, n_heads, dk):
    nw = nw_ref[...]
    parts = []
    for h in range(n_heads):
        sl = slice(h * dk, (h + 1) * dk)
        parts.append((_rms(o_ref[:, sl], nw, EPS) * _silu(z_ref[:, sl])).astype(BF16))
    y_ref[...] = x_ref[...] + _dot(jnp.concatenate(parts, axis=1), w_ref[...])


def _gdn_out(o, z, x, norm_w, w_out, *, tm, n_heads, dk):
    t, d = x.shape
    vdim = o.shape[1]
    row = lambda width: pl.BlockSpec((tm, width), lambda i: (i, 0))
    kern = functools.partial(_gdn_out_kernel, n_heads=n_heads, dk=dk)
    return pl.pallas_call(
        kern,
        grid=(t // tm,),
        in_specs=[row(vdim), row(vdim), row(d), _const_spec(norm_w.shape), _const_spec(w_out.shape)],
        out_specs=row(d),
        out_shape=jax.ShapeDtypeStruct((t, d), F32),
        compiler_params=_cparams("arbitrary"),
        name="gdn_out",
    )(o, z, x, norm_w, w_out)


def _ffn_tail(x, conv, up, cb_ref, wd_ref, gf_ref, o_ref, final_norm):
    act = (_silu(conv + cb_ref[...]) * up).astype(BF16)
    y = x + _dot(act, wd_ref[...])
    if final_norm:
        y = _rms(y, gf_ref[...], EPS)
    o_ref[...] = y


def _ffn_seq_kernel(x_ref, g_ref, wg_ref, wu_ref, cw_ref, cb_ref, wd_ref, gf_ref, o_ref, tail_ref, buf_ref,
                    *, tm, tiles_per_seq, final_norm):
    x = x_ref[...]
    xn = _rms(x, g_ref[...], EPS).astype(BF16)
    gate = _dot(xn, wg_ref[...])
    _seq_hist(buf_ref, gate, tiles_per_seq, tm)
    cw = cw_ref[...]
    width = cw.shape[0]
    conv = cw[width - 1:width] * gate
    for j in range(width - 1):
        conv = conv + cw[j:j + 1] * buf_ref[pl.ds(HIST_ROWS - (width - 1) + j, tm), :]
    tail_ref[0] = buf_ref[tm:tm + HIST_ROWS, :]
    _ffn_tail(x, conv, _dot(xn, wu_ref[...]), cb_ref, wd_ref, gf_ref, o_ref, final_norm)


def _ffn_step_kernel(x_ref, g_ref, wg_ref, wu_ref, cw_ref, cb_ref, wd_ref, gf_ref, *rest,
                     n_hist, final_norm):
    hist_refs = rest[:n_hist]
    o_ref, raw_ref = rest[n_hist:]
    x = x_ref[...]
    xn = _rms(x, g_ref[...], EPS).astype(BF16)
    gate = _dot(xn, wg_ref[...])
    raw_ref[...] = gate
    cw = cw_ref[...]
    width = cw.shape[0]
    conv = cw[width - 1:width] * gate
    for j in range(width - 1):
        conv = conv + cw[j:j + 1] * hist_refs[j][...]
    _ffn_tail(x, conv, _dot(xn, wu_ref[...]), cb_ref, wd_ref, gf_ref, o_ref, final_norm)


def _ffn(x, g, wg, wu, conv_w, conv_b, wd, gf, *, tm, final_norm, seq=None, hist=None):
    t, d = x.shape
    d_ff = wg.shape[1]
    row = lambda width: pl.BlockSpec((tm, width), lambda i: (i, 0))
    common = [row(d), _const_spec((1, d)), _const_spec(wg.shape), _const_spec(wu.shape),
              _const_spec(conv_w.shape), _const_spec(conv_b.shape), _const_spec(wd.shape),
              _const_spec((1, d))]
    if hist is None:
        tps = seq // tm
        kern = functools.partial(_ffn_seq_kernel, tm=tm, tiles_per_seq=tps, final_norm=final_norm)
        extra, extra_specs = [], []
        outs = [row(d), pl.BlockSpec((1, HIST_ROWS, d_ff), lambda i: (i // tps, 0, 0))]
        out_shapes = [jax.ShapeDtypeStruct((t, d), F32),
                      jax.ShapeDtypeStruct((t // seq, HIST_ROWS, d_ff), F32)]
        scratch = [pltpu.VMEM((tm + HIST_ROWS, d_ff), F32)]
    else:
        hist_arr, n_hist = hist
        kern = functools.partial(_ffn_step_kernel, n_hist=n_hist, final_norm=final_norm)
        extra = [hist_arr] * n_hist
        extra_specs = [pl.BlockSpec((tm, d_ff), functools.partial(lambda i, j: (i, j), j=j))
                       for j in range(n_hist)]
        outs = [row(d), row(d_ff)]
        out_shapes = [jax.ShapeDtypeStruct((t, d), F32), jax.ShapeDtypeStruct((t, d_ff), F32)]
        scratch = []
    return pl.pallas_call(
        kern,
        grid=(t // tm,),
        in_specs=common + extra_specs,
        out_specs=outs,
        out_shape=out_shapes,
        scratch_shapes=scratch,
        compiler_params=_cparams("arbitrary"),
        name="conv_ffn",
    )(x, g, wg, wu, conv_w, conv_b, wd, gf, *extra)


def _tile(n, pref):
    t = min(n, pref)
    while n % t:
        t //= 2
    return t


def kernel(x_prompt, x_sample, cache_k, cache_v, state_sconv, state_gdn_conv, state_gdn, state_ffn_conv, page_table, norm_mix, norm_ffn, norm_final, w_in_even, w_out_even, lambda_q1, lambda_k1, lambda_q2, lambda_k2, subln_w, sconv_w, w_in_odd, gdn_conv_w, a_log, dt_bias, gdn_norm_w, w_out_odd, w_up, ffn_conv_w, ffn_conv_b, w_down):
    batch, seq, d = x_prompt.shape
    dec_b = x_sample.shape[0]
    depth = norm_mix.shape[0]
    n_heads_a, e_a = cache_k.shape[3], cache_k.shape[4]
    n_att = n_heads_a * e_a
    rot = (e_a // 2) // 4
    w_b = state_sconv.shape[-1]
    sconv = sconv_w.shape[1]
    page = cache_k.shape[2]
    past_len = page_table.shape[1] * page
    n_heads_c, dk = state_gdn.shape[2], state_gdn.shape[3]
    kdim = n_heads_c * dk
    qkv_c = gdn_conv_w.shape[-1]
    gdn_conv = gdn_conv_w.shape[1]
    d_ff = ffn_conv_w.shape[-1]
    ffn_conv = ffn_conv_w.shape[1]
    t_p = batch * seq

    tm_p = _tile(seq, 256)
    tm_ffn = _tile(seq, 512)
    blk = _tile(seq, 512)
    scan_rows = _tile(seq, 512)

    xp = x_prompt.reshape(t_p, d)
    xs = x_sample.reshape(dec_b, d)
    rope_p = _rope_tables(jnp.arange(seq), e_a, rot)
    rope_s = _rope_tables(jnp.full((dec_b,), past_len), e_a, rot)
    gf = norm_final.reshape(1, d)

    kp, vp, ks, vs, scp, scs = [], [], [], [], [], []
    gcp, gcs, gsp, gss, fcp, fcs = [], [], [], [], [], []
    for l in range(depth):
        gm = norm_mix[l].reshape(1, d)
        if l % 2 == 0:
            e = l // 2
            lam_init = 0.8 - 0.6 * math.exp(-0.3 * l)
            w_in = w_in_even[e].astype(BF16)
            wvt = w_in_even[e][:, 2 * n_att:3 * n_att].T.astype(BF16)
            w_out = w_out_even[e].astype(BF16)
            lam_p = jnp.stack([lambda_q1[e], lambda_k1[e], lambda_q2[e], lambda_k2[e]])
            sw = subln_w[e].reshape(1, e_a)
            proj = functools.partial(_even_proj, n_heads=n_heads_a, e_a=e_a, w_b=w_b, half=rot // 2)
            q, k, v, kb, vt, bg, cx = proj(xp, gm, w_in, wvt, rope_p, tm=tm_p)
            att = _attn_prompt(q, kb, vt, lam_p, sw, batch=batch, seq=seq, n_heads=n_heads_a, e_a=e_a,
                               blk=blk, lam_init=lam_init)
            xp = _even_out(att, bg, cx, xp, sconv_w[e], w_out, tm=tm_p, seq=seq)
            kp.append(k.reshape(batch, seq, n_heads_a, e_a))
            vp.append(v.reshape(batch, seq, n_heads_a, e_a))
            cx3 = cx.reshape(batch, seq, w_b)
            if seq >= sconv - 1:
                scp.append(cx3[:, seq - (sconv - 1):])
            else:
                scp.append(jnp.concatenate([jnp.zeros((batch, sconv - 1 - seq, w_b), F32), cx3], axis=1))
            q, k, v, kb, vt, bg, cx = proj(xs, gm, w_in, wvt, rope_s, tm=dec_b)
            att = _attn_sample(q, k, v, cache_k, cache_v, e, page_table, lam_p, sw,
                               n_heads=n_heads_a, e_a=e_a, lam_init=lam_init)
            hist = [state_sconv[e, :, j] for j in range(sconv - 1)]
            xs = _even_out(att, bg, cx, xs, sconv_w[e], w_out, tm=dec_b, hist=hist)
            ks.append(k.reshape(dec_b, 1, n_heads_a, e_a))
            vs.append(v.reshape(dec_b, 1, n_heads_a, e_a))
            scs.append(jnp.concatenate([state_sconv[e, :, 1:], cx[:, None, :]], axis=1))
        else:
            o = l // 2
            w_in = w_in_odd[o]
            wqkv = w_in[:, :qkv_c].astype(BF16)
            wz = w_in[:, qkv_c:qkv_c + kdim].astype(BF16)
            wba = jnp.pad(w_in[:, qkv_c + kdim:], ((0, 0), (0, LANES - 2 * n_heads_c))).astype(BF16)
            w_out = w_out_odd[o].astype(BF16)
            lane_pad = lambda a: jnp.pad(a, (n_heads_c, LANES - 2 * n_heads_c)).reshape(1, LANES)
            al_pad, dtb_pad = lane_pad(a_log[o]), lane_pad(dt_bias[o])
            nw = gdn_norm_w[o].reshape(1, dk)
            proj = functools.partial(_gdn_proj, n_heads=n_heads_c, dk=dk)
            q, k, v, z, bga, tail = proj(xp, gm, wqkv, wz, wba, gdn_conv_w[o], al_pad, dtb_pad,
                                         tm=tm_p, seq=seq)
            u, w, qe, ke, qk, gc = _gdn_prep(q, k, v, bga, n_heads=n_heads_c, dk=dk,
                                             n_sub=_tile(seq // GDN_CHUNK, 2))
            og, s_new = _gdn_scan(u, w, qe, ke, qk, gc, batch=batch, seq=seq, n_heads=n_heads_c, dk=dk,
                                  rows=scan_rows)
            xp = _gdn_out(og, z, xp, nw, w_out, tm=tm_p, n_heads=n_heads_c, dk=dk)
            gcp.append(tail[:, HIST_ROWS - (gdn_conv - 1):])
            gsp.append(s_new)
            hist_arr = state_gdn_conv[o].reshape(dec_b, (gdn_conv - 1) * qkv_c)
            q, k, v, z, bga, raw = proj(xs, gm, wqkv, wz, wba, gdn_conv_w[o], al_pad, dtb_pad,
                                        tm=dec_b, hist=(hist_arr, gdn_conv - 1))
            og, s_new = _gdn_step(q, k, v, bga, state_gdn, o, n_heads=n_heads_c, dk=dk, nb=_tile(dec_b, 8))
            xs = _gdn_out(og, z, xs, nw, w_out, tm=dec_b, n_heads=n_heads_c, dk=dk)
            gcs.append(jnp.concatenate([state_gdn_conv[o, :, 1:], raw[:, None, :]], axis=1))
            gss.append(s_new)
        gn = norm_ffn[l].reshape(1, d)
        wg = w_up[l, :, :d_ff].astype(BF16)
        wu = w_up[l, :, d_ff:].astype(BF16)
        wd = w_down[l].astype(BF16)
        cb = ffn_conv_b[l].reshape(1, d_ff)
        last = l == depth - 1
        xp, tail = _ffn(xp, gn, wg, wu, ffn_conv_w[l], cb, wd, gf, tm=tm_ffn, final_norm=last, seq=seq)
        fcp.append(tail[:, HIST_ROWS - (ffn_conv - 1):])
        hist_arr = state_ffn_conv[l].reshape(dec_b, (ffn_conv - 1) * d_ff)
        xs, raw = _ffn(xs, gn, wg, wu, ffn_conv_w[l], cb, wd, gf, tm=dec_b, final_norm=last,
                       hist=(hist_arr, ffn_conv - 1))
        fcs.append(jnp.concatenate([state_ffn_conv[l, :, 1:], raw[:, None, :]], axis=1))
    return (xp.reshape(batch, seq, d), xs.reshape(dec_b, 1, d),
            jnp.stack(kp), jnp.stack(vp), jnp.stack(ks), jnp.stack(vs),
            jnp.stack(scp), jnp.stack(scs),
            jnp.stack(gcp), jnp.stack(gcs), jnp.stack(gsp), jnp.stack(gss),
            jnp.stack(fcp), jnp.stack(fcs))
```

```python
import functools
import math

import jax
import jax.numpy as jnp
from jax import lax
from jax.experimental import pallas as pl
from jax.experimental.pallas import tpu as pltpu

F32 = jnp.float32
BF16 = jnp.bfloat16

EPS = 1e-6
SUBLN_EPS = 1e-5
ROPE_THETA = 500000.0
GDN_CHUNK = 64
V7X_VMEM_LIMIT = 56 * 1024 * 1024
LANES = 128
HIST_ROWS = 8
BF16_ROWS = 16
ATTN_SUB = 2


def _cparams(*sem):
    return pltpu.CompilerParams(dimension_semantics=sem, vmem_limit_bytes=V7X_VMEM_LIMIT)


def _const_spec(shape):
    nd = len(shape)
    return pl.BlockSpec(shape, lambda i: (0,) * nd, pipeline_mode=pl.Buffered(1))


def _rms(x, g, eps):
    return x * lax.rsqrt(jnp.mean(x * x, axis=-1, keepdims=True) + eps) * g


def _sigmoid(x):
    return 0.5 * jnp.tanh(0.5 * x) + 0.5


def _silu(x):
    h = 0.5 * x
    return h * jnp.tanh(h) + h


def _softplus(x):
    return jnp.maximum(x, 0.0) + jnp.log1p(jnp.exp(-jnp.abs(x)))


def _dot(a, b):
    return jnp.dot(a, b, preferred_element_type=F32)


def _dot_nt(a, b):
    return lax.dot_general(a, b, (((1,), (1,)), ((), ())), preferred_element_type=F32)


def _dot_tn(a, b):
    return lax.dot_general(a, b, (((0,), (0,)), ((), ())), preferred_element_type=F32)


def _split3(a):
    hi = a.astype(BF16)
    r = a - hi.astype(F32)
    mid = r.astype(BF16)
    lo = (r - mid.astype(F32)).astype(BF16)
    return hi, mid, lo


def _even_proj_kernel(x_ref, g_ref, w_ref, wvt_ref, c_ref, s1_ref, s2_ref,
                      q_ref, k_ref, v_ref, kb_ref, vt_ref, bg_ref, cx_ref,
                      *, tm, n_heads, e_a, w_b, half, qscale):
    xn = _rms(x_ref[...], g_ref[...], EPS).astype(BF16)
    n = n_heads * e_a
    c, s1, s2 = c_ref[...], s1_ref[...], s2_ref[...]

    def rope(t):
        return t * c + pltpu.roll(t, e_a - half, 1) * s1 + pltpu.roll(t, half, 1) * s2

    hqk = _dot(xn, w_ref[:, :2 * n])
    for h in range(n_heads):
        sl = slice(h * e_a, (h + 1) * e_a)
        q_ref[:, sl] = (rope(hqk[:, sl]) * qscale).astype(BF16)
        kr = rope(hqk[:, n + h * e_a:n + (h + 1) * e_a])
        k_ref[pl.ds(h, tm, stride=n_heads), :] = kr
        kb_ref[:, sl] = kr.astype(BF16)
    hv = _dot(xn, w_ref[:, 2 * n:3 * n])
    for h in range(n_heads):
        v_ref[pl.ds(h, tm, stride=n_heads), :] = hv[:, h * e_a:(h + 1) * e_a]
    vt_ref[...] = _dot_nt(wvt_ref[...], xn).astype(BF16)
    o = 3 * n
    bg_ref[...] = _dot(xn, w_ref[:, o:o + w_b])
    hc = _dot(xn, w_ref[:, o + w_b:o + 3 * w_b])
    cx_ref[...] = hc[:, :w_b] * hc[:, w_b:]


def _even_proj(x, g, w, wvt, rope_tabs, *, tm, n_heads, e_a, w_b, half):
    t, d = x.shape
    n = n_heads * e_a
    c, s1, s2 = rope_tabs
    row = lambda width: pl.BlockSpec((tm, width), lambda i: (i, 0))
    tab_tiles = c.shape[0] // tm
    tab = pl.BlockSpec((tm, e_a), lambda i: (i % tab_tiles, 0))
    tok = pl.BlockSpec((tm * n_heads, e_a), lambda i: (i, 0))
    kern = functools.partial(_even_proj_kernel, tm=tm, n_heads=n_heads, e_a=e_a, w_b=w_b, half=half,
                             qscale=(e_a // 2) ** -0.5)
    return pl.pallas_call(
        kern,
        grid=(t // tm,),
        in_specs=[row(d), _const_spec((1, d)), _const_spec(w.shape), _const_spec(wvt.shape),
                  tab, tab, tab],
        out_specs=[row(n), tok, tok, row(n), pl.BlockSpec((n, tm), lambda i: (0, i)),
                   row(w_b), row(w_b)],
        out_shape=[jax.ShapeDtypeStruct((t, n), BF16), jax.ShapeDtypeStruct((t * n_heads, e_a), F32),
                   jax.ShapeDtypeStruct((t * n_heads, e_a), F32), jax.ShapeDtypeStruct((t, n), BF16),
                   jax.ShapeDtypeStruct((n, t), BF16), jax.ShapeDtypeStruct((t, w_b), F32),
                   jax.ShapeDtypeStruct((t, w_b), F32)],
        compiler_params=_cparams("arbitrary"),
        name="even_proj",
    )(x, g, w, wvt, c, s1, s2)


def _rope_tables(pos, e_a, rot, dtype=F32):
    half = rot // 2
    dh = e_a // 2
    inv = ROPE_THETA ** (-jnp.arange(half, dtype=F32) * 2.0 / rot)
    ang = pos.astype(F32)[:, None] * inv[None, :]
    cos, sin = jnp.cos(ang).astype(dtype), jnp.sin(ang).astype(dtype)
    t = pos.shape[0]
    ones = jnp.ones((t, dh - rot), dtype)
    zeros_h = jnp.zeros((t, half), dtype)
    zeros_r = jnp.zeros((t, dh - rot), dtype)
    c = jnp.concatenate([cos, cos, ones], axis=1)
    s1 = jnp.concatenate([-sin, zeros_h, zeros_r], axis=1)
    s2 = jnp.concatenate([zeros_h, sin, zeros_r], axis=1)
    rep = lambda a: jnp.concatenate([a, a], axis=1)
    return rep(c), rep(s1), rep(s2)


def _lambda(lp_ref, lam_init):
    lp = lp_ref[...]
    a = jnp.sum(lp[0:1] * lp[1:2], axis=-1, keepdims=True)
    b = jnp.sum(lp[2:3] * lp[3:4], axis=-1, keepdims=True)
    return jnp.exp(a) - jnp.exp(b) + lam_init


def _attn_prompt_kernel(q_ref, k_ref, vt_ref, lp_ref, swc_ref, o_ref, *, blk, n_sub, dh, lam_init):
    i = pl.program_id(2)
    qt = q_ref[...].astype(F32).T
    feat = lax.broadcasted_iota(jnp.int32, qt.shape, 0)
    qs = jnp.concatenate([jnp.where(feat < dh, qt, 0.0), jnp.where(feat >= dh, qt, 0.0)],
                         axis=1).astype(BF16)
    e_a = qt.shape[0]
    sub = blk // n_sub

    def step(off, carry, diag):
        def scores(t):
            s = _dot(k_ref[pl.ds(off + t * sub, sub), :], qs)
            if diag:
                key = lax.broadcasted_iota(jnp.int32, s.shape, 0) + t * sub
                qry = lax.broadcasted_iota(jnp.int32, s.shape, 1)
                qry = jnp.where(qry >= blk, qry - blk, qry)
                s = jnp.where(key <= qry, s, jnp.finfo(F32).min)
            return s

        ss = [scores(t) for t in range(n_sub)]
        m, l, acc = carry
        for t in range(n_sub):
            m_new = jnp.maximum(m, jnp.max(ss[t], axis=0, keepdims=True))
            p = jnp.exp(ss[t] - m_new)
            alpha = jnp.exp(m - m_new)
            l = alpha * l + jnp.sum(p, axis=0, keepdims=True)
            acc = alpha * acc + _dot(vt_ref[:, pl.ds(off + t * sub, sub)], p.astype(BF16))
            m = m_new
        return m, l, acc

    def body(j, carry):
        return step(pl.multiple_of(j * blk, blk), carry, False)

    init = (jnp.full((1, 2 * blk), -1e30, F32), jnp.zeros((1, 2 * blk), F32),
            jnp.zeros((e_a, 2 * blk), F32))
    carry = lax.fori_loop(0, i, body, init)
    _, l, acc = step(pl.multiple_of(i * blk, blk), carry, True)
    o = acc / l
    lam = _lambda(lp_ref, lam_init)
    att = o[:, :blk] - lam * o[:, blk:]
    y = att * lax.rsqrt(jnp.mean(att * att, axis=0, keepdims=True) + SUBLN_EPS) * swc_ref[...]
    o_ref[...] = (y * (1.0 - lam_init)).T.astype(o_ref.dtype)


def _attn_prompt(q, kb, vt, lam_p, subln_w, *, batch, seq, n_heads, e_a, blk, lam_init):
    t = q.shape[0]
    nq = seq // blk
    swc = subln_w.reshape(e_a, 1)
    n_sub = ATTN_SUB if blk % (ATTN_SUB * LANES) == 0 else 1
    kern = functools.partial(_attn_prompt_kernel, blk=blk, n_sub=n_sub, dh=e_a // 2, lam_init=lam_init)
    return pl.pallas_call(
        kern,
        grid=(batch, n_heads, nq),
        in_specs=[pl.BlockSpec((blk, e_a), lambda b, h, i: (b * nq + i, h)),
                  pl.BlockSpec((seq, e_a), lambda b, h, i: (b, h)),
                  pl.BlockSpec((e_a, seq), lambda b, h, i: (h, b)),
                  pl.BlockSpec(lam_p.shape, lambda b, h, i: (0, 0)),
                  pl.BlockSpec(swc.shape, lambda b, h, i: (0, 0))],
        out_specs=pl.BlockSpec((blk, e_a), lambda b, h, i: (b * nq + i, h)),
        out_shape=jax.ShapeDtypeStruct((t, n_heads * e_a), BF16),
        compiler_params=_cparams("arbitrary", "arbitrary", "arbitrary"),
        name="attn_prompt",
    )(q, kb, vt, lam_p, swc)


def _attn_sample_kernel(pt_ref, q_ref, kn_ref, vn_ref, lp_ref, sw_ref, *rest,
                        n_pages, n_heads, e_a, lam_init):
    del pt_ref
    k_refs = rest[:n_pages]
    v_refs = rest[n_pages:2 * n_pages]
    o_ref = rest[2 * n_pages]
    dh = e_a // 2
    nm = max(2 * n_heads, BF16_ROWS)
    rows_pp = k_refs[0].shape[1]
    rowi = lax.broadcasted_iota(jnp.int32, (nm, e_a), 0)
    lane = lax.broadcasted_iota(jnp.int32, (nm, e_a), 1)

    def per_head_rows(ref):
        t = ref[0]
        out = jnp.zeros((nm, e_a), F32)
        for h in range(n_heads):
            out = jnp.where((rowi >> 1) == h, jnp.broadcast_to(t[h:h + 1].astype(F32), (nm, e_a)), out)
        return out

    qm_f = jnp.where((lane // dh) == (rowi & 1), per_head_rows(q_ref), 0.0)
    qm = qm_f.astype(BF16)
    colh = lax.broadcasted_iota(jnp.int32, (nm, rows_pp), 1) % n_heads
    valid = colh == (lax.broadcasted_iota(jnp.int32, (nm, rows_pp), 0) >> 1)
    ss = [jnp.where(valid, _dot_nt(qm, k_refs[p][0].astype(BF16)), jnp.finfo(F32).min)
          for p in range(n_pages)]
    kn = per_head_rows(kn_ref).astype(BF16).astype(F32)
    vn = per_head_rows(vn_ref).astype(BF16).astype(F32)
    s_new = jnp.sum(qm.astype(F32) * kn, axis=-1, keepdims=True)
    m = s_new
    for s in ss:
        m = jnp.maximum(m, jnp.max(s, axis=-1, keepdims=True))
    p_new = jnp.exp(s_new - m)
    l = p_new
    acc = p_new.astype(BF16).astype(F32) * vn
    for p in range(n_pages):
        pr = jnp.exp(ss[p] - m)
        l = l + jnp.sum(pr, axis=-1, keepdims=True)
        acc = acc + _dot(pr.astype(BF16), v_refs[p][0].astype(BF16))
    o = acc / l
    lam = _lambda(lp_ref, lam_init)
    sw = sw_ref[...]
    for h in range(n_heads):
        att = o[2 * h:2 * h + 1] - lam * o[2 * h + 1:2 * h + 2]
        o_ref[0, h:h + 1, :] = (_rms(att, sw, SUBLN_EPS) * (1.0 - lam_init)).astype(o_ref.dtype)


def _attn_sample(q, k_new, v_new, cache_k, cache_v, layer, page_table, lam_p, subln_w,
                 *, n_heads, e_a, lam_init):
    b, n = q.shape
    n_pages = page_table.shape[1]
    n_phys, page = cache_k.shape[1], cache_k.shape[2]
    ck = cache_k.reshape(cache_k.shape[0] * n_phys, page * n_heads, e_a)
    cv = cache_v.reshape(cache_v.shape[0] * n_phys, page * n_heads, e_a)
    base = layer * n_phys
    tok = pl.BlockSpec((1, n_heads, e_a), lambda i, pt: (i, 0, 0))
    page_specs = [pl.BlockSpec((1, page * n_heads, e_a), functools.partial(
        lambda i, pt, p: (base + pt[i, p], 0, 0), p=p)) for p in range(n_pages)]
    kern = functools.partial(_attn_sample_kernel, n_pages=n_pages, n_heads=n_heads, e_a=e_a,
                             lam_init=lam_init)
    grid_spec = pltpu.PrefetchScalarGridSpec(
        num_scalar_prefetch=1,
        grid=(b,),
        in_specs=[tok, tok, tok,
                  pl.BlockSpec(lam_p.shape, lambda i, pt: (0, 0)),
                  pl.BlockSpec(subln_w.shape, lambda i, pt: (0, 0))] + page_specs + page_specs,
        out_specs=tok,
    )
    tok3 = lambda a: a.reshape(b, n_heads, e_a)
    out = pl.pallas_call(
        kern,
        grid_spec=grid_spec,
        out_shape=jax.ShapeDtypeStruct((b, n_heads, e_a), F32),
        compiler_params=_cparams("arbitrary"),
        name="attn_sample",
    )(page_table, tok3(q).astype(F32), tok3(k_new), tok3(v_new), lam_p, subln_w,
      *([ck] * n_pages), *([cv] * n_pages))
    return out.reshape(b, n)


def _seq_hist(buf_ref, cur, tiles_per_seq, tm):
    first = (pl.program_id(0) % tiles_per_seq) == 0

    @pl.when(first)
    def _():
        buf_ref[0:HIST_ROWS, :] = jnp.zeros((HIST_ROWS, buf_ref.shape[1]), buf_ref.dtype)

    @pl.when(jnp.logical_not(first))
    def _():
        buf_ref[0:HIST_ROWS, :] = buf_ref[tm:tm + HIST_ROWS, :]

    buf_ref[HIST_ROWS:HIST_ROWS + tm, :] = cur


def _even_out_seq_kernel(att_ref, bg_ref, cx_ref, x_ref, cw_ref, w_ref, o_ref, buf_ref,
                         *, tm, tiles_per_seq, n_att):
    cx = cx_ref[...]
    _seq_hist(buf_ref, cx, tiles_per_seq, tm)
    cw = cw_ref[...]
    width = cw.shape[0]
    conv = cw[width - 1:width] * cx
    for j in range(width - 1):
        conv = conv + cw[j:j + 1] * buf_ref[pl.ds(HIST_ROWS - (width - 1) + j, tm), :]
    yb = (bg_ref[...] * conv).astype(BF16)
    o_ref[...] = (x_ref[...] + _dot(att_ref[...].astype(BF16), w_ref[:n_att, :])
                  + _dot(yb, w_ref[n_att:, :]))


def _even_out_step_kernel(att_ref, bg_ref, cx_ref, x_ref, cw_ref, w_ref, *rest, n_att):
    hist_refs, o_ref = rest[:-1], rest[-1]
    cw = cw_ref[...]
    width = cw.shape[0]
    conv = cw[width - 1:width] * cx_ref[...]
    for j in range(width - 1):
        conv = conv + cw[j:j + 1] * hist_refs[j][...]
    yb = (bg_ref[...] * conv).astype(BF16)
    o_ref[...] = (x_ref[...] + _dot(att_ref[...].astype(BF16), w_ref[:n_att, :])
                  + _dot(yb, w_ref[n_att:, :]))


def _even_out(att, bg, cx, x, conv_w, w_out, *, tm, seq=None, hist=None):
    t, d = x.shape
    n_att, w_b = att.shape[1], bg.shape[1]
    row = lambda width: pl.BlockSpec((tm, width), lambda i: (i, 0))
    common = [row(n_att), row(w_b), row(w_b), row(d), _const_spec(conv_w.shape), _const_spec(w_out.shape)]
    if hist is None:
        kern = functools.partial(_even_out_seq_kernel, tm=tm, tiles_per_seq=seq // tm, n_att=n_att)
        extra, extra_specs = [], []
        scratch = [pltpu.VMEM((tm + HIST_ROWS, w_b), F32)]
    else:
        kern = functools.partial(_even_out_step_kernel, n_att=n_att)
        extra, extra_specs = list(hist), [row(w_b)] * len(hist)
        scratch = []
    return pl.pallas_call(
        kern,
        grid=(t // tm,),
        in_specs=common + extra_specs,
        out_specs=row(d),
        out_shape=jax.ShapeDtypeStruct((t, d), F32),
        scratch_shapes=scratch,
        compiler_params=_cparams("arbitrary"),
        name="even_out",
    )(att, bg, cx, x, conv_w, w_out, *extra)


def _gdn_gates(xn, wz_ref, wba_ref, al_ref, dtb_ref, z_ref, bga_ref, *, n_heads):
    z_ref[...] = _dot(xn, wz_ref[...])
    ba = _dot(xn, wba_ref[...])
    beta = _sigmoid(ba)
    g = -jnp.exp(al_ref[...]) * _softplus(ba + dtb_ref[...])
    lane = lax.broadcasted_iota(jnp.int32, ba.shape, 1)
    bga_ref[...] = jnp.where(lane < n_heads, beta, g)


def _gdn_qkv_chunk(conv, c0, q_ref, k_ref, v_ref, *, n_heads, dk):
    kdim = n_heads * dk
    act = _silu(conv)
    for i in range(conv.shape[1] // dk):
        t = act[:, i * dk:(i + 1) * dk]
        ch = c0 + i * dk
        if ch < 2 * kdim:
            t = t * lax.rsqrt(jnp.sum(t * t, axis=-1, keepdims=True) + EPS)
        if ch < kdim:
            q_ref[:, ch:ch + dk] = t * (dk ** -0.5)
        elif ch < 2 * kdim:
            k_ref[:, ch - kdim:ch - kdim + dk] = t
        else:
            v_ref[:, ch - 2 * kdim:ch - 2 * kdim + dk] = t


def _gdn_proj_seq_kernel(x_ref, g_ref, wqkv_ref, wz_ref, wba_ref, cw_ref, al_ref, dtb_ref,
                         q_ref, k_ref, v_ref, z_ref, bga_ref, tail_ref, buf_ref,
                         *, tm, tiles_per_seq, n_heads, dk, cwid):
    xn = _rms(x_ref[...], g_ref[...], EPS).astype(BF16)
    first = (pl.program_id(0) % tiles_per_seq) == 0

    @pl.when(first)
    def _():
        buf_ref[0:HIST_ROWS, :] = jnp.zeros((HIST_ROWS, buf_ref.shape[1]), buf_ref.dtype)

    @pl.when(jnp.logical_not(first))
    def _():
        buf_ref[0:HIST_ROWS, :] = buf_ref[tm:tm + HIST_ROWS, :]

    cw = cw_ref[...]
    width = cw.shape[0]
    n_chunks = wqkv_ref.shape[1] // cwid
    mm = lambda c: _dot(xn, wqkv_ref[:, c * cwid:(c + 1) * cwid])
    h_next = mm(0)
    for c in range(n_chunks):
        cs = slice(c * cwid, (c + 1) * cwid)
        hm = h_next
        if c + 1 < n_chunks:
            h_next = mm(c + 1)
        buf_ref[HIST_ROWS:HIST_ROWS + tm, cs] = hm
        conv = cw[width - 1:width, cs] * hm
        for j in range(width - 1):
            conv = conv + cw[j:j + 1, cs] * buf_ref[pl.ds(HIST_ROWS - (width - 1) + j, tm), cs]
        _gdn_qkv_chunk(conv, c * cwid, q_ref, k_ref, v_ref, n_heads=n_heads, dk=dk)
    tail_ref[0] = buf_ref[tm:tm + HIST_ROWS, :]
    _gdn_gates(xn, wz_ref, wba_ref, al_ref, dtb_ref, z_ref, bga_ref, n_heads=n_heads)


def _gdn_proj_step_kernel(x_ref, g_ref, wqkv_ref, wz_ref, wba_ref, cw_ref, al_ref, dtb_ref, *rest,
                          n_heads, dk, n_hist):
    hist_refs = rest[:n_hist]
    q_ref, k_ref, v_ref, z_ref, bga_ref, raw_ref = rest[n_hist:]
    xn = _rms(x_ref[...], g_ref[...], EPS).astype(BF16)
    hm = _dot(xn, wqkv_ref[...])
    raw_ref[...] = hm
    cw = cw_ref[...]
    width = cw.shape[0]
    conv = cw[width - 1:width] * hm
    for j in range(width - 1):
        conv = conv + cw[j:j + 1] * hist_refs[j][...]
    _gdn_qkv_chunk(conv, 0, q_ref, k_ref, v_ref, n_heads=n_heads, dk=dk)
    _gdn_gates(xn, wz_ref, wba_ref, al_ref, dtb_ref, z_ref, bga_ref, n_heads=n_heads)


def _gdn_proj(x, g, wqkv, wz, wba, conv_w, al_pad, dtb_pad, *, tm, n_heads, dk, seq=None, hist=None):
    t, d = x.shape
    qkv_c = wqkv.shape[1]
    vdim = wz.shape[1]
    row = lambda width: pl.BlockSpec((tm, width), lambda i: (i, 0))
    common = [row(d), _const_spec((1, d)), _const_spec(wqkv.shape), _const_spec(wz.shape),
              _const_spec(wba.shape), _const_spec(conv_w.shape), _const_spec(al_pad.shape),
              _const_spec(dtb_pad.shape)]
    outs = [row(vdim), row(vdim), row(vdim), row(vdim), row(LANES)]
    out_shapes = [jax.ShapeDtypeStruct((t, vdim), F32)] * 4 + [jax.ShapeDtypeStruct((t, LANES), F32)]
    if hist is None:
        tps = seq // tm
        kern = functools.partial(_gdn_proj_seq_kernel, tm=tm, tiles_per_seq=tps, n_heads=n_heads, dk=dk,
                                 cwid=2 * dk)
        extra, extra_specs = [], []
        outs.append(pl.BlockSpec((1, HIST_ROWS, qkv_c), lambda i: (i // tps, 0, 0)))
        out_shapes.append(jax.ShapeDtypeStruct((t // seq, HIST_ROWS, qkv_c), F32))
        scratch = [pltpu.VMEM((tm + HIST_ROWS, qkv_c), F32)]
    else:
        hist_arr, n_hist = hist
        kern = functools.partial(_gdn_proj_step_kernel, n_heads=n_heads, dk=dk, n_hist=n_hist)
        extra = [hist_arr] * n_hist
        extra_specs = [pl.BlockSpec((tm, qkv_c), functools.partial(lambda i, j: (i, j), j=j))
                       for j in range(n_hist)]
        outs.append(row(qkv_c))
        out_shapes.append(jax.ShapeDtypeStruct((t, qkv_c), F32))
        scratch = []
    return pl.pallas_call(
        kern,
        grid=(t // tm,),
        in_specs=common + extra_specs,
        out_specs=outs,
        out_shape=out_shapes,
        scratch_shapes=scratch,
        compiler_params=_cparams("arbitrary"),
        name="gdn_proj",
    )(x, g, wqkv, wz, wba, conv_w, al_pad, dtb_pad, *extra)


def _pair_blockdiag(a, b):
    z = jnp.zeros_like(a)
    return jnp.concatenate([jnp.concatenate([a, z], axis=1), jnp.concatenate([z, b], axis=1)], axis=0)


def _gdn_prep_kernel(q_ref, k_ref, v_ref, bga_ref, u_ref, w_ref, qe_ref, ke_ref, qk_ref, gc_ref,
                     *, n_heads, dk, chunk, n_sub):
    n_pairs = n_heads // 2
    rows_blk = chunk * n_sub
    bga = bga_ref[...]
    r = lax.broadcasted_iota(jnp.int32, (rows_blk, rows_blk), 0)
    c = lax.broadcasted_iota(jnp.int32, (rows_blk, rows_blk), 1)
    sh = chunk.bit_length() - 1
    tril = jnp.where(jnp.logical_and(r >= c, (r >> sh) == (c >> sh)), 1.0, 0.0).astype(BF16)
    b3 = _split3(bga)
    gc = _dot(tril, b3[0]) + (_dot(tril, b3[1]) + _dot(tril, b3[2]))
    gc_ref[...] = gc
    lane = lax.broadcasted_iota(jnp.int32, (rows_blk, LANES), 1)
    is_g = jnp.logical_and(lane >= n_heads, lane < 2 * n_heads)
    odd = ((lane - n_heads) & 1) == 1
    gce = jnp.where(jnp.logical_and(is_g, jnp.logical_not(odd)), gc, 0.0)
    gco = jnp.where(jnp.logical_and(is_g, odd), gc, 0.0)
    ar = lax.broadcasted_iota(jnp.int32, (BF16_ROWS, LANES), 0)
    al = lax.broadcasted_iota(jnp.int32, (BF16_ROWS, LANES), 1)
    asel = jnp.where(jnp.logical_and(jnp.logical_and(al >= n_heads, ar < n_pairs),
                                     ((al - n_heads) >> 1) == ar), 1.0, 0.0).astype(BF16)
    prow = lax.broadcasted_iota(jnp.int32, (chunk, 2 * chunk), 0)
    plane = lax.broadcasted_iota(jnp.int32, (chunk, 2 * chunk), 1)
    pcol = plane & (chunk - 1)
    lo_half = plane < chunk
    incl = prow >= pcol
    strict = prow > pcol

    def bd(m):
        return jnp.concatenate([jnp.where(lo_half, m, 0.0), jnp.where(lo_half, 0.0, m)], axis=0).astype(BF16)

    items = [(ci, p) for ci in range(n_sub) for p in range(n_pairs)]
    lmat, tinv, vbn, kbe = {}, {}, {}, {}
    for ci, p in items:
        rs = slice(ci * chunk, (ci + 1) * chunk)
        cs = slice(2 * p * dk, 2 * (p + 1) * dk)
        g3 = _split3(jnp.concatenate([gce[rs], gco[rs]], axis=0))
        grow = _dot_nt(asel, g3[0]) + (_dot_nt(asel, g3[1]) + _dot_nt(asel, g3[2]))
        k2, q2, v2 = k_ref[rs, cs], q_ref[rs, cs], v_ref[rs, cs]
        ha, hb = 2 * p, 2 * p + 1

        def nat(a, b, rows=chunk):
            return jnp.concatenate([jnp.broadcast_to(a, (rows, dk)), jnp.broadcast_to(b, (rows, dk))], axis=1)

        gca, gcb = gc[rs, n_heads + ha:n_heads + ha + 1], gc[rs, n_heads + hb:n_heads + hb + 1]
        bnat = nat(bga[rs, ha:ha + 1], bga[rs, hb:hb + 1])
        gnat = nat(gca, gcb)
        glast = nat(gca[chunk - 1:chunk], gcb[chunk - 1:chunk], rows=1)
        egn = jnp.exp(gnat)
        kb2 = k2 * bnat
        k2b = k2.astype(BF16)
        rhs_t = _pair_blockdiag(k2b[:, :dk], k2b[:, dk:])
        decay = jnp.where(incl, jnp.exp(jnp.where(incl, jnp.where(lo_half, gca, gcb) - grow[p:p + 1], 0.0)), 0.0)
        lm = jnp.where(strict, _dot_nt(kb2.astype(BF16), rhs_t) * decay, 0.0)
        qk_ref[rs, 2 * p * chunk:2 * (p + 1) * chunk] = jnp.where(
            incl, _dot_nt(q2.astype(BF16), rhs_t) * decay, 0.0).astype(BF16)
        qe_ref[rs, cs] = (q2 * egn).astype(BF16)
        ke_ref[rs, cs] = (k2 * jnp.exp(glast - gnat)).astype(BF16)
        lmat[ci, p] = lm
        tinv[ci, p] = jnp.where(prow == pcol, 1.0, 0.0) - jnp.where((prow >> 1) == (pcol >> 1), lm, 0.0)
        vbn[ci, p] = (v2 * bnat).astype(BF16)
        kbe[ci, p] = (kb2 * egn).astype(BF16)
    s = 1
    while 2 * (1 << s) <= chunk:
        joins = jnp.logical_and((prow >> (s + 1)) == (pcol >> (s + 1)), (prow >> s) != (pcol >> s))
        x = {it: _dot(tinv[it].astype(BF16), bd(jnp.where(joins, lmat[it], 0.0))) for it in items}
        tinv = {it: tinv[it] - _dot(x[it].astype(BF16), bd(tinv[it])) for it in items}
        s += 1
    for ci, p in items:
        rs = slice(ci * chunk, (ci + 1) * chunk)
        cs = slice(2 * p * dk, 2 * (p + 1) * dk)
        tb = tinv[ci, p].astype(BF16)
        u_ref[rs, cs] = _dot(tb, _pair_blockdiag(vbn[ci, p][:, :dk], vbn[ci, p][:, dk:]))
        w_ref[rs, cs] = _dot(tb, _pair_blockdiag(kbe[ci, p][:, :dk], kbe[ci, p][:, dk:])).astype(BF16)


def _gdn_prep(q, k, v, bga, *, n_heads, dk, n_sub):
    t, vdim = q.shape
    chunk = GDN_CHUNK
    assert 2 * chunk == LANES and n_heads % 2 == 0 and 2 * n_heads <= LANES
    rows = chunk * n_sub
    qk_w = n_heads * chunk
    row = lambda width: pl.BlockSpec((rows, width), lambda i: (i, 0))
    kern = functools.partial(_gdn_prep_kernel, n_heads=n_heads, dk=dk, chunk=chunk, n_sub=n_sub)
    return pl.pallas_call(
        kern,
        grid=(t // rows,),
        in_specs=[row(vdim), row(vdim), row(vdim), row(LANES)],
        out_specs=[row(vdim), row(vdim), row(vdim), row(vdim), row(qk_w), row(LANES)],
        out_shape=[jax.ShapeDtypeStruct((t, vdim), F32)] + [jax.ShapeDtypeStruct((t, vdim), BF16)] * 3
                  + [jax.ShapeDtypeStruct((t, qk_w), BF16), jax.ShapeDtypeStruct((t, LANES), F32)],
        compiler_params=_cparams("arbitrary"),
        name="gdn_prep",
    )(q, k, v, bga)


def _gdn_scan_kernel(u_ref, w_ref, qe_ref, ke_ref, qk_ref, gc_ref, o_ref, sout_ref, s_ref,
                     *, n_heads, dk, chunk, n_sub, nbat):
    j = pl.program_id(1)

    @pl.when(j == 0)
    def _():
        s_ref[...] = jnp.zeros(s_ref.shape, s_ref.dtype)

    sls = [slice(h * dk, (h + 1) * dk) for h in range(n_heads)]
    items = [(bi, h) for bi in range(nbat) for h in range(n_heads)]

    def body(c, carry):
        r0 = pl.multiple_of(c * chunk, chunk)
        rows = pl.ds(r0, chunk)
        st = {it: s_ref[it[0], it[1]] for it in items}
        sb = {it: st[it].astype(BF16) for it in items}
        ws = {(bi, h): _dot(jnp.concatenate([w_ref[bi, rows, sls[h]], qe_ref[bi, rows, sls[h]]], axis=0),
                            sb[bi, h]) for bi, h in items}
        vb = {(bi, h): (u_ref[bi, rows, sls[h]] - ws[bi, h][:chunk]).astype(BF16) for bi, h in items}
        for bi in range(nbat):
            for p in range(n_heads // 2):
                ha, hb = 2 * p, 2 * p + 1
                intra = _dot(qk_ref[bi, rows, 2 * p * chunk:2 * (p + 1) * chunk],
                             _pair_blockdiag(vb[bi, ha], vb[bi, hb]))
                o_ref[bi, rows, 2 * p * dk:2 * (p + 1) * dk] = (
                    jnp.concatenate([ws[bi, ha][chunk:], ws[bi, hb][chunk:]], axis=1) + intra)
        for bi in range(nbat):
            eg = jnp.exp(gc_ref[bi, pl.ds(r0 + chunk - 1, 1), :])
            for h in range(n_heads):
                s_ref[bi, h] = (st[bi, h] * eg[:, n_heads + h:n_heads + h + 1]
                                + _dot_tn(ke_ref[bi, rows, sls[h]], vb[bi, h]))
        return carry

    lax.fori_loop(0, n_sub, body, 0)

    @pl.when(j == pl.num_programs(1) - 1)
    def _():
        sout_ref[...] = s_ref[...]


def _gdn_scan(u, w, qe, ke, qk, gc, *, batch, seq, n_heads, dk, rows, nbat):
    t, vdim = u.shape
    chunk = GDN_CHUNK
    seq3 = lambda a: a.reshape(batch, seq, a.shape[1])
    row = lambda width: pl.BlockSpec((nbat, rows, width), lambda b, j: (b, j, 0))
    kern = functools.partial(_gdn_scan_kernel, n_heads=n_heads, dk=dk, chunk=chunk, n_sub=rows // chunk,
                             nbat=nbat)
    o, s_new = pl.pallas_call(
        kern,
        grid=(batch // nbat, seq // rows),
        in_specs=[row(vdim)] * 4 + [row(qk.shape[1]), row(LANES)],
        out_specs=[row(vdim), pl.BlockSpec((nbat, n_heads, dk, dk), lambda b, j: (b, 0, 0, 0))],
        out_shape=[jax.ShapeDtypeStruct((batch, seq, vdim), F32),
                   jax.ShapeDtypeStruct((batch, n_heads, dk, dk), F32)],
        scratch_shapes=[pltpu.VMEM((nbat, n_heads, dk, dk), F32)],
        compiler_params=_cparams("arbitrary", "arbitrary"),
        name="gdn_scan",
    )(seq3(u), seq3(w), seq3(qe), seq3(ke), seq3(qk), seq3(gc))
    return o.reshape(t, vdim), s_new


def _gdn_step_kernel(q_ref, k_ref, v_ref, bga_ref, s_ref, o_ref, sout_ref, *, n_heads, dk, nb):
    s_ref = s_ref.at[0]
    pad = jnp.zeros((dk - n_heads, dk), F32)

    def body(b, carry):
        qt = jnp.concatenate([q_ref[b], pad], axis=0).T
        kt = jnp.concatenate([k_ref[b], pad], axis=0).T
        vb = v_ref[b]
        bg = bga_ref[pl.ds(b, 1), :]
        eg = jnp.exp(bg)
        rows = []
        for h in range(n_heads):
            st = s_ref[b, h] * eg[:, n_heads + h:n_heads + h + 1]
            kc = kt[:, h:h + 1]
            ks = jnp.sum(kc * st, axis=0, keepdims=True)
            delta = (vb[h:h + 1, :] - ks) * bg[:, h:h + 1]
            st = st + kc * delta
            sout_ref[b, h] = st
            rows.append(jnp.sum(qt[:, h:h + 1] * st, axis=0, keepdims=True))
        o_ref[b] = jnp.concatenate(rows, axis=0)
        return carry

    lax.fori_loop(0, nb, body, 0)


def _gdn_step(q, k, v, bga, states, layer, *, n_heads, dk, nb):
    b = q.shape[0]
    q3, k3, v3 = (a.reshape(b, n_heads, dk) for a in (q, k, v))
    vec = pl.BlockSpec((nb, n_heads, dk), lambda i: (i, 0, 0))
    st = pl.BlockSpec((nb, n_heads, dk, dk), lambda i: (i, 0, 0, 0))
    kern = functools.partial(_gdn_step_kernel, n_heads=n_heads, dk=dk, nb=nb)
    o, s_new = pl.pallas_call(
        kern,
        grid=(b // nb,),
        in_specs=[vec, vec, vec, pl.BlockSpec((nb, LANES), lambda i: (i, 0)),
                  pl.BlockSpec((1, nb, n_heads, dk, dk), lambda i: (layer, i, 0, 0, 0))],
        out_specs=[vec, st],
        out_shape=[jax.ShapeDtypeStruct((b, n_heads, dk), F32), jax.ShapeDtypeStruct(states.shape[1:], F32)],
        compiler_params=_cparams("arbitrary"),
        name="gdn_step",
    )(q3, k3, v3, bga, states)
    return o.reshape(b, n_heads * dk), s_new


def _gdn_out_kernel(o_ref, z_ref, x_ref, nw_ref, w_ref, y_ref, *, n_heads, dk):
    nw = nw_ref[...]
    parts = []
    for h in range(n_heads):
        sl = slice(h * dk, (h + 1) * dk)
        parts.append((_rms(o_ref[:, sl], nw, EPS) * _silu(z_ref[:, sl])).astype(BF16))
    y_ref[...] = x_ref[...] + _dot(jnp.concatenate(parts, axis=1), w_ref[...])


def _gdn_out(o, z, x, norm_w, w_out, *, tm, n_heads, dk):
    t, d = x.shape
    vdim = o.shape[1]
    row = lambda width: pl.BlockSpec((tm, width), lambda i: (i, 0))
    kern = functools.partial(_gdn_out_kernel, n_heads=n_heads, dk=dk)
    return pl.pallas_call(
        kern,
        grid=(t // tm,),
        in_specs=[row(vdim), row(vdim), row(d), _const_spec(norm_w.shape), _const_spec(w_out.shape)],
        out_specs=row(d),
        out_shape=jax.ShapeDtypeStruct((t, d), F32),
        compiler_params=_cparams("arbitrary"),
        name="gdn_out",
    )(o, z, x, norm_w, w_out)


def _ffn_tail(x, conv, up, cb_ref, wd_ref, gf_ref, o_ref, final_norm):
    act = (_silu(conv + cb_ref[...]) * up).astype(BF16)
    y = x + _dot(act, wd_ref[...])
    if final_norm:
        y = _rms(y, gf_ref[...], EPS)
    o_ref[...] = y


def _ffn_seq_kernel(x_ref, g_ref, wg_ref, wu_ref, cw_ref, cb_ref, wd_ref, gf_ref, o_ref, tail_ref,
                    buf_ref, act_ref, *, tm, tiles_per_seq, final_norm, cwid):
    x = x_ref[...]
    xn = _rms(x, g_ref[...], EPS).astype(BF16)
    first = (pl.program_id(0) % tiles_per_seq) == 0

    @pl.when(first)
    def _():
        buf_ref[0:HIST_ROWS, :] = jnp.zeros((HIST_ROWS, buf_ref.shape[1]), buf_ref.dtype)

    @pl.when(jnp.logical_not(first))
    def _():
        buf_ref[0:HIST_ROWS, :] = buf_ref[tm:tm + HIST_ROWS, :]

    cw = cw_ref[...]
    cb = cb_ref[...]
    width = cw.shape[0]
    n_chunks = wg_ref.shape[1] // cwid

    def mm(c):
        cs = slice(c * cwid, (c + 1) * cwid)
        return _dot(xn, wg_ref[:, cs]), _dot(xn, wu_ref[:, cs])

    nxt = mm(0)
    for c in range(n_chunks):
        cs = slice(c * cwid, (c + 1) * cwid)
        gate, up = nxt
        if c + 1 < n_chunks:
            nxt = mm(c + 1)
        buf_ref[HIST_ROWS:HIST_ROWS + tm, cs] = gate
        conv = cw[width - 1:width, cs] * gate
        for j in range(width - 1):
            conv = conv + cw[j:j + 1, cs] * buf_ref[pl.ds(HIST_ROWS - (width - 1) + j, tm), cs]
        act_ref[:, cs] = (_silu(conv + cb[:, cs]) * up).astype(BF16)
    tail_ref[0] = buf_ref[tm:tm + HIST_ROWS, :]
    y = x + _dot(act_ref[...], wd_ref[...])
    if final_norm:
        y = _rms(y, gf_ref[...], EPS)
    o_ref[...] = y


def _ffn_step_kernel(x_ref, g_ref, wg_ref, wu_ref, cw_ref, cb_ref, wd_ref, gf_ref, *rest,
                     n_hist, final_norm):
    hist_refs = rest[:n_hist]
    o_ref, raw_ref = rest[n_hist:]
    x = x_ref[...]
    xn = _rms(x, g_ref[...], EPS).astype(BF16)
    gate = _dot(xn, wg_ref[...])
    raw_ref[...] = gate
    cw = cw_ref[...]
    width = cw.shape[0]
    conv = cw[width - 1:width] * gate
    for j in range(width - 1):
        conv = conv + cw[j:j + 1] * hist_refs[j][...]
    _ffn_tail(x, conv, _dot(xn, wu_ref[...]), cb_ref, wd_ref, gf_ref, o_ref, final_norm)


def _ffn(x, g, wg, wu, conv_w, conv_b, wd, gf, *, tm, final_norm, seq=None, hist=None):
    t, d = x.shape
    d_ff = wg.shape[1]
    row = lambda width: pl.BlockSpec((tm, width), lambda i: (i, 0))
    common = [row(d), _const_spec((1, d)), _const_spec(wg.shape), _const_spec(wu.shape),
              _const_spec(conv_w.shape), _const_spec(conv_b.shape), _const_spec(wd.shape),
              _const_spec((1, d))]
    if hist is None:
        tps = seq // tm
        assert d_ff % (2 * LANES) == 0
        kern = functools.partial(_ffn_seq_kernel, tm=tm, tiles_per_seq=tps, final_norm=final_norm,
                                 cwid=2 * LANES)
        extra, extra_specs = [], []
        outs = [row(d), pl.BlockSpec((1, HIST_ROWS, d_ff), lambda i: (i // tps, 0, 0))]
        out_shapes = [jax.ShapeDtypeStruct((t, d), F32),
                      jax.ShapeDtypeStruct((t // seq, HIST_ROWS, d_ff), F32)]
        scratch = [pltpu.VMEM((tm + HIST_ROWS, d_ff), F32), pltpu.VMEM((tm, d_ff), BF16)]
    else:
        hist_arr, n_hist = hist
        kern = functools.partial(_ffn_step_kernel, n_hist=n_hist, final_norm=final_norm)
        extra = [hist_arr] * n_hist
        extra_specs = [pl.BlockSpec((tm, d_ff), functools.partial(lambda i, j: (i, j), j=j))
                       for j in range(n_hist)]
        outs = [row(d), row(d_ff)]
        out_shapes = [jax.ShapeDtypeStruct((t, d), F32), jax.ShapeDtypeStruct((t, d_ff), F32)]
        scratch = []
    return pl.pallas_call(
        kern,
        grid=(t // tm,),
        in_specs=common + extra_specs,
        out_specs=outs,
        out_shape=out_shapes,
        scratch_shapes=scratch,
        compiler_params=_cparams("arbitrary"),
        name="conv_ffn",
    )(x, g, wg, wu, conv_w, conv_b, wd, gf, *extra)


def _tile(n, pref):
    t = min(n, pref)
    while n % t:
        t //= 2
    return t


def kernel(x_prompt, x_sample, cache_k, cache_v, state_sconv, state_gdn_conv, state_gdn, state_ffn_conv, page_table, norm_mix, norm_ffn, norm_final, w_in_even, w_out_even, lambda_q1, lambda_k1, lambda_q2, lambda_k2, subln_w, sconv_w, w_in_odd, gdn_conv_w, a_log, dt_bias, gdn_norm_w, w_out_odd, w_up, ffn_conv_w, ffn_conv_b, w_down):
    batch, seq, d = x_prompt.shape
    dec_b = x_sample.shape[0]
    depth = norm_mix.shape[0]
    n_heads_a, e_a = cache_k.shape[3], cache_k.shape[4]
    n_att = n_heads_a * e_a
    rot = (e_a // 2) // 4
    w_b = state_sconv.shape[-1]
    sconv = sconv_w.shape[1]
    page = cache_k.shape[2]
    past_len = page_table.shape[1] * page
    n_heads_c, dk = state_gdn.shape[2], state_gdn.shape[3]
    kdim = n_heads_c * dk
    qkv_c = gdn_conv_w.shape[-1]
    gdn_conv = gdn_conv_w.shape[1]
    d_ff = ffn_conv_w.shape[-1]
    ffn_conv = ffn_conv_w.shape[1]
    t_p = batch * seq

    tm_p = _tile(seq, 256)
    tm_big = _tile(seq, 512)
    blk = _tile(seq, 512)
    scan_rows = _tile(seq, 256)

    xp = x_prompt.reshape(t_p, d)
    xs = x_sample.reshape(dec_b, d)
    rope_p = _rope_tables(jnp.arange(seq), e_a, rot)
    rope_s = _rope_tables(jnp.full((dec_b,), past_len), e_a, rot)
    gf = norm_final.reshape(1, d)

    kp, vp, ks, vs, scp, scs = [], [], [], [], [], []
    gcp, gcs, gsp, gss, fcp, fcs = [], [], [], [], [], []
    for l in range(depth):
        gm = norm_mix[l].reshape(1, d)
        if l % 2 == 0:
            e = l // 2
            lam_init = 0.8 - 0.6 * math.exp(-0.3 * l)
            w_in = w_in_even[e].astype(BF16)
            wvt = w_in_even[e][:, 2 * n_att:3 * n_att].T.astype(BF16)
            w_out = w_out_even[e].astype(BF16)
            lam_p = jnp.stack([lambda_q1[e], lambda_k1[e], lambda_q2[e], lambda_k2[e]])
            sw = subln_w[e].reshape(1, e_a)
            proj = functools.partial(_even_proj, n_heads=n_heads_a, e_a=e_a, w_b=w_b, half=rot // 2)
            q, k, v, kb, vt, bg, cx = proj(xp, gm, w_in, wvt, rope_p, tm=tm_p)
            att = _attn_prompt(q, kb, vt, lam_p, sw, batch=batch, seq=seq, n_heads=n_heads_a, e_a=e_a,
                               blk=blk, lam_init=lam_init)
            xp = _even_out(att, bg, cx, xp, sconv_w[e], w_out, tm=tm_big, seq=seq)
            kp.append(k.reshape(batch, seq, n_heads_a, e_a))
            vp.append(v.reshape(batch, seq, n_heads_a, e_a))
            cx3 = cx.reshape(batch, seq, w_b)
            if seq >= sconv - 1:
                scp.append(cx3[:, seq - (sconv - 1):])
            else:
                scp.append(jnp.concatenate([jnp.zeros((batch, sconv - 1 - seq, w_b), F32), cx3], axis=1))
            q, k, v, kb, vt, bg, cx = proj(xs, gm, w_in, wvt, rope_s, tm=dec_b)
            att = _attn_sample(q, k, v, cache_k, cache_v, e, page_table, lam_p, sw,
                               n_heads=n_heads_a, e_a=e_a, lam_init=lam_init)
            hist = [state_sconv[e, :, j] for j in range(sconv - 1)]
            xs = _even_out(att, bg, cx, xs, sconv_w[e], w_out, tm=dec_b, hist=hist)
            ks.append(k.reshape(dec_b, 1, n_heads_a, e_a))
            vs.append(v.reshape(dec_b, 1, n_heads_a, e_a))
            scs.append(jnp.concatenate([state_sconv[e, :, 1:], cx[:, None, :]], axis=1))
        else:
            o = l // 2
            w_in = w_in_odd[o]
            wqkv = w_in[:, :qkv_c].astype(BF16)
            wz = w_in[:, qkv_c:qkv_c + kdim].astype(BF16)
            wba = jnp.pad(w_in[:, qkv_c + kdim:], ((0, 0), (0, LANES - 2 * n_heads_c))).astype(BF16)
            w_out = w_out_odd[o].astype(BF16)
            lane_pad = lambda a: jnp.pad(a, (n_heads_c, LANES - 2 * n_heads_c)).reshape(1, LANES)
            al_pad, dtb_pad = lane_pad(a_log[o]), lane_pad(dt_bias[o])
            nw = gdn_norm_w[o].reshape(1, dk)
            proj = functools.partial(_gdn_proj, n_heads=n_heads_c, dk=dk)
            q, k, v, z, bga, tail = proj(xp, gm, wqkv, wz, wba, gdn_conv_w[o], al_pad, dtb_pad,
                                         tm=tm_p, seq=seq)
            u, w, qe, ke, qk, gc = _gdn_prep(q, k, v, bga, n_heads=n_heads_c, dk=dk,
                                             n_sub=_tile(seq // GDN_CHUNK, 4))
            og, s_new = _gdn_scan(u, w, qe, ke, qk, gc, batch=batch, seq=seq, n_heads=n_heads_c, dk=dk,
                                  rows=scan_rows, nbat=_tile(batch, 4))
            xp = _gdn_out(og, z, xp, nw, w_out, tm=tm_big, n_heads=n_heads_c, dk=dk)
            gcp.append(tail[:, HIST_ROWS - (gdn_conv - 1):])
            gsp.append(s_new)
            hist_arr = state_gdn_conv[o].reshape(dec_b, (gdn_conv - 1) * qkv_c)
            q, k, v, z, bga, raw = proj(xs, gm, wqkv, wz, wba, gdn_conv_w[o], al_pad, dtb_pad,
                                        tm=dec_b, hist=(hist_arr, gdn_conv - 1))
            og, s_new = _gdn_step(q, k, v, bga, state_gdn, o, n_heads=n_heads_c, dk=dk, nb=_tile(dec_b, 8))
            xs = _gdn_out(og, z, xs, nw, w_out, tm=dec_b, n_heads=n_heads_c, dk=dk)
            gcs.append(jnp.concatenate([state_gdn_conv[o, :, 1:], raw[:, None, :]], axis=1))
            gss.append(s_new)
        gn = norm_ffn[l].reshape(1, d)
        wg = w_up[l, :, :d_ff].astype(BF16)
        wu = w_up[l, :, d_ff:].astype(BF16)
        wd = w_down[l].astype(BF16)
        cb = ffn_conv_b[l].reshape(1, d_ff)
        last = l == depth - 1
        xp, tail = _ffn(xp, gn, wg, wu, ffn_conv_w[l], cb, wd, gf, tm=tm_big, final_norm=last, seq=seq)
        fcp.append(tail[:, HIST_ROWS - (ffn_conv - 1):])
        hist_arr = state_ffn_conv[l].reshape(dec_b, (ffn_conv - 1) * d_ff)
        xs, raw = _ffn(xs, gn, wg, wu, ffn_conv_w[l], cb, wd, gf, tm=dec_b, final_norm=last,
                       hist=(hist_arr, ffn_conv - 1))
        fcs.append(jnp.concatenate([state_ffn_conv[l, :, 1:], raw[:, None, :]], axis=1))
    return (xp.reshape(batch, seq, d), xs.reshape(dec_b, 1, d),
            jnp.stack(kp), jnp.stack(vp), jnp.stack(ks), jnp.stack(vs),
            jnp.stack(scp), jnp.stack(scs),
            jnp.stack(gcp), jnp.stack(gcs), jnp.stack(gsp), jnp.stack(gss),
            jnp.stack(fcp), jnp.stack(fcs))
```

```python
import functools
import math

import jax
import jax.numpy as jnp
from jax import lax
from jax.experimental import pallas as pl
from jax.experimental.pallas import tpu as pltpu

F32 = jnp.float32
BF16 = jnp.bfloat16

EPS = 1e-6
SUBLN_EPS = 1e-5
ROPE_THETA = 500000.0
GDN_CHUNK = 64
V7X_VMEM_LIMIT = 56 * 1024 * 1024
LANES = 128
HIST_ROWS = 8
BF16_ROWS = 16
LOG2E = math.log2(math.e)
VT_PAD = 16
ATTN_SUB = 2


def _cparams(*sem):
    return pltpu.CompilerParams(dimension_semantics=sem, vmem_limit_bytes=V7X_VMEM_LIMIT)


def _const_spec(shape):
    nd = len(shape)
    return pl.BlockSpec(shape, lambda i: (0,) * nd, pipeline_mode=pl.Buffered(1))


def _rms(x, g, eps):
    return x * lax.rsqrt(jnp.mean(x * x, axis=-1, keepdims=True) + eps) * g


def _sigmoid(x):
    return 0.5 * jnp.tanh(0.5 * x) + 0.5


def _silu(x):
    h = 0.5 * x
    return h * jnp.tanh(h) + h


def _softplus(x):
    return jnp.maximum(x, 0.0) + jnp.log1p(jnp.exp(-jnp.abs(x)))


def _dot(a, b):
    return jnp.dot(a, b, preferred_element_type=F32)


def _dot_nt(a, b):
    return lax.dot_general(a, b, (((1,), (1,)), ((), ())), preferred_element_type=F32)


def _dot_tn(a, b):
    return lax.dot_general(a, b, (((0,), (0,)), ((), ())), preferred_element_type=F32)


def _split3(a):
    hi = a.astype(BF16)
    r = a - hi.astype(F32)
    mid = r.astype(BF16)
    lo = (r - mid.astype(F32)).astype(BF16)
    return hi, mid, lo


def _even_proj_kernel(x_ref, g_ref, w_ref, wvt_ref, c_ref, s1_ref, s2_ref,
                      q_ref, k_ref, v_ref, kb_ref, vt_ref, bg_ref, cx_ref,
                      *, tm, n_heads, e_a, w_b, half, qscale):
    xn = _rms(x_ref[...], g_ref[...], EPS).astype(BF16)
    n = n_heads * e_a
    c, s1, s2 = c_ref[...], s1_ref[...], s2_ref[...]

    def rope(t):
        return t * c + pltpu.roll(t, e_a - half, 1) * s1 + pltpu.roll(t, half, 1) * s2

    hqk = _dot(xn, w_ref[:, :2 * n])
    for h in range(n_heads):
        sl = slice(h * e_a, (h + 1) * e_a)
        q_ref[:, sl] = (rope(hqk[:, sl]) * qscale).astype(BF16)
        kr = rope(hqk[:, n + h * e_a:n + (h + 1) * e_a])
        k_ref[pl.ds(h, tm, stride=n_heads), :] = kr
        kb_ref[:, sl] = kr.astype(BF16)
    hv = _dot(xn, w_ref[:, 2 * n:3 * n])
    for h in range(n_heads):
        v_ref[pl.ds(h, tm, stride=n_heads), :] = hv[:, h * e_a:(h + 1) * e_a]
    hvt = _dot_nt(wvt_ref[...], xn).astype(BF16)
    ones_row = jnp.where(lax.broadcasted_iota(jnp.int32, (VT_PAD, tm), 0) == 0, 1.0, 0.0).astype(BF16)
    for h in range(n_heads):
        r0 = h * (e_a + VT_PAD)
        vt_ref[r0:r0 + e_a, :] = hvt[h * e_a:(h + 1) * e_a, :]
        vt_ref[r0 + e_a:r0 + e_a + VT_PAD, :] = ones_row
    o = 3 * n
    bg_ref[...] = _dot(xn, w_ref[:, o:o + w_b])
    hc = _dot(xn, w_ref[:, o + w_b:o + 3 * w_b])
    cx_ref[...] = hc[:, :w_b] * hc[:, w_b:]


def _even_proj(x, g, w, wvt, rope_tabs, *, tm, n_heads, e_a, w_b, half):
    t, d = x.shape
    n = n_heads * e_a
    c, s1, s2 = rope_tabs
    row = lambda width: pl.BlockSpec((tm, width), lambda i: (i, 0))
    tab_tiles = c.shape[0] // tm
    n_vt = n_heads * (e_a + VT_PAD)
    tab = pl.BlockSpec((tm, e_a), lambda i: (i % tab_tiles, 0))
    tok = pl.BlockSpec((tm * n_heads, e_a), lambda i: (i, 0))
    kern = functools.partial(_even_proj_kernel, tm=tm, n_heads=n_heads, e_a=e_a, w_b=w_b, half=half,
                             qscale=(e_a // 2) ** -0.5 * LOG2E)
    return pl.pallas_call(
        kern,
        grid=(t // tm,),
        in_specs=[row(d), _const_spec((1, d)), _const_spec(w.shape), _const_spec(wvt.shape),
                  tab, tab, tab],
        out_specs=[row(n), tok, tok, row(n), pl.BlockSpec((n_vt, tm), lambda i: (0, i)),
                   row(w_b), row(w_b)],
        out_shape=[jax.ShapeDtypeStruct((t, n), BF16), jax.ShapeDtypeStruct((t * n_heads, e_a), F32),
                   jax.ShapeDtypeStruct((t * n_heads, e_a), F32), jax.ShapeDtypeStruct((t, n), BF16),
                   jax.ShapeDtypeStruct((n_vt, t), BF16), jax.ShapeDtypeStruct((t, w_b), F32),
                   jax.ShapeDtypeStruct((t, w_b), F32)],
        compiler_params=_cparams("arbitrary"),
        name="even_proj",
    )(x, g, w, wvt, c, s1, s2)


def _rope_tables(pos, e_a, rot, dtype=F32):
    half = rot // 2
    dh = e_a // 2
    inv = ROPE_THETA ** (-jnp.arange(half, dtype=F32) * 2.0 / rot)
    ang = pos.astype(F32)[:, None] * inv[None, :]
    cos, sin = jnp.cos(ang).astype(dtype), jnp.sin(ang).astype(dtype)
    t = pos.shape[0]
    ones = jnp.ones((t, dh - rot), dtype)
    zeros_h = jnp.zeros((t, half), dtype)
    zeros_r = jnp.zeros((t, dh - rot), dtype)
    c = jnp.concatenate([cos, cos, ones], axis=1)
    s1 = jnp.concatenate([-sin, zeros_h, zeros_r], axis=1)
    s2 = jnp.concatenate([zeros_h, sin, zeros_r], axis=1)
    rep = lambda a: jnp.concatenate([a, a], axis=1)
    return rep(c), rep(s1), rep(s2)


def _lambda(lp_ref, lam_init):
    lp = lp_ref[...]
    a = jnp.sum(lp[0:1] * lp[1:2], axis=-1, keepdims=True)
    b = jnp.sum(lp[2:3] * lp[3:4], axis=-1, keepdims=True)
    return jnp.exp(a) - jnp.exp(b) + lam_init


def _attn_prompt_kernel(q_ref, k_ref, vt_ref, lp_ref, swc_ref, o_ref, *, blk, n_sub, dh, lam_init):
    i = pl.program_id(2)
    qt = q_ref[...].astype(F32).T
    feat = lax.broadcasted_iota(jnp.int32, qt.shape, 0)
    qs = jnp.concatenate([jnp.where(feat < dh, qt, 0.0), jnp.where(feat >= dh, qt, 0.0)],
                         axis=1).astype(BF16)
    e_a = qt.shape[0]
    sub = blk // n_sub

    def step(off, carry, diag):
        def scores(t):
            s = _dot(k_ref[pl.ds(off + t * sub, sub), :], qs)
            if diag:
                key = lax.broadcasted_iota(jnp.int32, s.shape, 0) + t * sub
                qry = lax.broadcasted_iota(jnp.int32, s.shape, 1)
                qry = jnp.where(qry >= blk, qry - blk, qry)
                s = jnp.where(key <= qry, s, jnp.finfo(F32).min)
            return s

        ss = [scores(t) for t in range(n_sub)]
        m, acc = carry
        for t in range(n_sub):
            m_new = jnp.maximum(m, jnp.max(ss[t], axis=0, keepdims=True))
            p = jnp.exp2(ss[t] - m_new)
            acc = jnp.exp2(m - m_new) * acc + _dot(vt_ref[:, pl.ds(off + t * sub, sub)], p.astype(BF16))
            m = m_new
        return m, acc

    def body(j, carry):
        return step(pl.multiple_of(j * blk, blk), carry, False)

    init = (jnp.full((1, 2 * blk), -1e30, F32), jnp.zeros((vt_ref.shape[0], 2 * blk), F32))
    carry = lax.fori_loop(0, i, body, init)
    _, acc = step(pl.multiple_of(i * blk, blk), carry, True)
    o = acc[:e_a] / acc[e_a:e_a + 1]
    lam = _lambda(lp_ref, lam_init)
    att = o[:, :blk] - lam * o[:, blk:]
    y = att * lax.rsqrt(jnp.mean(att * att, axis=0, keepdims=True) + SUBLN_EPS) * swc_ref[...]
    o_ref[...] = (y * (1.0 - lam_init)).T.astype(o_ref.dtype)


def _attn_prompt(q, kb, vt, lam_p, subln_w, *, batch, seq, n_heads, e_a, blk, lam_init):
    t = q.shape[0]
    nq = seq // blk
    swc = subln_w.reshape(e_a, 1)
    n_sub = ATTN_SUB if blk % (ATTN_SUB * LANES) == 0 else 1
    kern = functools.partial(_attn_prompt_kernel, blk=blk, n_sub=n_sub, dh=e_a // 2, lam_init=lam_init)
    return pl.pallas_call(
        kern,
        grid=(batch, n_heads, nq),
        in_specs=[pl.BlockSpec((blk, e_a), lambda b, h, i: (b * nq + i, h)),
                  pl.BlockSpec((seq, e_a), lambda b, h, i: (b, h)),
                  pl.BlockSpec((e_a + VT_PAD, seq), lambda b, h, i: (h, b)),
                  pl.BlockSpec(lam_p.shape, lambda b, h, i: (0, 0)),
                  pl.BlockSpec(swc.shape, lambda b, h, i: (0, 0))],
        out_specs=pl.BlockSpec((blk, e_a), lambda b, h, i: (b * nq + i, h)),
        out_shape=jax.ShapeDtypeStruct((t, n_heads * e_a), BF16),
        compiler_params=_cparams("arbitrary", "arbitrary", "arbitrary"),
        name="attn_prompt",
    )(q, kb, vt, lam_p, swc)


def _attn_sample_kernel(pt_ref, q_ref, kn_ref, vn_ref, lp_ref, sw_ref, *rest,
                        n_pages, n_heads, e_a, lam_init):
    del pt_ref
    k_refs = rest[:n_pages]
    v_refs = rest[n_pages:2 * n_pages]
    o_ref = rest[2 * n_pages]
    dh = e_a // 2
    nm = max(2 * n_heads, BF16_ROWS)
    rows_pp = k_refs[0].shape[1]
    rowi = lax.broadcasted_iota(jnp.int32, (nm, e_a), 0)
    lane = lax.broadcasted_iota(jnp.int32, (nm, e_a), 1)

    def per_head_rows(ref):
        t = ref[0]
        out = jnp.zeros((nm, e_a), F32)
        for h in range(n_heads):
            out = jnp.where((rowi >> 1) == h, jnp.broadcast_to(t[h:h + 1].astype(F32), (nm, e_a)), out)
        return out

    qm_f = jnp.where((lane // dh) == (rowi & 1), per_head_rows(q_ref), 0.0)
    qm = qm_f.astype(BF16)
    colh = lax.broadcasted_iota(jnp.int32, (nm, rows_pp), 1) % n_heads
    valid = colh == (lax.broadcasted_iota(jnp.int32, (nm, rows_pp), 0) >> 1)
    ss = [jnp.where(valid, _dot_nt(qm, k_refs[p][0].astype(BF16)), jnp.finfo(F32).min)
          for p in range(n_pages)]
    kn = per_head_rows(kn_ref).astype(BF16).astype(F32)
    vn = per_head_rows(vn_ref).astype(BF16).astype(F32)
    s_new = jnp.sum(qm.astype(F32) * kn, axis=-1, keepdims=True)
    m = s_new
    for s in ss:
        m = jnp.maximum(m, jnp.max(s, axis=-1, keepdims=True))
    p_new = jnp.exp2(s_new - m)
    l = p_new
    acc = p_new.astype(BF16).astype(F32) * vn
    for p in range(n_pages):
        pr = jnp.exp2(ss[p] - m)
        l = l + jnp.sum(pr, axis=-1, keepdims=True)
        acc = acc + _dot(pr.astype(BF16), v_refs[p][0].astype(BF16))
    o = acc / l
    lam = _lambda(lp_ref, lam_init)
    sw = sw_ref[...]
    for h in range(n_heads):
        att = o[2 * h:2 * h + 1] - lam * o[2 * h + 1:2 * h + 2]
        o_ref[0, h:h + 1, :] = (_rms(att, sw, SUBLN_EPS) * (1.0 - lam_init)).astype(o_ref.dtype)


def _attn_sample(q, k_new, v_new, cache_k, cache_v, layer, page_table, lam_p, subln_w,
                 *, n_heads, e_a, lam_init):
    b, n = q.shape
    n_pages = page_table.shape[1]
    n_phys, page = cache_k.shape[1], cache_k.shape[2]
    ck = cache_k.reshape(cache_k.shape[0] * n_phys, page * n_heads, e_a)
    cv = cache_v.reshape(cache_v.shape[0] * n_phys, page * n_heads, e_a)
    base = layer * n_phys
    tok = pl.BlockSpec((1, n_heads, e_a), lambda i, pt: (i, 0, 0))
    page_specs = [pl.BlockSpec((1, page * n_heads, e_a), functools.partial(
        lambda i, pt, p: (base + pt[i, p], 0, 0), p=p)) for p in range(n_pages)]
    kern = functools.partial(_attn_sample_kernel, n_pages=n_pages, n_heads=n_heads, e_a=e_a,
                             lam_init=lam_init)
    grid_spec = pltpu.PrefetchScalarGridSpec(
        num_scalar_prefetch=1,
        grid=(b,),
        in_specs=[tok, tok, tok,
                  pl.BlockSpec(lam_p.shape, lambda i, pt: (0, 0)),
                  pl.BlockSpec(subln_w.shape, lambda i, pt: (0, 0))] + page_specs + page_specs,
        out_specs=tok,
    )
    tok3 = lambda a: a.reshape(b, n_heads, e_a)
    out = pl.pallas_call(
        kern,
        grid_spec=grid_spec,
        out_shape=jax.ShapeDtypeStruct((b, n_heads, e_a), F32),
        compiler_params=_cparams("arbitrary"),
        name="attn_sample",
    )(page_table, tok3(q).astype(F32), tok3(k_new), tok3(v_new), lam_p, subln_w,
      *([ck] * n_pages), *([cv] * n_pages))
    return out.reshape(b, n)


def _seq_hist(buf_ref, cur, tiles_per_seq, tm):
    first = (pl.program_id(0) % tiles_per_seq) == 0

    @pl.when(first)
    def _():
        buf_ref[0:HIST_ROWS, :] = jnp.zeros((HIST_ROWS, buf_ref.shape[1]), buf_ref.dtype)

    @pl.when(jnp.logical_not(first))
    def _():
        buf_ref[0:HIST_ROWS, :] = buf_ref[tm:tm + HIST_ROWS, :]

    buf_ref[HIST_ROWS:HIST_ROWS + tm, :] = cur


def _even_out_seq_kernel(att_ref, bg_ref, cx_ref, x_ref, cw_ref, w_ref, o_ref, buf_ref,
                         *, tm, tiles_per_seq, n_att):
    cx = cx_ref[...]
    _seq_hist(buf_ref, cx, tiles_per_seq, tm)
    cw = cw_ref[...]
    width = cw.shape[0]
    conv = cw[width - 1:width] * cx
    for j in range(width - 1):
        conv = conv + cw[j:j + 1] * buf_ref[pl.ds(HIST_ROWS - (width - 1) + j, tm), :]
    yb = (bg_ref[...] * conv).astype(BF16)
    o_ref[...] = (x_ref[...] + _dot(att_ref[...].astype(BF16), w_ref[:n_att, :])
                  + _dot(yb, w_ref[n_att:, :]))


def _even_out_step_kernel(att_ref, bg_ref, cx_ref, x_ref, cw_ref, w_ref, *rest, n_att):
    hist_refs, o_ref = rest[:-1], rest[-1]
    cw = cw_ref[...]
    width = cw.shape[0]
    conv = cw[width - 1:width] * cx_ref[...]
    for j in range(width - 1):
        conv = conv + cw[j:j + 1] * hist_refs[j][...]
    yb = (bg_ref[...] * conv).astype(BF16)
    o_ref[...] = (x_ref[...] + _dot(att_ref[...].astype(BF16), w_ref[:n_att, :])
                  + _dot(yb, w_ref[n_att:, :]))


def _even_out(att, bg, cx, x, conv_w, w_out, *, tm, seq=None, hist=None):
    t, d = x.shape
    n_att, w_b = att.shape[1], bg.shape[1]
    row = lambda width: pl.BlockSpec((tm, width), lambda i: (i, 0))
    common = [row(n_att), row(w_b), row(w_b), row(d), _const_spec(conv_w.shape), _const_spec(w_out.shape)]
    if hist is None:
        kern = functools.partial(_even_out_seq_kernel, tm=tm, tiles_per_seq=seq // tm, n_att=n_att)
        extra, extra_specs = [], []
        scratch = [pltpu.VMEM((tm + HIST_ROWS, w_b), F32)]
    else:
        kern = functools.partial(_even_out_step_kernel, n_att=n_att)
        extra, extra_specs = list(hist), [row(w_b)] * len(hist)
        scratch = []
    return pl.pallas_call(
        kern,
        grid=(t // tm,),
        in_specs=common + extra_specs,
        out_specs=row(d),
        out_shape=jax.ShapeDtypeStruct((t, d), F32),
        scratch_shapes=scratch,
        compiler_params=_cparams("arbitrary"),
        name="even_out",
    )(att, bg, cx, x, conv_w, w_out, *extra)


def _gdn_gates(xn, wz_ref, wba_ref, al_ref, dtb_ref, z_ref, bga_ref, *, n_heads):
    z_ref[...] = _dot(xn, wz_ref[...])
    ba = _dot(xn, wba_ref[...])
    beta = _sigmoid(ba)
    g = -jnp.exp(al_ref[...]) * _softplus(ba + dtb_ref[...])
    lane = lax.broadcasted_iota(jnp.int32, ba.shape, 1)
    bga_ref[...] = jnp.where(lane < n_heads, beta, g)


def _gdn_qkv_chunk(conv, c0, q_ref, k_ref, v_ref, *, n_heads, dk):
    kdim = n_heads * dk
    act = _silu(conv)
    for i in range(conv.shape[1] // dk):
        t = act[:, i * dk:(i + 1) * dk]
        ch = c0 + i * dk
        if ch < 2 * kdim:
            t = t * lax.rsqrt(jnp.sum(t * t, axis=-1, keepdims=True) + EPS)
        if ch < kdim:
            q_ref[:, ch:ch + dk] = t * (dk ** -0.5)
        elif ch < 2 * kdim:
            k_ref[:, ch - kdim:ch - kdim + dk] = t
        else:
            v_ref[:, ch - 2 * kdim:ch - 2 * kdim + dk] = t


def _gdn_proj_seq_kernel(x_ref, g_ref, wqkv_ref, wz_ref, wba_ref, cw_ref, al_ref, dtb_ref,
                         q_ref, k_ref, v_ref, z_ref, bga_ref, tail_ref, buf_ref,
                         *, tm, tiles_per_seq, n_heads, dk, cwid):
    xn = _rms(x_ref[...], g_ref[...], EPS).astype(BF16)
    first = (pl.program_id(0) % tiles_per_seq) == 0

    @pl.when(first)
    def _():
        buf_ref[0:HIST_ROWS, :] = jnp.zeros((HIST_ROWS, buf_ref.shape[1]), buf_ref.dtype)

    @pl.when(jnp.logical_not(first))
    def _():
        buf_ref[0:HIST_ROWS, :] = buf_ref[tm:tm + HIST_ROWS, :]

    cw = cw_ref[...]
    width = cw.shape[0]
    n_chunks = wqkv_ref.shape[1] // cwid
    mm = lambda c: _dot(xn, wqkv_ref[:, c * cwid:(c + 1) * cwid])
    h_next = mm(0)
    for c in range(n_chunks):
        cs = slice(c * cwid, (c + 1) * cwid)
        hm = h_next
        if c + 1 < n_chunks:
            h_next = mm(c + 1)
        buf_ref[HIST_ROWS:HIST_ROWS + tm, cs] = hm
        conv = cw[width - 1:width, cs] * hm
        for j in range(width - 1):
            conv = conv + cw[j:j + 1, cs] * buf_ref[pl.ds(HIST_ROWS - (width - 1) + j, tm), cs]
        _gdn_qkv_chunk(conv, c * cwid, q_ref, k_ref, v_ref, n_heads=n_heads, dk=dk)
    tail_ref[0] = buf_ref[tm:tm + HIST_ROWS, :]
    _gdn_gates(xn, wz_ref, wba_ref, al_ref, dtb_ref, z_ref, bga_ref, n_heads=n_heads)


def _gdn_proj_step_kernel(x_ref, g_ref, wqkv_ref, wz_ref, wba_ref, cw_ref, al_ref, dtb_ref, *rest,
                          n_heads, dk, n_hist):
    hist_refs = rest[:n_hist]
    q_ref, k_ref, v_ref, z_ref, bga_ref, raw_ref = rest[n_hist:]
    xn = _rms(x_ref[...], g_ref[...], EPS).astype(BF16)
    hm = _dot(xn, wqkv_ref[...])
    raw_ref[...] = hm
    cw = cw_ref[...]
    width = cw.shape[0]
    conv = cw[width - 1:width] * hm
    for j in range(width - 1):
        conv = conv + cw[j:j + 1] * hist_refs[j][...]
    _gdn_qkv_chunk(conv, 0, q_ref, k_ref, v_ref, n_heads=n_heads, dk=dk)
    _gdn_gates(xn, wz_ref, wba_ref, al_ref, dtb_ref, z_ref, bga_ref, n_heads=n_heads)


def _gdn_proj(x, g, wqkv, wz, wba, conv_w, al_pad, dtb_pad, *, tm, n_heads, dk, seq=None, hist=None):
    t, d = x.shape
    qkv_c = wqkv.shape[1]
    vdim = wz.shape[1]
    row = lambda width: pl.BlockSpec((tm, width), lambda i: (i, 0))
    common = [row(d), _const_spec((1, d)), _const_spec(wqkv.shape), _const_spec(wz.shape),
              _const_spec(wba.shape), _const_spec(conv_w.shape), _const_spec(al_pad.shape),
              _const_spec(dtb_pad.shape)]
    outs = [row(vdim), row(vdim), row(vdim), row(vdim), row(LANES)]
    out_shapes = [jax.ShapeDtypeStruct((t, vdim), F32)] * 4 + [jax.ShapeDtypeStruct((t, LANES), F32)]
    if hist is None:
        tps = seq // tm
        kern = functools.partial(_gdn_proj_seq_kernel, tm=tm, tiles_per_seq=tps, n_heads=n_heads, dk=dk,
                                 cwid=2 * dk)
        extra, extra_specs = [], []
        outs.append(pl.BlockSpec((1, HIST_ROWS, qkv_c), lambda i: (i // tps, 0, 0)))
        out_shapes.append(jax.ShapeDtypeStruct((t // seq, HIST_ROWS, qkv_c), F32))
        scratch = [pltpu.VMEM((tm + HIST_ROWS, qkv_c), F32)]
    else:
        hist_arr, n_hist = hist
        kern = functools.partial(_gdn_proj_step_kernel, n_heads=n_heads, dk=dk, n_hist=n_hist)
        extra = [hist_arr] * n_hist
        extra_specs = [pl.BlockSpec((tm, qkv_c), functools.partial(lambda i, j: (i, j), j=j))
                       for j in range(n_hist)]
        outs.append(row(qkv_c))
        out_shapes.append(jax.ShapeDtypeStruct((t, qkv_c), F32))
        scratch = []
    return pl.pallas_call(
        kern,
        grid=(t // tm,),
        in_specs=common + extra_specs,
        out_specs=outs,
        out_shape=out_shapes,
        scratch_shapes=scratch,
        compiler_params=_cparams("arbitrary"),
        name="gdn_proj",
    )(x, g, wqkv, wz, wba, conv_w, al_pad, dtb_pad, *extra)


def _pair_blockdiag(a, b):
    z = jnp.zeros_like(a)
    return jnp.concatenate([jnp.concatenate([a, z], axis=1), jnp.concatenate([z, b], axis=1)], axis=0)


def _gdn_prep_kernel(q_ref, k_ref, v_ref, bga_ref, u_ref, w_ref, qe_ref, ke_ref, qk_ref, gc_ref,
                     *, n_heads, dk, chunk, n_sub):
    n_pairs = n_heads // 2
    rows_blk = chunk * n_sub
    bga = bga_ref[...]
    r = lax.broadcasted_iota(jnp.int32, (rows_blk, rows_blk), 0)
    c = lax.broadcasted_iota(jnp.int32, (rows_blk, rows_blk), 1)
    sh = chunk.bit_length() - 1
    tril = jnp.where(jnp.logical_and(r >= c, (r >> sh) == (c >> sh)), 1.0, 0.0).astype(BF16)
    b3 = _split3(bga)
    gc = _dot(tril, b3[0]) + (_dot(tril, b3[1]) + _dot(tril, b3[2]))
    gc_ref[...] = gc
    lane = lax.broadcasted_iota(jnp.int32, (rows_blk, LANES), 1)
    is_g = jnp.logical_and(lane >= n_heads, lane < 2 * n_heads)
    odd = ((lane - n_heads) & 1) == 1
    gce = jnp.where(jnp.logical_and(is_g, jnp.logical_not(odd)), gc, 0.0)
    gco = jnp.where(jnp.logical_and(is_g, odd), gc, 0.0)
    ar = lax.broadcasted_iota(jnp.int32, (BF16_ROWS, LANES), 0)
    al = lax.broadcasted_iota(jnp.int32, (BF16_ROWS, LANES), 1)
    asel = jnp.where(jnp.logical_and(jnp.logical_and(al >= n_heads, ar < n_pairs),
                                     ((al - n_heads) >> 1) == ar), 1.0, 0.0).astype(BF16)
    prow = lax.broadcasted_iota(jnp.int32, (chunk, 2 * chunk), 0)
    plane = lax.broadcasted_iota(jnp.int32, (chunk, 2 * chunk), 1)
    pcol = plane & (chunk - 1)
    lo_half = plane < chunk
    incl = prow >= pcol
    strict = prow > pcol

    def bd(m):
        return jnp.concatenate([jnp.where(lo_half, m, 0.0), jnp.where(lo_half, 0.0, m)], axis=0).astype(BF16)

    items = [(ci, p) for ci in range(n_sub) for p in range(n_pairs)]
    lmat, tinv, vbn, kbe = {}, {}, {}, {}
    for ci, p in items:
        rs = slice(ci * chunk, (ci + 1) * chunk)
        cs = slice(2 * p * dk, 2 * (p + 1) * dk)
        g3 = _split3(jnp.concatenate([gce[rs], gco[rs]], axis=0))
        grow = _dot_nt(asel, g3[0]) + (_dot_nt(asel, g3[1]) + _dot_nt(asel, g3[2]))
        k2, q2, v2 = k_ref[rs, cs], q_ref[rs, cs], v_ref[rs, cs]
        ha, hb = 2 * p, 2 * p + 1

        def nat(a, b, rows=chunk):
            return jnp.concatenate([jnp.broadcast_to(a, (rows, dk)), jnp.broadcast_to(b, (rows, dk))], axis=1)

        gca, gcb = gc[rs, n_heads + ha:n_heads + ha + 1], gc[rs, n_heads + hb:n_heads + hb + 1]
        bnat = nat(bga[rs, ha:ha + 1], bga[rs, hb:hb + 1])
        gnat = nat(gca, gcb)
        glast = nat(gca[chunk - 1:chunk], gcb[chunk - 1:chunk], rows=1)
        egn = jnp.exp(gnat)
        kb2 = k2 * bnat
        k2b = k2.astype(BF16)
        rhs_t = _pair_blockdiag(k2b[:, :dk], k2b[:, dk:])
        decay = jnp.where(incl, jnp.exp(jnp.where(incl, jnp.where(lo_half, gca, gcb) - grow[p:p + 1], 0.0)), 0.0)
        lm = jnp.where(strict, _dot_nt(kb2.astype(BF16), rhs_t) * decay, 0.0)
        qk_ref[rs, 2 * p * chunk:2 * (p + 1) * chunk] = jnp.where(
            incl, _dot_nt(q2.astype(BF16), rhs_t) * decay, 0.0).astype(BF16)
        qe_ref[rs, cs] = (q2 * egn).astype(BF16)
        ke_ref[rs, cs] = (k2 * jnp.exp(glast - gnat)).astype(BF16)
        lmat[ci, p] = lm
        tinv[ci, p] = jnp.where(prow == pcol, 1.0, 0.0) - jnp.where((prow >> 1) == (pcol >> 1), lm, 0.0)
        vbn[ci, p] = (v2 * bnat).astype(BF16)
        kbe[ci, p] = (kb2 * egn).astype(BF16)
    s = 1
    while 2 * (1 << s) <= chunk:
        joins = jnp.logical_and((prow >> (s + 1)) == (pcol >> (s + 1)), (prow >> s) != (pcol >> s))
        x = {it: _dot(tinv[it].astype(BF16), bd(jnp.where(joins, lmat[it], 0.0))) for it in items}
        tinv = {it: tinv[it] - _dot(x[it].astype(BF16), bd(tinv[it])) for it in items}
        s += 1
    for ci, p in items:
        rs = slice(ci * chunk, (ci + 1) * chunk)
        cs = slice(2 * p * dk, 2 * (p + 1) * dk)
        tb = tinv[ci, p].astype(BF16)
        u_ref[rs, cs] = _dot(tb, _pair_blockdiag(vbn[ci, p][:, :dk], vbn[ci, p][:, dk:]))
        w_ref[rs, cs] = _dot(tb, _pair_blockdiag(kbe[ci, p][:, :dk], kbe[ci, p][:, dk:])).astype(BF16)


def _gdn_prep(q, k, v, bga, *, n_heads, dk, n_sub):
    t, vdim = q.shape
    chunk = GDN_CHUNK
    assert 2 * chunk == LANES and n_heads % 2 == 0 and 2 * n_heads <= LANES
    rows = chunk * n_sub
    qk_w = n_heads * chunk
    row = lambda width: pl.BlockSpec((rows, width), lambda i: (i, 0))
    kern = functools.partial(_gdn_prep_kernel, n_heads=n_heads, dk=dk, chunk=chunk, n_sub=n_sub)
    return pl.pallas_call(
        kern,
        grid=(t // rows,),
        in_specs=[row(vdim), row(vdim), row(vdim), row(LANES)],
        out_specs=[row(vdim), row(vdim), row(vdim), row(vdim), row(qk_w), row(LANES)],
        out_shape=[jax.ShapeDtypeStruct((t, vdim), F32)] + [jax.ShapeDtypeStruct((t, vdim), BF16)] * 3
                  + [jax.ShapeDtypeStruct((t, qk_w), BF16), jax.ShapeDtypeStruct((t, LANES), F32)],
        compiler_params=_cparams("arbitrary"),
        name="gdn_prep",
    )(q, k, v, bga)


def _gdn_scan_kernel(u_ref, w_ref, qe_ref, ke_ref, qk_ref, gc_ref, o_ref, sout_ref, s_ref,
                     *, n_heads, dk, chunk, n_sub, nbat):
    j = pl.program_id(1)

    @pl.when(j == 0)
    def _():
        s_ref[...] = jnp.zeros(s_ref.shape, s_ref.dtype)

    sls = [slice(h * dk, (h + 1) * dk) for h in range(n_heads)]
    items = [(bi, h) for bi in range(nbat) for h in range(n_heads)]

    def body(c, carry):
        r0 = pl.multiple_of(c * chunk, chunk)
        rows = pl.ds(r0, chunk)
        st = {it: s_ref[it[0], it[1]] for it in items}
        sb = {it: st[it].astype(BF16) for it in items}
        ws = {(bi, h): _dot(jnp.concatenate([w_ref[bi, rows, sls[h]], qe_ref[bi, rows, sls[h]]], axis=0),
                            sb[bi, h]) for bi, h in items}
        vb = {(bi, h): (u_ref[bi, rows, sls[h]] - ws[bi, h][:chunk]).astype(BF16) for bi, h in items}
        for bi in range(nbat):
            for p in range(n_heads // 2):
                ha, hb = 2 * p, 2 * p + 1
                intra = _dot(qk_ref[bi, rows, 2 * p * chunk:2 * (p + 1) * chunk],
                             _pair_blockdiag(vb[bi, ha], vb[bi, hb]))
                o_ref[bi, rows, 2 * p * dk:2 * (p + 1) * dk] = (
                    jnp.concatenate([ws[bi, ha][chunk:], ws[bi, hb][chunk:]], axis=1) + intra)
        for bi in range(nbat):
            eg = jnp.exp(gc_ref[bi, pl.ds(r0 + chunk - 1, 1), :])
            for h in range(n_heads):
                s_ref[bi, h] = (st[bi, h] * eg[:, n_heads + h:n_heads + h + 1]
                                + _dot_tn(ke_ref[bi, rows, sls[h]], vb[bi, h]))
        return carry

    lax.fori_loop(0, n_sub, body, 0)

    @pl.when(j == pl.num_programs(1) - 1)
    def _():
        sout_ref[...] = s_ref[...]


def _gdn_scan(u, w, qe, ke, qk, gc, *, batch, seq, n_heads, dk, rows, nbat):
    t, vdim = u.shape
    chunk = GDN_CHUNK
    seq3 = lambda a: a.reshape(batch, seq, a.shape[1])
    row = lambda width: pl.BlockSpec((nbat, rows, width), lambda b, j: (b, j, 0))
    kern = functools.partial(_gdn_scan_kernel, n_heads=n_heads, dk=dk, chunk=chunk, n_sub=rows // chunk,
                             nbat=nbat)
    o, s_new = pl.pallas_call(
        kern,
        grid=(batch // nbat, seq // rows),
        in_specs=[row(vdim)] * 4 + [row(qk.shape[1]), row(LANES)],
        out_specs=[row(vdim), pl.BlockSpec((nbat, n_heads, dk, dk), lambda b, j: (b, 0, 0, 0))],
        out_shape=[jax.ShapeDtypeStruct((batch, seq, vdim), F32),
                   jax.ShapeDtypeStruct((batch, n_heads, dk, dk), F32)],
        scratch_shapes=[pltpu.VMEM((nbat, n_heads, dk, dk), F32)],
        compiler_params=_cparams("arbitrary", "arbitrary"),
        name="gdn_scan",
    )(seq3(u), seq3(w), seq3(qe), seq3(ke), seq3(qk), seq3(gc))
    return o.reshape(t, vdim), s_new


def _gdn_step_kernel(q_ref, k_ref, v_ref, bga_ref, s_ref, o_ref, sout_ref, *, n_heads, dk, nb):
    s_ref = s_ref.at[0]
    pad = jnp.zeros((dk - n_heads, dk), F32)

    def body(b, carry):
        qt = jnp.concatenate([q_ref[b], pad], axis=0).T
        kt = jnp.concatenate([k_ref[b], pad], axis=0).T
        vb = v_ref[b]
        bg = bga_ref[pl.ds(b, 1), :]
        eg = jnp.exp(bg)
        rows = []
        for h in range(n_heads):
            st = s_ref[b, h] * eg[:, n_heads + h:n_heads + h + 1]
            kc = kt[:, h:h + 1]
            ks = jnp.sum(kc * st, axis=0, keepdims=True)
            delta = (vb[h:h + 1, :] - ks) * bg[:, h:h + 1]
            st = st + kc * delta
            sout_ref[b, h] = st
            rows.append(jnp.sum(qt[:, h:h + 1] * st, axis=0, keepdims=True))
        o_ref[b] = jnp.concatenate(rows, axis=0)
        return carry

    lax.fori_loop(0, nb, body, 0)


def _gdn_step(q, k, v, bga, states, layer, *, n_heads, dk, nb):
    b = q.shape[0]
    q3, k3, v3 = (a.reshape(b, n_heads, dk) for a in (q, k, v))
    vec = pl.BlockSpec((nb, n_heads, dk), lambda i: (i, 0, 0))
    st = pl.BlockSpec((nb, n_heads, dk, dk), lambda i: (i, 0, 0, 0))
    kern = functools.partial(_gdn_step_kernel, n_heads=n_heads, dk=dk, nb=nb)
    o, s_new = pl.pallas_call(
        kern,
        grid=(b // nb,),
        in_specs=[vec, vec, vec, pl.BlockSpec((nb, LANES), lambda i: (i, 0)),
                  pl.BlockSpec((1, nb, n_heads, dk, dk), lambda i: (layer, i, 0, 0, 0))],
        out_specs=[vec, st],
        out_shape=[jax.ShapeDtypeStruct((b, n_heads, dk), F32), jax.ShapeDtypeStruct(states.shape[1:], F32)],
        compiler_params=_cparams("arbitrary"),
        name="gdn_step",
    )(q3, k3, v3, bga, states)
    return o.reshape(b, n_heads * dk), s_new


def _gdn_out_kernel(o_ref, z_ref, x_ref, nw_ref, w_ref, y_ref, *, n_heads, dk):
    nw = nw_ref[...]
    parts = []
    for h in range(n_heads):
        sl = slice(h * dk, (h + 1) * dk)
        parts.append((_rms(o_ref[:, sl], nw, EPS) * _silu(z_ref[:, sl])).astype(BF16))
    y_ref[...] = x_ref[...] + _dot(jnp.concatenate(parts, axis=1), w_ref[...])


def _gdn_out(o, z, x, norm_w, w_out, *, tm, n_heads, dk):
    t, d = x.shape
    vdim = o.shape[1]
    row = lambda width: pl.BlockSpec((tm, width), lambda i: (i, 0))
    kern = functools.partial(_gdn_out_kernel, n_heads=n_heads, dk=dk)
    return pl.pallas_call(
        kern,
        grid=(t // tm,),
        in_specs=[row(vdim), row(vdim), row(d), _const_spec(norm_w.shape), _const_spec(w_out.shape)],
        out_specs=row(d),
        out_shape=jax.ShapeDtypeStruct((t, d), F32),
        compiler_params=_cparams("arbitrary"),
        name="gdn_out",
    )(o, z, x, norm_w, w_out)


def _ffn_tail(x, conv, up, cb_ref, wd_ref, gf_ref, o_ref, final_norm):
    act = (_silu(conv + cb_ref[...]) * up).astype(BF16)
    y = x + _dot(act, wd_ref[...])
    if final_norm:
        y = _rms(y, gf_ref[...], EPS)
    o_ref[...] = y


def _ffn_seq_kernel(x_ref, g_ref, wg_ref, wu_ref, cw_ref, cb_ref, wd_ref, gf_ref, o_ref, tail_ref,
                    buf_ref, act_ref, *, tm, tiles_per_seq, final_norm, cwid):
    x = x_ref[...]
    xn = _rms(x, g_ref[...], EPS).astype(BF16)
    first = (pl.program_id(0) % tiles_per_seq) == 0

    @pl.when(first)
    def _():
        buf_ref[0:HIST_ROWS, :] = jnp.zeros((HIST_ROWS, buf_ref.shape[1]), buf_ref.dtype)

    @pl.when(jnp.logical_not(first))
    def _():
        buf_ref[0:HIST_ROWS, :] = buf_ref[tm:tm + HIST_ROWS, :]

    cw = cw_ref[...]
    cb = cb_ref[...]
    width = cw.shape[0]
    n_chunks = wg_ref.shape[1] // cwid

    def mm(c):
        cs = slice(c * cwid, (c + 1) * cwid)
        return _dot(xn, wg_ref[:, cs]), _dot(xn, wu_ref[:, cs])

    nxt = mm(0)
    for c in range(n_chunks):
        cs = slice(c * cwid, (c + 1) * cwid)
        gate, up = nxt
        if c + 1 < n_chunks:
            nxt = mm(c + 1)
        buf_ref[HIST_ROWS:HIST_ROWS + tm, cs] = gate
        conv = cw[width - 1:width, cs] * gate
        for j in range(width - 1):
            conv = conv + cw[j:j + 1, cs] * buf_ref[pl.ds(HIST_ROWS - (width - 1) + j, tm), cs]
        act_ref[:, cs] = (_silu(conv + cb[:, cs]) * up).astype(BF16)
    tail_ref[0] = buf_ref[tm:tm + HIST_ROWS, :]
    y = x + _dot(act_ref[...], wd_ref[...])
    if final_norm:
        y = _rms(y, gf_ref[...], EPS)
    o_ref[...] = y


def _ffn_step_kernel(x_ref, g_ref, wg_ref, wu_ref, cw_ref, cb_ref, wd_ref, gf_ref, *rest,
                     n_hist, final_norm):
    hist_refs = rest[:n_hist]
    o_ref, raw_ref = rest[n_hist:]
    x = x_ref[...]
    xn = _rms(x, g_ref[...], EPS).astype(BF16)
    gate = _dot(xn, wg_ref[...])
    raw_ref[...] = gate
    cw = cw_ref[...]
    width = cw.shape[0]
    conv = cw[width - 1:width] * gate
    for j in range(width - 1):
        conv = conv + cw[j:j + 1] * hist_refs[j][...]
    _ffn_tail(x, conv, _dot(xn, wu_ref[...]), cb_ref, wd_ref, gf_ref, o_ref, final_norm)


def _ffn(x, g, wg, wu, conv_w, conv_b, wd, gf, *, tm, final_norm, seq=None, hist=None):
    t, d = x.shape
    d_ff = wg.shape[1]
    row = lambda width: pl.BlockSpec((tm, width), lambda i: (i, 0))
    common = [row(d), _const_spec((1, d)), _const_spec(wg.shape), _const_spec(wu.shape),
              _const_spec(conv_w.shape), _const_spec(conv_b.shape), _const_spec(wd.shape),
              _const_spec((1, d))]
    if hist is None:
        tps = seq // tm
        assert d_ff % (2 * LANES) == 0
        kern = functools.partial(_ffn_seq_kernel, tm=tm, tiles_per_seq=tps, final_norm=final_norm,
                                 cwid=2 * LANES)
        extra, extra_specs = [], []
        outs = [row(d), pl.BlockSpec((1, HIST_ROWS, d_ff), lambda i: (i // tps, 0, 0))]
        out_shapes = [jax.ShapeDtypeStruct((t, d), F32),
                      jax.ShapeDtypeStruct((t // seq, HIST_ROWS, d_ff), F32)]
        scratch = [pltpu.VMEM((tm + HIST_ROWS, d_ff), F32), pltpu.VMEM((tm, d_ff), BF16)]
    else:
        hist_arr, n_hist = hist
        kern = functools.partial(_ffn_step_kernel, n_hist=n_hist, final_norm=final_norm)
        extra = [hist_arr] * n_hist
        extra_specs = [pl.BlockSpec((tm, d_ff), functools.partial(lambda i, j: (i, j), j=j))
                       for j in range(n_hist)]
        outs = [row(d), row(d_ff)]
        out_shapes = [jax.ShapeDtypeStruct((t, d), F32), jax.ShapeDtypeStruct((t, d_ff), F32)]
        scratch = []
    return pl.pallas_call(
        kern,
        grid=(t // tm,),
        in_specs=common + extra_specs,
        out_specs=outs,
        out_shape=out_shapes,
        scratch_shapes=scratch,
        compiler_params=_cparams("arbitrary"),
        name="conv_ffn",
    )(x, g, wg, wu, conv_w, conv_b, wd, gf, *extra)


def _tile(n, pref):
    t = min(n, pref)
    while n % t:
        t //= 2
    return t


def kernel(x_prompt, x_sample, cache_k, cache_v, state_sconv, state_gdn_conv, state_gdn, state_ffn_conv, page_table, norm_mix, norm_ffn, norm_final, w_in_even, w_out_even, lambda_q1, lambda_k1, lambda_q2, lambda_k2, subln_w, sconv_w, w_in_odd, gdn_conv_w, a_log, dt_bias, gdn_norm_w, w_out_odd, w_up, ffn_conv_w, ffn_conv_b, w_down):
    batch, seq, d = x_prompt.shape
    dec_b = x_sample.shape[0]
    depth = norm_mix.shape[0]
    n_heads_a, e_a = cache_k.shape[3], cache_k.shape[4]
    n_att = n_heads_a * e_a
    rot = (e_a // 2) // 4
    w_b = state_sconv.shape[-1]
    sconv = sconv_w.shape[1]
    page = cache_k.shape[2]
    past_len = page_table.shape[1] * page
    n_heads_c, dk = state_gdn.shape[2], state_gdn.shape[3]
    kdim = n_heads_c * dk
    qkv_c = gdn_conv_w.shape[-1]
    gdn_conv = gdn_conv_w.shape[1]
    d_ff = ffn_conv_w.shape[-1]
    ffn_conv = ffn_conv_w.shape[1]
    t_p = batch * seq

    tm_p = _tile(seq, 256)
    tm_big = _tile(seq, 512)
    blk = _tile(seq, 512)
    scan_rows = _tile(seq, 256)

    xp = x_prompt.reshape(t_p, d)
    xs = x_sample.reshape(dec_b, d)
    rope_p = _rope_tables(jnp.arange(seq), e_a, rot)
    rope_s = _rope_tables(jnp.full((dec_b,), past_len), e_a, rot)
    gf = norm_final.reshape(1, d)

    kp, vp, ks, vs, scp, scs = [], [], [], [], [], []
    gcp, gcs, gsp, gss, fcp, fcs = [], [], [], [], [], []
    for l in range(depth):
        gm = norm_mix[l].reshape(1, d)
        if l % 2 == 0:
            e = l // 2
            lam_init = 0.8 - 0.6 * math.exp(-0.3 * l)
            w_in = w_in_even[e].astype(BF16)
            wvt = w_in_even[e][:, 2 * n_att:3 * n_att].T.astype(BF16)
            w_out = w_out_even[e].astype(BF16)
            lam_p = jnp.stack([lambda_q1[e], lambda_k1[e], lambda_q2[e], lambda_k2[e]])
            sw = subln_w[e].reshape(1, e_a)
            proj = functools.partial(_even_proj, n_heads=n_heads_a, e_a=e_a, w_b=w_b, half=rot // 2)
            q, k, v, kb, vt, bg, cx = proj(xp, gm, w_in, wvt, rope_p, tm=tm_p)
            att = _attn_prompt(q, kb, vt, lam_p, sw, batch=batch, seq=seq, n_heads=n_heads_a, e_a=e_a,
                               blk=blk, lam_init=lam_init)
            xp = _even_out(att, bg, cx, xp, sconv_w[e], w_out, tm=tm_big, seq=seq)
            kp.append(k.reshape(batch, seq, n_heads_a, e_a))
            vp.append(v.reshape(batch, seq, n_heads_a, e_a))
            cx3 = cx.reshape(batch, seq, w_b)
            if seq >= sconv - 1:
                scp.append(cx3[:, seq - (sconv - 1):])
            else:
                scp.append(jnp.concatenate([jnp.zeros((batch, sconv - 1 - seq, w_b), F32), cx3], axis=1))
            q, k, v, kb, vt, bg, cx = proj(xs, gm, w_in, wvt, rope_s, tm=dec_b)
            att = _attn_sample(q, k, v, cache_k, cache_v, e, page_table, lam_p, sw,
                               n_heads=n_heads_a, e_a=e_a, lam_init=lam_init)
            hist = [state_sconv[e, :, j] for j in range(sconv - 1)]
            xs = _even_out(att, bg, cx, xs, sconv_w[e], w_out, tm=dec_b, hist=hist)
            ks.append(k.reshape(dec_b, 1, n_heads_a, e_a))
            vs.append(v.reshape(dec_b, 1, n_heads_a, e_a))
            scs.append(jnp.concatenate([state_sconv[e, :, 1:], cx[:, None, :]], axis=1))
        else:
            o = l // 2
            w_in = w_in_odd[o]
            wqkv = w_in[:, :qkv_c].astype(BF16)
            wz = w_in[:, qkv_c:qkv_c + kdim].astype(BF16)
            wba = jnp.pad(w_in[:, qkv_c + kdim:], ((0, 0), (0, LANES - 2 * n_heads_c))).astype(BF16)
            w_out = w_out_odd[o].astype(BF16)
            lane_pad = lambda a: jnp.pad(a, (n_heads_c, LANES - 2 * n_heads_c)).reshape(1, LANES)
            al_pad, dtb_pad = lane_pad(a_log[o]), lane_pad(dt_bias[o])
            nw = gdn_norm_w[o].reshape(1, dk)
            proj = functools.partial(_gdn_proj, n_heads=n_heads_c, dk=dk)
            q, k, v, z, bga, tail = proj(xp, gm, wqkv, wz, wba, gdn_conv_w[o], al_pad, dtb_pad,
                                         tm=tm_p, seq=seq)
            u, w, qe, ke, qk, gc = _gdn_prep(q, k, v, bga, n_heads=n_heads_c, dk=dk,
                                             n_sub=_tile(seq // GDN_CHUNK, 4))
            og, s_new = _gdn_scan(u, w, qe, ke, qk, gc, batch=batch, seq=seq, n_heads=n_heads_c, dk=dk,
                                  rows=scan_rows, nbat=_tile(batch, 4))
            xp = _gdn_out(og, z, xp, nw, w_out, tm=tm_big, n_heads=n_heads_c, dk=dk)
            gcp.append(tail[:, HIST_ROWS - (gdn_conv - 1):])
            gsp.append(s_new)
            hist_arr = state_gdn_conv[o].reshape(dec_b, (gdn_conv - 1) * qkv_c)
            q, k, v, z, bga, raw = proj(xs, gm, wqkv, wz, wba, gdn_conv_w[o], al_pad, dtb_pad,
                                        tm=dec_b, hist=(hist_arr, gdn_conv - 1))
            og, s_new = _gdn_step(q, k, v, bga, state_gdn, o, n_heads=n_heads_c, dk=dk, nb=_tile(dec_b, 8))
            xs = _gdn_out(og, z, xs, nw, w_out, tm=dec_b, n_heads=n_heads_c, dk=dk)
            gcs.append(jnp.concatenate([state_gdn_conv[o, :, 1:], raw[:, None, :]], axis=1))
            gss.append(s_new)
        gn = norm_ffn[l].reshape(1, d)
        wg = w_up[l, :, :d_ff].astype(BF16)
        wu = w_up[l, :, d_ff:].astype(BF16)
        wd = w_down[l].astype(BF16)
        cb = ffn_conv_b[l].reshape(1, d_ff)
        last = l == depth - 1
        xp, tail = _ffn(xp, gn, wg, wu, ffn_conv_w[l], cb, wd, gf, tm=tm_big, final_norm=last, seq=seq)
        fcp.append(tail[:, HIST_ROWS - (ffn_conv - 1):])
        hist_arr = state_ffn_conv[l].reshape(dec_b, (ffn_conv - 1) * d_ff)
        xs, raw = _ffn(xs, gn, wg, wu, ffn_conv_w[l], cb, wd, gf, tm=dec_b, final_norm=last,
                       hist=(hist_arr, ffn_conv - 1))
        fcs.append(jnp.concatenate([state_ffn_conv[l, :, 1:], raw[:, None, :]], axis=1))
    return (xp.reshape(batch, seq, d), xs.reshape(dec_b, 1, d),
            jnp.stack(kp), jnp.stack(vp), jnp.stack(ks), jnp.stack(vs),
            jnp.stack(scp), jnp.stack(scs),
            jnp.stack(gcp), jnp.stack(gcs), jnp.stack(gsp), jnp.stack(gss),
            jnp.stack(fcp), jnp.stack(fcs))
```

```python
import functools
import math

import jax
import jax.numpy as jnp
from jax import lax
from jax.experimental import pallas as pl
from jax.experimental.pallas import tpu as pltpu

F32 = jnp.float32
BF16 = jnp.bfloat16

EPS = 1e-6
SUBLN_EPS = 1e-5
ROPE_THETA = 500000.0
GDN_CHUNK = 64
V7X_VMEM_LIMIT = 56 * 1024 * 1024
LANES = 128
HIST_ROWS = 8
BF16_ROWS = 16
LOG2E = math.log2(math.e)
VT_PAD = 16
ATTN_HEADS = 4
ATTN_SUB = 2


def _cparams(*sem):
    return pltpu.CompilerParams(dimension_semantics=sem, vmem_limit_bytes=V7X_VMEM_LIMIT)


def _const_spec(shape):
    nd = len(shape)
    return pl.BlockSpec(shape, lambda i: (0,) * nd, pipeline_mode=pl.Buffered(1))


def _rms(x, g, eps):
    return x * lax.rsqrt(jnp.mean(x * x, axis=-1, keepdims=True) + eps) * g


def _sigmoid(x):
    return 0.5 * jnp.tanh(0.5 * x) + 0.5


def _silu(x):
    h = 0.5 * x
    return h * jnp.tanh(h) + h


def _softplus(x):
    return jnp.maximum(x, 0.0) + jnp.log1p(jnp.exp(-jnp.abs(x)))


def _dot(a, b):
    return jnp.dot(a, b, preferred_element_type=F32)


def _dot_nt(a, b):
    return lax.dot_general(a, b, (((1,), (1,)), ((), ())), preferred_element_type=F32)


def _dot_tn(a, b):
    return lax.dot_general(a, b, (((0,), (0,)), ((), ())), preferred_element_type=F32)


def _split3(a):
    hi = a.astype(BF16)
    r = a - hi.astype(F32)
    mid = r.astype(BF16)
    lo = (r - mid.astype(F32)).astype(BF16)
    return hi, mid, lo


def _even_proj_kernel(x_ref, g_ref, w_ref, wvt_ref, c_ref, s1_ref, s2_ref,
                      q_ref, k_ref, v_ref, kb_ref, vt_ref, bg_ref, cx_ref,
                      *, tm, n_heads, e_a, w_b, half, qscale):
    xn = _rms(x_ref[...], g_ref[...], EPS).astype(BF16)
    n = n_heads * e_a
    c, s1, s2 = c_ref[...], s1_ref[...], s2_ref[...]

    def rope(t):
        return t * c + pltpu.roll(t, e_a - half, 1) * s1 + pltpu.roll(t, half, 1) * s2

    hqk = _dot(xn, w_ref[:, :2 * n])
    for h in range(n_heads):
        sl = slice(h * e_a, (h + 1) * e_a)
        q_ref[:, sl] = (rope(hqk[:, sl]) * qscale).astype(BF16)
        kr = rope(hqk[:, n + h * e_a:n + (h + 1) * e_a])
        k_ref[pl.ds(h, tm, stride=n_heads), :] = kr
        kb_ref[:, sl] = kr.astype(BF16)
    hv = _dot(xn, w_ref[:, 2 * n:3 * n])
    for h in range(n_heads):
        v_ref[pl.ds(h, tm, stride=n_heads), :] = hv[:, h * e_a:(h + 1) * e_a]
    hvt = _dot_nt(wvt_ref[...], xn).astype(BF16)
    ones_row = jnp.where(lax.broadcasted_iota(jnp.int32, (VT_PAD, tm), 0) == 0, 1.0, 0.0).astype(BF16)
    for h in range(n_heads):
        r0 = h * (e_a + VT_PAD)
        vt_ref[r0:r0 + e_a, :] = hvt[h * e_a:(h + 1) * e_a, :]
        vt_ref[r0 + e_a:r0 + e_a + VT_PAD, :] = ones_row
    o = 3 * n
    bg_ref[...] = _dot(xn, w_ref[:, o:o + w_b])
    hc = _dot(xn, w_ref[:, o + w_b:o + 3 * w_b])
    cx_ref[...] = hc[:, :w_b] * hc[:, w_b:]


def _even_proj(x, g, w, wvt, rope_tabs, *, tm, n_heads, e_a, w_b, half):
    t, d = x.shape
    n = n_heads * e_a
    c, s1, s2 = rope_tabs
    row = lambda width: pl.BlockSpec((tm, width), lambda i: (i, 0))
    tab_tiles = c.shape[0] // tm
    n_vt = n_heads * (e_a + VT_PAD)
    tab = pl.BlockSpec((tm, e_a), lambda i: (i % tab_tiles, 0))
    tok = pl.BlockSpec((tm * n_heads, e_a), lambda i: (i, 0))
    kern = functools.partial(_even_proj_kernel, tm=tm, n_heads=n_heads, e_a=e_a, w_b=w_b, half=half,
                             qscale=(e_a // 2) ** -0.5 * LOG2E)
    return pl.pallas_call(
        kern,
        grid=(t // tm,),
        in_specs=[row(d), _const_spec((1, d)), _const_spec(w.shape), _const_spec(wvt.shape),
                  tab, tab, tab],
        out_specs=[row(n), tok, tok, row(n), pl.BlockSpec((n_vt, tm), lambda i: (0, i)),
                   row(w_b), row(w_b)],
        out_shape=[jax.ShapeDtypeStruct((t, n), BF16), jax.ShapeDtypeStruct((t * n_heads, e_a), F32),
                   jax.ShapeDtypeStruct((t * n_heads, e_a), F32), jax.ShapeDtypeStruct((t, n), BF16),
                   jax.ShapeDtypeStruct((n_vt, t), BF16), jax.ShapeDtypeStruct((t, w_b), F32),
                   jax.ShapeDtypeStruct((t, w_b), F32)],
        compiler_params=_cparams("arbitrary"),
        name="even_proj",
    )(x, g, w, wvt, c, s1, s2)


def _rope_tables(pos, e_a, rot, dtype=F32):
    half = rot // 2
    dh = e_a // 2
    inv = ROPE_THETA ** (-jnp.arange(half, dtype=F32) * 2.0 / rot)
    ang = pos.astype(F32)[:, None] * inv[None, :]
    cos, sin = jnp.cos(ang).astype(dtype), jnp.sin(ang).astype(dtype)
    t = pos.shape[0]
    ones = jnp.ones((t, dh - rot), dtype)
    zeros_h = jnp.zeros((t, half), dtype)
    zeros_r = jnp.zeros((t, dh - rot), dtype)
    c = jnp.concatenate([cos, cos, ones], axis=1)
    s1 = jnp.concatenate([-sin, zeros_h, zeros_r], axis=1)
    s2 = jnp.concatenate([zeros_h, sin, zeros_r], axis=1)
    rep = lambda a: jnp.concatenate([a, a], axis=1)
    return rep(c), rep(s1), rep(s2)


def _lambda(lp_ref, lam_init):
    lp = lp_ref[...]
    a = jnp.sum(lp[0:1] * lp[1:2], axis=-1, keepdims=True)
    b = jnp.sum(lp[2:3] * lp[3:4], axis=-1, keepdims=True)
    return jnp.exp(a) - jnp.exp(b) + lam_init


def _attn_prompt_kernel(q_ref, k_ref, vt_ref, lp_ref, swc_ref, o_ref, m_ref, acc_ref,
                        *, blk, n_sub, nh, dh, lam_init):
    i = pl.program_id(2)
    e_a = 2 * dh
    vr = vt_ref.shape[0] // nh
    sub = blk // n_sub
    feat = lax.broadcasted_iota(jnp.int32, (e_a, blk), 0)
    qs = []
    for hh in range(nh):
        qt = q_ref[:, hh * e_a:(hh + 1) * e_a].astype(F32).T
        qs.append(jnp.concatenate([jnp.where(feat < dh, qt, 0.0), jnp.where(feat >= dh, qt, 0.0)],
                                  axis=1).astype(BF16))

    m_ref[...] = jnp.full(m_ref.shape, -1e30, F32)
    acc_ref[...] = jnp.zeros(acc_ref.shape, F32)

    def step(off, diag):
        def scores(hh, t):
            s = _dot(k_ref[pl.ds(off + t * sub, sub), hh * e_a:(hh + 1) * e_a], qs[hh])
            if diag:
                key = lax.broadcasted_iota(jnp.int32, s.shape, 0) + t * sub
                qry = lax.broadcasted_iota(jnp.int32, s.shape, 1)
                qry = jnp.where(qry >= blk, qry - blk, qry)
                s = jnp.where(key <= qry, s, jnp.finfo(F32).min)
            return s

        ss = [[scores(hh, t) for t in range(n_sub)] for hh in range(nh)]
        for hh in range(nh):
            for t in range(n_sub):
                m = m_ref[hh]
                m_new = jnp.maximum(m, jnp.max(ss[hh][t], axis=0, keepdims=True))
                p = jnp.exp2(ss[hh][t] - m_new)
                acc_ref[hh] = jnp.exp2(m - m_new) * acc_ref[hh] + _dot(
                    vt_ref[hh * vr:(hh + 1) * vr, pl.ds(off + t * sub, sub)], p.astype(BF16))
                m_ref[hh] = m_new

    def body(j, carry):
        step(pl.multiple_of(j * blk, blk), False)
        return carry

    lax.fori_loop(0, i, body, 0)
    step(pl.multiple_of(i * blk, blk), True)
    lam = _lambda(lp_ref, lam_init)
    for hh in range(nh):
        acc = acc_ref[hh]
        o = acc[:e_a] / acc[e_a:e_a + 1]
        att = o[:, :blk] - lam * o[:, blk:]
        y = att * lax.rsqrt(jnp.mean(att * att, axis=0, keepdims=True) + SUBLN_EPS) * swc_ref[...]
        o_ref[:, hh * e_a:(hh + 1) * e_a] = (y * (1.0 - lam_init)).T.astype(o_ref.dtype)


def _attn_prompt(q, kb, vt, lam_p, subln_w, *, batch, seq, n_heads, e_a, blk, lam_init):
    t = q.shape[0]
    nq = seq // blk
    swc = subln_w.reshape(e_a, 1)
    n_sub = ATTN_SUB if blk % (ATTN_SUB * LANES) == 0 else 1
    nh = ATTN_HEADS if n_heads % ATTN_HEADS == 0 else 1
    kern = functools.partial(_attn_prompt_kernel, blk=blk, n_sub=n_sub, nh=nh, dh=e_a // 2, lam_init=lam_init)
    return pl.pallas_call(
        kern,
        grid=(batch, n_heads // nh, nq),
        in_specs=[pl.BlockSpec((blk, nh * e_a), lambda b, h, i: (b * nq + i, h)),
                  pl.BlockSpec((seq, nh * e_a), lambda b, h, i: (b, h)),
                  pl.BlockSpec((nh * (e_a + VT_PAD), seq), lambda b, h, i: (h, b)),
                  pl.BlockSpec(lam_p.shape, lambda b, h, i: (0, 0)),
                  pl.BlockSpec(swc.shape, lambda b, h, i: (0, 0))],
        out_specs=pl.BlockSpec((blk, nh * e_a), lambda b, h, i: (b * nq + i, h)),
        out_shape=jax.ShapeDtypeStruct((t, n_heads * e_a), BF16),
        scratch_shapes=[pltpu.VMEM((nh, 1, 2 * blk), F32), pltpu.VMEM((nh, e_a + VT_PAD, 2 * blk), F32)],
        compiler_params=_cparams("arbitrary", "arbitrary", "arbitrary"),
        name="attn_prompt",
    )(q, kb, vt, lam_p, swc)


def _attn_sample_kernel(pt_ref, q_ref, kn_ref, vn_ref, lp_ref, sw_ref, *rest,
                        n_pages, n_heads, e_a, lam_init):
    del pt_ref
    k_refs = rest[:n_pages]
    v_refs = rest[n_pages:2 * n_pages]
    o_ref = rest[2 * n_pages]
    dh = e_a // 2
    nm = max(2 * n_heads, BF16_ROWS)
    rows_pp = k_refs[0].shape[1]
    rowi = lax.broadcasted_iota(jnp.int32, (nm, e_a), 0)
    lane = lax.broadcasted_iota(jnp.int32, (nm, e_a), 1)

    def per_head_rows(ref):
        t = ref[0]
        out = jnp.zeros((nm, e_a), F32)
        for h in range(n_heads):
            out = jnp.where((rowi >> 1) == h, jnp.broadcast_to(t[h:h + 1].astype(F32), (nm, e_a)), out)
        return out

    qm_f = jnp.where((lane // dh) == (rowi & 1), per_head_rows(q_ref), 0.0)
    qm = qm_f.astype(BF16)
    colh = lax.broadcasted_iota(jnp.int32, (nm, rows_pp), 1) % n_heads
    valid = colh == (lax.broadcasted_iota(jnp.int32, (nm, rows_pp), 0) >> 1)
    ss = [jnp.where(valid, _dot_nt(qm, k_refs[p][0].astype(BF16)), jnp.finfo(F32).min)
          for p in range(n_pages)]
    kn = per_head_rows(kn_ref).astype(BF16).astype(F32)
    vn = per_head_rows(vn_ref).astype(BF16).astype(F32)
    s_new = jnp.sum(qm.astype(F32) * kn, axis=-1, keepdims=True)
    m = s_new
    for s in ss:
        m = jnp.maximum(m, jnp.max(s, axis=-1, keepdims=True))
    p_new = jnp.exp2(s_new - m)
    l = p_new
    acc = p_new.astype(BF16).astype(F32) * vn
    for p in range(n_pages):
        pr = jnp.exp2(ss[p] - m)
        l = l + jnp.sum(pr, axis=-1, keepdims=True)
        acc = acc + _dot(pr.astype(BF16), v_refs[p][0].astype(BF16))
    o = acc / l
    lam = _lambda(lp_ref, lam_init)
    sw = sw_ref[...]
    for h in range(n_heads):
        att = o[2 * h:2 * h + 1] - lam * o[2 * h + 1:2 * h + 2]
        o_ref[0, h:h + 1, :] = (_rms(att, sw, SUBLN_EPS) * (1.0 - lam_init)).astype(o_ref.dtype)


def _attn_sample(q, k_new, v_new, cache_k, cache_v, layer, page_table, lam_p, subln_w,
                 *, n_heads, e_a, lam_init):
    b, n = q.shape
    n_pages = page_table.shape[1]
    n_phys, page = cache_k.shape[1], cache_k.shape[2]
    ck = cache_k.reshape(cache_k.shape[0] * n_phys, page * n_heads, e_a)
    cv = cache_v.reshape(cache_v.shape[0] * n_phys, page * n_heads, e_a)
    base = layer * n_phys
    tok = pl.BlockSpec((1, n_heads, e_a), lambda i, pt: (i, 0, 0))
    page_specs = [pl.BlockSpec((1, page * n_heads, e_a), functools.partial(
        lambda i, pt, p: (base + pt[i, p], 0, 0), p=p)) for p in range(n_pages)]
    kern = functools.partial(_attn_sample_kernel, n_pages=n_pages, n_heads=n_heads, e_a=e_a,
                             lam_init=lam_init)
    grid_spec = pltpu.PrefetchScalarGridSpec(
        num_scalar_prefetch=1,
        grid=(b,),
        in_specs=[tok, tok, tok,
                  pl.BlockSpec(lam_p.shape, lambda i, pt: (0, 0)),
                  pl.BlockSpec(subln_w.shape, lambda i, pt: (0, 0))] + page_specs + page_specs,
        out_specs=tok,
    )
    tok3 = lambda a: a.reshape(b, n_heads, e_a)
    out = pl.pallas_call(
        kern,
        grid_spec=grid_spec,
        out_shape=jax.ShapeDtypeStruct((b, n_heads, e_a), F32),
        compiler_params=_cparams("arbitrary"),
        name="attn_sample",
    )(page_table, tok3(q).astype(F32), tok3(k_new), tok3(v_new), lam_p, subln_w,
      *([ck] * n_pages), *([cv] * n_pages))
    return out.reshape(b, n)


def _seq_hist(buf_ref, cur, tiles_per_seq, tm):
    first = (pl.program_id(0) % tiles_per_seq) == 0

    @pl.when(first)
    def _():
        buf_ref[0:HIST_ROWS, :] = jnp.zeros((HIST_ROWS, buf_ref.shape[1]), buf_ref.dtype)

    @pl.when(jnp.logical_not(first))
    def _():
        buf_ref[0:HIST_ROWS, :] = buf_ref[tm:tm + HIST_ROWS, :]

    buf_ref[HIST_ROWS:HIST_ROWS + tm, :] = cur


def _even_out_seq_kernel(att_ref, bg_ref, cx_ref, x_ref, cw_ref, w_ref, o_ref, buf_ref,
                         *, tm, tiles_per_seq, n_att):
    cx = cx_ref[...]
    _seq_hist(buf_ref, cx, tiles_per_seq, tm)
    cw = cw_ref[...]
    width = cw.shape[0]
    conv = cw[width - 1:width] * cx
    for j in range(width - 1):
        conv = conv + cw[j:j + 1] * buf_ref[pl.ds(HIST_ROWS - (width - 1) + j, tm), :]
    yb = (bg_ref[...] * conv).astype(BF16)
    o_ref[...] = (x_ref[...] + _dot(att_ref[...].astype(BF16), w_ref[:n_att, :])
                  + _dot(yb, w_ref[n_att:, :]))


def _even_out_step_kernel(att_ref, bg_ref, cx_ref, x_ref, cw_ref, w_ref, *rest, n_att):
    hist_refs, o_ref = rest[:-1], rest[-1]
    cw = cw_ref[...]
    width = cw.shape[0]
    conv = cw[width - 1:width] * cx_ref[...]
    for j in range(width - 1):
        conv = conv + cw[j:j + 1] * hist_refs[j][...]
    yb = (bg_ref[...] * conv).astype(BF16)
    o_ref[...] = (x_ref[...] + _dot(att_ref[...].astype(BF16), w_ref[:n_att, :])
                  + _dot(yb, w_ref[n_att:, :]))


def _even_out(att, bg, cx, x, conv_w, w_out, *, tm, seq=None, hist=None):
    t, d = x.shape
    n_att, w_b = att.shape[1], bg.shape[1]
    row = lambda width: pl.BlockSpec((tm, width), lambda i: (i, 0))
    common = [row(n_att), row(w_b), row(w_b), row(d), _const_spec(conv_w.shape), _const_spec(w_out.shape)]
    if hist is None:
        kern = functools.partial(_even_out_seq_kernel, tm=tm, tiles_per_seq=seq // tm, n_att=n_att)
        extra, extra_specs = [], []
        scratch = [pltpu.VMEM((tm + HIST_ROWS, w_b), F32)]
    else:
        kern = functools.partial(_even_out_step_kernel, n_att=n_att)
        extra, extra_specs = list(hist), [row(w_b)] * len(hist)
        scratch = []
    return pl.pallas_call(
        kern,
        grid=(t // tm,),
        in_specs=common + extra_specs,
        out_specs=row(d),
        out_shape=jax.ShapeDtypeStruct((t, d), F32),
        scratch_shapes=scratch,
        compiler_params=_cparams("arbitrary"),
        name="even_out",
    )(att, bg, cx, x, conv_w, w_out, *extra)


def _gdn_gates(xn, wz_ref, wba_ref, al_ref, dtb_ref, z_ref, bga_ref, *, n_heads):
    z_ref[...] = _dot(xn, wz_ref[...]).astype(z_ref.dtype)
    ba = _dot(xn, wba_ref[...])
    beta = _sigmoid(ba)
    g = -jnp.exp(al_ref[...]) * _softplus(ba + dtb_ref[...])
    lane = lax.broadcasted_iota(jnp.int32, ba.shape, 1)
    bga_ref[...] = jnp.where(lane < n_heads, beta, g)


def _gdn_qkv_chunk(conv, c0, q_ref, k_ref, v_ref, *, n_heads, dk):
    kdim = n_heads * dk
    act = _silu(conv)
    for i in range(conv.shape[1] // dk):
        t = act[:, i * dk:(i + 1) * dk]
        ch = c0 + i * dk
        if ch < 2 * kdim:
            t = t * lax.rsqrt(jnp.sum(t * t, axis=-1, keepdims=True) + EPS)
        if ch < kdim:
            q_ref[:, ch:ch + dk] = t * (dk ** -0.5)
        elif ch < 2 * kdim:
            k_ref[:, ch - kdim:ch - kdim + dk] = t
        else:
            v_ref[:, ch - 2 * kdim:ch - 2 * kdim + dk] = t


def _gdn_proj_seq_kernel(x_ref, g_ref, wqkv_ref, wz_ref, wba_ref, cw_ref, al_ref, dtb_ref,
                         q_ref, k_ref, v_ref, z_ref, bga_ref, tail_ref, buf_ref,
                         *, tm, tiles_per_seq, n_heads, dk, cwid):
    xn = _rms(x_ref[...], g_ref[...], EPS).astype(BF16)
    first = (pl.program_id(0) % tiles_per_seq) == 0

    @pl.when(first)
    def _():
        buf_ref[0:HIST_ROWS, :] = jnp.zeros((HIST_ROWS, buf_ref.shape[1]), buf_ref.dtype)

    @pl.when(jnp.logical_not(first))
    def _():
        buf_ref[0:HIST_ROWS, :] = buf_ref[tm:tm + HIST_ROWS, :]

    cw = cw_ref[...]
    width = cw.shape[0]
    n_chunks = wqkv_ref.shape[1] // cwid
    mm = lambda c: _dot(xn, wqkv_ref[:, c * cwid:(c + 1) * cwid])
    h_next = mm(0)
    for c in range(n_chunks):
        cs = slice(c * cwid, (c + 1) * cwid)
        hm = h_next
        if c + 1 < n_chunks:
            h_next = mm(c + 1)
        buf_ref[HIST_ROWS:HIST_ROWS + tm, cs] = hm
        conv = cw[width - 1:width, cs] * hm
        for j in range(width - 1):
            conv = conv + cw[j:j + 1, cs] * buf_ref[pl.ds(HIST_ROWS - (width - 1) + j, tm), cs]
        _gdn_qkv_chunk(conv, c * cwid, q_ref, k_ref, v_ref, n_heads=n_heads, dk=dk)
    tail_ref[0] = buf_ref[tm:tm + HIST_ROWS, :]
    _gdn_gates(xn, wz_ref, wba_ref, al_ref, dtb_ref, z_ref, bga_ref, n_heads=n_heads)


def _gdn_proj_step_kernel(x_ref, g_ref, wqkv_ref, wz_ref, wba_ref, cw_ref, al_ref, dtb_ref, *rest,
                          n_heads, dk, n_hist):
    hist_refs = rest[:n_hist]
    q_ref, k_ref, v_ref, z_ref, bga_ref, raw_ref = rest[n_hist:]
    xn = _rms(x_ref[...], g_ref[...], EPS).astype(BF16)
    hm = _dot(xn, wqkv_ref[...])
    raw_ref[...] = hm
    cw = cw_ref[...]
    width = cw.shape[0]
    conv = cw[width - 1:width] * hm
    for j in range(width - 1):
        conv = conv + cw[j:j + 1] * hist_refs[j][...]
    _gdn_qkv_chunk(conv, 0, q_ref, k_ref, v_ref, n_heads=n_heads, dk=dk)
    _gdn_gates(xn, wz_ref, wba_ref, al_ref, dtb_ref, z_ref, bga_ref, n_heads=n_heads)


def _gdn_proj(x, g, wqkv, wz, wba, conv_w, al_pad, dtb_pad, *, tm, n_heads, dk, seq=None, hist=None):
    t, d = x.shape
    qkv_c = wqkv.shape[1]
    vdim = wz.shape[1]
    row = lambda width: pl.BlockSpec((tm, width), lambda i: (i, 0))
    common = [row(d), _const_spec((1, d)), _const_spec(wqkv.shape), _const_spec(wz.shape),
              _const_spec(wba.shape), _const_spec(conv_w.shape), _const_spec(al_pad.shape),
              _const_spec(dtb_pad.shape)]
    outs = [row(vdim), row(vdim), row(vdim), row(vdim), row(LANES)]
    out_shapes = ([jax.ShapeDtypeStruct((t, vdim), F32)] * 3 + [jax.ShapeDtypeStruct((t, vdim), BF16)]
                  + [jax.ShapeDtypeStruct((t, LANES), F32)])
    if hist is None:
        tps = seq // tm
        kern = functools.partial(_gdn_proj_seq_kernel, tm=tm, tiles_per_seq=tps, n_heads=n_heads, dk=dk,
                                 cwid=2 * dk)
        extra, extra_specs = [], []
        outs.append(pl.BlockSpec((1, HIST_ROWS, qkv_c), lambda i: (i // tps, 0, 0)))
        out_shapes.append(jax.ShapeDtypeStruct((t // seq, HIST_ROWS, qkv_c), F32))
        scratch = [pltpu.VMEM((tm + HIST_ROWS, qkv_c), F32)]
    else:
        hist_arr, n_hist = hist
        kern = functools.partial(_gdn_proj_step_kernel, n_heads=n_heads, dk=dk, n_hist=n_hist)
        extra = [hist_arr] * n_hist
        extra_specs = [pl.BlockSpec((tm, qkv_c), functools.partial(lambda i, j: (i, j), j=j))
                       for j in range(n_hist)]
        outs.append(row(qkv_c))
        out_shapes.append(jax.ShapeDtypeStruct((t, qkv_c), F32))
        scratch = []
    return pl.pallas_call(
        kern,
        grid=(t // tm,),
        in_specs=common + extra_specs,
        out_specs=outs,
        out_shape=out_shapes,
        scratch_shapes=scratch,
        compiler_params=_cparams("arbitrary"),
        name="gdn_proj",
    )(x, g, wqkv, wz, wba, conv_w, al_pad, dtb_pad, *extra)


def _pair_blockdiag(a, b):
    z = jnp.zeros_like(a)
    return jnp.concatenate([jnp.concatenate([a, z], axis=1), jnp.concatenate([z, b], axis=1)], axis=0)


def _gdn_prep_kernel(q_ref, k_ref, v_ref, bga_ref, u_ref, w_ref, qe_ref, ke_ref, qk_ref, gc_ref,
                     *, n_heads, dk, chunk, n_sub):
    n_pairs = n_heads // 2
    rows_blk = chunk * n_sub
    bga = bga_ref[...]
    r = lax.broadcasted_iota(jnp.int32, (rows_blk, rows_blk), 0)
    c = lax.broadcasted_iota(jnp.int32, (rows_blk, rows_blk), 1)
    sh = chunk.bit_length() - 1
    tril = jnp.where(jnp.logical_and(r >= c, (r >> sh) == (c >> sh)), 1.0, 0.0).astype(BF16)
    b3 = _split3(bga)
    gc = _dot(tril, b3[0]) + (_dot(tril, b3[1]) + _dot(tril, b3[2]))
    gc_ref[...] = gc
    lane = lax.broadcasted_iota(jnp.int32, (rows_blk, LANES), 1)
    is_g = jnp.logical_and(lane >= n_heads, lane < 2 * n_heads)
    odd = ((lane - n_heads) & 1) == 1
    gce = jnp.where(jnp.logical_and(is_g, jnp.logical_not(odd)), gc, 0.0)
    gco = jnp.where(jnp.logical_and(is_g, odd), gc, 0.0)
    ar = lax.broadcasted_iota(jnp.int32, (BF16_ROWS, LANES), 0)
    al = lax.broadcasted_iota(jnp.int32, (BF16_ROWS, LANES), 1)
    asel = jnp.where(jnp.logical_and(jnp.logical_and(al >= n_heads, ar < n_pairs),
                                     ((al - n_heads) >> 1) == ar), 1.0, 0.0).astype(BF16)
    prow = lax.broadcasted_iota(jnp.int32, (chunk, 2 * chunk), 0)
    plane = lax.broadcasted_iota(jnp.int32, (chunk, 2 * chunk), 1)
    pcol = plane & (chunk - 1)
    lo_half = plane < chunk
    incl = prow >= pcol
    strict = prow > pcol

    def bd(m):
        return jnp.concatenate([jnp.where(lo_half, m, 0.0), jnp.where(lo_half, 0.0, m)], axis=0).astype(BF16)

    items = [(ci, p) for ci in range(n_sub) for p in range(n_pairs)]
    lmat, tinv, vbn, kbe = {}, {}, {}, {}
    for ci, p in items:
        rs = slice(ci * chunk, (ci + 1) * chunk)
        cs = slice(2 * p * dk, 2 * (p + 1) * dk)
        g3 = _split3(jnp.concatenate([gce[rs], gco[rs]], axis=0))
        grow = _dot_nt(asel, g3[0]) + (_dot_nt(asel, g3[1]) + _dot_nt(asel, g3[2]))
        k2, q2, v2 = k_ref[rs, cs], q_ref[rs, cs], v_ref[rs, cs]
        ha, hb = 2 * p, 2 * p + 1

        def nat(a, b, rows=chunk):
            return jnp.concatenate([jnp.broadcast_to(a, (rows, dk)), jnp.broadcast_to(b, (rows, dk))], axis=1)

        gca, gcb = gc[rs, n_heads + ha:n_heads + ha + 1], gc[rs, n_heads + hb:n_heads + hb + 1]
        bnat = nat(bga[rs, ha:ha + 1], bga[rs, hb:hb + 1])
        gnat = nat(gca, gcb)
        glast = nat(gca[chunk - 1:chunk], gcb[chunk - 1:chunk], rows=1)
        egn = jnp.exp(gnat)
        kb2 = k2 * bnat
        k2b = k2.astype(BF16)
        rhs_t = _pair_blockdiag(k2b[:, :dk], k2b[:, dk:])
        decay = jnp.where(incl, jnp.exp(jnp.where(incl, jnp.where(lo_half, gca, gcb) - grow[p:p + 1], 0.0)), 0.0)
        lm = jnp.where(strict, _dot_nt(kb2.astype(BF16), rhs_t) * decay, 0.0)
        qk_ref[rs, 2 * p * chunk:2 * (p + 1) * chunk] = jnp.where(
            incl, _dot_nt(q2.astype(BF16), rhs_t) * decay, 0.0).astype(BF16)
        qe_ref[rs, cs] = (q2 * egn).astype(BF16)
        ke_ref[rs, cs] = (k2 * jnp.exp(glast - gnat)).astype(BF16)
        lmat[ci, p] = lm
        tinv[ci, p] = jnp.where(prow == pcol, 1.0, 0.0) - jnp.where((prow >> 1) == (pcol >> 1), lm, 0.0)
        vbn[ci, p] = (v2 * bnat).astype(BF16)
        kbe[ci, p] = (kb2 * egn).astype(BF16)
    s = 1
    while 2 * (1 << s) <= chunk:
        joins = jnp.logical_and((prow >> (s + 1)) == (pcol >> (s + 1)), (prow >> s) != (pcol >> s))
        x = {it: _dot(tinv[it].astype(BF16), bd(jnp.where(joins, lmat[it], 0.0))) for it in items}
        tinv = {it: tinv[it] - _dot(x[it].astype(BF16), bd(tinv[it])) for it in items}
        s += 1
    for ci, p in items:
        rs = slice(ci * chunk, (ci + 1) * chunk)
        cs = slice(2 * p * dk, 2 * (p + 1) * dk)
        tb = tinv[ci, p].astype(BF16)
        u_ref[rs, cs] = _dot(tb, _pair_blockdiag(vbn[ci, p][:, :dk], vbn[ci, p][:, dk:]))
        w_ref[rs, cs] = _dot(tb, _pair_blockdiag(kbe[ci, p][:, :dk], kbe[ci, p][:, dk:])).astype(BF16)


def _gdn_prep(q, k, v, bga, *, n_heads, dk, n_sub):
    t, vdim = q.shape
    chunk = GDN_CHUNK
    assert 2 * chunk == LANES and n_heads % 2 == 0 and 2 * n_heads <= LANES
    rows = chunk * n_sub
    qk_w = n_heads * chunk
    row = lambda width: pl.BlockSpec((rows, width), lambda i: (i, 0))
    kern = functools.partial(_gdn_prep_kernel, n_heads=n_heads, dk=dk, chunk=chunk, n_sub=n_sub)
    return pl.pallas_call(
        kern,
        grid=(t // rows,),
        in_specs=[row(vdim), row(vdim), row(vdim), row(LANES)],
        out_specs=[row(vdim), row(vdim), row(vdim), row(vdim), row(qk_w), row(LANES)],
        out_shape=[jax.ShapeDtypeStruct((t, vdim), F32)] + [jax.ShapeDtypeStruct((t, vdim), BF16)] * 3
                  + [jax.ShapeDtypeStruct((t, qk_w), BF16), jax.ShapeDtypeStruct((t, LANES), F32)],
        compiler_params=_cparams("arbitrary"),
        name="gdn_prep",
    )(q, k, v, bga)


def _gdn_scan_kernel(u_ref, w_ref, qe_ref, ke_ref, qk_ref, gc_ref, o_ref, sout_ref, s_ref,
                     *, n_heads, dk, chunk, n_sub, nbat):
    j = pl.program_id(1)

    @pl.when(j == 0)
    def _():
        s_ref[...] = jnp.zeros(s_ref.shape, s_ref.dtype)

    sls = [slice(h * dk, (h + 1) * dk) for h in range(n_heads)]
    items = [(bi, h) for bi in range(nbat) for h in range(n_heads)]

    def body(c, carry):
        r0 = pl.multiple_of(c * chunk, chunk)
        rows = pl.ds(r0, chunk)
        st = {it: s_ref[it[0], it[1]] for it in items}
        sb = {it: st[it].astype(BF16) for it in items}
        ws = {(bi, h): _dot(jnp.concatenate([w_ref[bi, rows, sls[h]], qe_ref[bi, rows, sls[h]]], axis=0),
                            sb[bi, h]) for bi, h in items}
        vb = {(bi, h): (u_ref[bi, rows, sls[h]] - ws[bi, h][:chunk]).astype(BF16) for bi, h in items}
        for bi in range(nbat):
            for p in range(n_heads // 2):
                ha, hb = 2 * p, 2 * p + 1
                intra = _dot(qk_ref[bi, rows, 2 * p * chunk:2 * (p + 1) * chunk],
                             _pair_blockdiag(vb[bi, ha], vb[bi, hb]))
                o_ref[bi, rows, 2 * p * dk:2 * (p + 1) * dk] = (
                    jnp.concatenate([ws[bi, ha][chunk:], ws[bi, hb][chunk:]], axis=1) + intra)
        for bi in range(nbat):
            eg = jnp.exp(gc_ref[bi, pl.ds(r0 + chunk - 1, 1), :])
            for h in range(n_heads):
                s_ref[bi, h] = (st[bi, h] * eg[:, n_heads + h:n_heads + h + 1]
                                + _dot_tn(ke_ref[bi, rows, sls[h]], vb[bi, h]))
        return carry

    lax.fori_loop(0, n_sub, body, 0)

    @pl.when(j == pl.num_programs(1) - 1)
    def _():
        sout_ref[...] = s_ref[...]


def _gdn_scan(u, w, qe, ke, qk, gc, *, batch, seq, n_heads, dk, rows, nbat):
    t, vdim = u.shape
    chunk = GDN_CHUNK
    seq3 = lambda a: a.reshape(batch, seq, a.shape[1])
    row = lambda width: pl.BlockSpec((nbat, rows, width), lambda b, j: (b, j, 0))
    kern = functools.partial(_gdn_scan_kernel, n_heads=n_heads, dk=dk, chunk=chunk, n_sub=rows // chunk,
                             nbat=nbat)
    o, s_new = pl.pallas_call(
        kern,
        grid=(batch // nbat, seq // rows),
        in_specs=[row(vdim)] * 4 + [row(qk.shape[1]), row(LANES)],
        out_specs=[row(vdim), pl.BlockSpec((nbat, n_heads, dk, dk), lambda b, j: (b, 0, 0, 0))],
        out_shape=[jax.ShapeDtypeStruct((batch, seq, vdim), F32),
                   jax.ShapeDtypeStruct((batch, n_heads, dk, dk), F32)],
        scratch_shapes=[pltpu.VMEM((nbat, n_heads, dk, dk), F32)],
        compiler_params=_cparams("arbitrary", "arbitrary"),
        name="gdn_scan",
    )(seq3(u), seq3(w), seq3(qe), seq3(ke), seq3(qk), seq3(gc))
    return o.reshape(t, vdim), s_new


def _gdn_step_kernel(q_ref, k_ref, v_ref, bga_ref, s_ref, o_ref, sout_ref, *, n_heads, dk, nb):
    s_ref = s_ref.at[0]
    pad = jnp.zeros((dk - n_heads, dk), F32)

    def body(b, carry):
        qt = jnp.concatenate([q_ref[b], pad], axis=0).T
        kt = jnp.concatenate([k_ref[b], pad], axis=0).T
        vb = v_ref[b]
        bg = bga_ref[pl.ds(b, 1), :]
        eg = jnp.exp(bg)
        rows = []
        for h in range(n_heads):
            st = s_ref[b, h] * eg[:, n_heads + h:n_heads + h + 1]
            kc = kt[:, h:h + 1]
            ks = jnp.sum(kc * st, axis=0, keepdims=True)
            delta = (vb[h:h + 1, :] - ks) * bg[:, h:h + 1]
            st = st + kc * delta
            sout_ref[b, h] = st
            rows.append(jnp.sum(qt[:, h:h + 1] * st, axis=0, keepdims=True))
        o_ref[b] = jnp.concatenate(rows, axis=0)
        return carry

    lax.fori_loop(0, nb, body, 0)


def _gdn_step(q, k, v, bga, states, layer, *, n_heads, dk, nb):
    b = q.shape[0]
    q3, k3, v3 = (a.reshape(b, n_heads, dk) for a in (q, k, v))
    vec = pl.BlockSpec((nb, n_heads, dk), lambda i: (i, 0, 0))
    st = pl.BlockSpec((nb, n_heads, dk, dk), lambda i: (i, 0, 0, 0))
    kern = functools.partial(_gdn_step_kernel, n_heads=n_heads, dk=dk, nb=nb)
    o, s_new = pl.pallas_call(
        kern,
        grid=(b // nb,),
        in_specs=[vec, vec, vec, pl.BlockSpec((nb, LANES), lambda i: (i, 0)),
                  pl.BlockSpec((1, nb, n_heads, dk, dk), lambda i: (layer, i, 0, 0, 0))],
        out_specs=[vec, st],
        out_shape=[jax.ShapeDtypeStruct((b, n_heads, dk), F32), jax.ShapeDtypeStruct(states.shape[1:], F32)],
        compiler_params=_cparams("arbitrary"),
        name="gdn_step",
    )(q3, k3, v3, bga, states)
    return o.reshape(b, n_heads * dk), s_new


def _gdn_out_kernel(o_ref, z_ref, x_ref, nw_ref, w_ref, y_ref, *, n_heads, dk):
    nw = nw_ref[...]
    parts = []
    for h in range(n_heads):
        sl = slice(h * dk, (h + 1) * dk)
        parts.append((_rms(o_ref[:, sl], nw, EPS) * _silu(z_ref[:, sl].astype(F32))).astype(BF16))
    y_ref[...] = x_ref[...] + _dot(jnp.concatenate(parts, axis=1), w_ref[...])


def _gdn_out(o, z, x, norm_w, w_out, *, tm, n_heads, dk):
    t, d = x.shape
    vdim = o.shape[1]
    row = lambda width: pl.BlockSpec((tm, width), lambda i: (i, 0))
    kern = functools.partial(_gdn_out_kernel, n_heads=n_heads, dk=dk)
    return pl.pallas_call(
        kern,
        grid=(t // tm,),
        in_specs=[row(vdim), row(vdim), row(d), _const_spec(norm_w.shape), _const_spec(w_out.shape)],
        out_specs=row(d),
        out_shape=jax.ShapeDtypeStruct((t, d), F32),
        compiler_params=_cparams("arbitrary"),
        name="gdn_out",
    )(o, z, x, norm_w, w_out)


def _ffn_tail(x, conv, up, cb_ref, wd_ref, gf_ref, o_ref, final_norm):
    act = (_silu(conv + cb_ref[...]) * up).astype(BF16)
    y = x + _dot(act, wd_ref[...])
    if final_norm:
        y = _rms(y, gf_ref[...], EPS)
    o_ref[...] = y


def _ffn_seq_kernel(x_ref, g_ref, wg_ref, wu_ref, cw_ref, cb_ref, wd_ref, gf_ref, o_ref, tail_ref,
                    buf_ref, act_ref, *, tm, tiles_per_seq, final_norm, cwid):
    x = x_ref[...]
    xn = _rms(x, g_ref[...], EPS).astype(BF16)
    first = (pl.program_id(0) % tiles_per_seq) == 0

    @pl.when(first)
    def _():
        buf_ref[0:HIST_ROWS, :] = jnp.zeros((HIST_ROWS, buf_ref.shape[1]), buf_ref.dtype)

    @pl.when(jnp.logical_not(first))
    def _():
        buf_ref[0:HIST_ROWS, :] = buf_ref[tm:tm + HIST_ROWS, :]

    cw = cw_ref[...]
    cb = cb_ref[...]
    width = cw.shape[0]
    n_chunks = wg_ref.shape[1] // cwid

    def mm(c):
        cs = slice(c * cwid, (c + 1) * cwid)
        return _dot(xn, wg_ref[:, cs]), _dot(xn, wu_ref[:, cs])

    nxt = mm(0)
    for c in range(n_chunks):
        cs = slice(c * cwid, (c + 1) * cwid)
        gate, up = nxt
        if c + 1 < n_chunks:
            nxt = mm(c + 1)
        buf_ref[HIST_ROWS:HIST_ROWS + tm, cs] = gate
        conv = cw[width - 1:width, cs] * gate
        for j in range(width - 1):
            conv = conv + cw[j:j + 1, cs] * buf_ref[pl.ds(HIST_ROWS - (width - 1) + j, tm), cs]
        act_ref[:, cs] = (_silu(conv + cb[:, cs]) * up).astype(BF16)
    tail_ref[0] = buf_ref[tm:tm + HIST_ROWS, :]
    y = x + _dot(act_ref[...], wd_ref[...])
    if final_norm:
        y = _rms(y, gf_ref[...], EPS)
    o_ref[...] = y


def _ffn_step_kernel(x_ref, g_ref, wg_ref, wu_ref, cw_ref, cb_ref, wd_ref, gf_ref, *rest,
                     n_hist, final_norm):
    hist_refs = rest[:n_hist]
    o_ref, raw_ref = rest[n_hist:]
    x = x_ref[...]
    xn = _rms(x, g_ref[...], EPS).astype(BF16)
    gate = _dot(xn, wg_ref[...])
    raw_ref[...] = gate
    cw = cw_ref[...]
    width = cw.shape[0]
    conv = cw[width - 1:width] * gate
    for j in range(width - 1):
        conv = conv + cw[j:j + 1] * hist_refs[j][...]
    _ffn_tail(x, conv, _dot(xn, wu_ref[...]), cb_ref, wd_ref, gf_ref, o_ref, final_norm)


def _ffn(x, g, wg, wu, conv_w, conv_b, wd, gf, *, tm, final_norm, seq=None, hist=None):
    t, d = x.shape
    d_ff = wg.shape[1]
    row = lambda width: pl.BlockSpec((tm, width), lambda i: (i, 0))
    common = [row(d), _const_spec((1, d)), _const_spec(wg.shape), _const_spec(wu.shape),
              _const_spec(conv_w.shape), _const_spec(conv_b.shape), _const_spec(wd.shape),
              _const_spec((1, d))]
    if hist is None:
        tps = seq // tm
        assert d_ff % (2 * LANES) == 0
        kern = functools.partial(_ffn_seq_kernel, tm=tm, tiles_per_seq=tps, final_norm=final_norm,
                                 cwid=2 * LANES)
        extra, extra_specs = [], []
        outs = [row(d), pl.BlockSpec((1, HIST_ROWS, d_ff), lambda i: (i // tps, 0, 0))]
        out_shapes = [jax.ShapeDtypeStruct((t, d), F32),
                      jax.ShapeDtypeStruct((t // seq, HIST_ROWS, d_ff), F32)]
        scratch = [pltpu.VMEM((tm + HIST_ROWS, d_ff), F32), pltpu.VMEM((tm, d_ff), BF16)]
    else:
        hist_arr, n_hist = hist
        kern = functools.partial(_ffn_step_kernel, n_hist=n_hist, final_norm=final_norm)
        extra = [hist_arr] * n_hist
        extra_specs = [pl.BlockSpec((tm, d_ff), functools.partial(lambda i, j: (i, j), j=j))
                       for j in range(n_hist)]
        outs = [row(d), row(d_ff)]
        out_shapes = [jax.ShapeDtypeStruct((t, d), F32), jax.ShapeDtypeStruct((t, d_ff), F32)]
        scratch = []
    return pl.pallas_call(
        kern,
        grid=(t // tm,),
        in_specs=common + extra_specs,
        out_specs=outs,
        out_shape=out_shapes,
        scratch_shapes=scratch,
        compiler_params=_cparams("arbitrary"),
        name="conv_ffn",
    )(x, g, wg, wu, conv_w, conv_b, wd, gf, *extra)


def _tile(n, pref):
    t = min(n, pref)
    while n % t:
        t //= 2
    return t


def kernel(x_prompt, x_sample, cache_k, cache_v, state_sconv, state_gdn_conv, state_gdn, state_ffn_conv, page_table, norm_mix, norm_ffn, norm_final, w_in_even, w_out_even, lambda_q1, lambda_k1, lambda_q2, lambda_k2, subln_w, sconv_w, w_in_odd, gdn_conv_w, a_log, dt_bias, gdn_norm_w, w_out_odd, w_up, ffn_conv_w, ffn_conv_b, w_down):
    batch, seq, d = x_prompt.shape
    dec_b = x_sample.shape[0]
    depth = norm_mix.shape[0]
    n_heads_a, e_a = cache_k.shape[3], cache_k.shape[4]
    n_att = n_heads_a * e_a
    rot = (e_a // 2) // 4
    w_b = state_sconv.shape[-1]
    sconv = sconv_w.shape[1]
    page = cache_k.shape[2]
    past_len = page_table.shape[1] * page
    n_heads_c, dk = state_gdn.shape[2], state_gdn.shape[3]
    kdim = n_heads_c * dk
    qkv_c = gdn_conv_w.shape[-1]
    gdn_conv = gdn_conv_w.shape[1]
    d_ff = ffn_conv_w.shape[-1]
    ffn_conv = ffn_conv_w.shape[1]
    t_p = batch * seq

    tm_p = _tile(seq, 256)
    tm_big = _tile(seq, 512)
    blk = _tile(seq, 512)
    scan_rows = _tile(seq, 256)

    xp = x_prompt.reshape(t_p, d)
    xs = x_sample.reshape(dec_b, d)
    rope_p = _rope_tables(jnp.arange(seq), e_a, rot)
    rope_s = _rope_tables(jnp.full((dec_b,), past_len), e_a, rot)
    gf = norm_final.reshape(1, d)

    kp, vp, ks, vs, scp, scs = [], [], [], [], [], []
    gcp, gcs, gsp, gss, fcp, fcs = [], [], [], [], [], []
    for l in range(depth):
        gm = norm_mix[l].reshape(1, d)
        if l % 2 == 0:
            e = l // 2
            lam_init = 0.8 - 0.6 * math.exp(-0.3 * l)
            w_in = w_in_even[e].astype(BF16)
            wvt = w_in_even[e][:, 2 * n_att:3 * n_att].T.astype(BF16)
            w_out = w_out_even[e].astype(BF16)
            lam_p = jnp.stack([lambda_q1[e], lambda_k1[e], lambda_q2[e], lambda_k2[e]])
            sw = subln_w[e].reshape(1, e_a)
            proj = functools.partial(_even_proj, n_heads=n_heads_a, e_a=e_a, w_b=w_b, half=rot // 2)
            q, k, v, kb, vt, bg, cx = proj(xp, gm, w_in, wvt, rope_p, tm=tm_p)
            att = _attn_prompt(q, kb, vt, lam_p, sw, batch=batch, seq=seq, n_heads=n_heads_a, e_a=e_a,
                               blk=blk, lam_init=lam_init)
            xp = _even_out(att, bg, cx, xp, sconv_w[e], w_out, tm=tm_big, seq=seq)
            kp.append(k.reshape(batch, seq, n_heads_a, e_a))
            vp.append(v.reshape(batch, seq, n_heads_a, e_a))
            cx3 = cx.reshape(batch, seq, w_b)
            if seq >= sconv - 1:
                scp.append(cx3[:, seq - (sconv - 1):])
            else:
                scp.append(jnp.concatenate([jnp.zeros((batch, sconv - 1 - seq, w_b), F32), cx3], axis=1))
            q, k, v, kb, vt, bg, cx = proj(xs, gm, w_in, wvt, rope_s, tm=dec_b)
            att = _attn_sample(q, k, v, cache_k, cache_v, e, page_table, lam_p, sw,
                               n_heads=n_heads_a, e_a=e_a, lam_init=lam_init)
            hist = [state_sconv[e, :, j] for j in range(sconv - 1)]
            xs = _even_out(att, bg, cx, xs, sconv_w[e], w_out, tm=dec_b, hist=hist)
            ks.append(k.reshape(dec_b, 1, n_heads_a, e_a))
            vs.append(v.reshape(dec_b, 1, n_heads_a, e_a))
            scs.append(jnp.concatenate([state_sconv[e, :, 1:], cx[:, None, :]], axis=1))
        else:
            o = l // 2
            w_in = w_in_odd[o]
            wqkv = w_in[:, :qkv_c].astype(BF16)
            wz = w_in[:, qkv_c:qkv_c + kdim].astype(BF16)
            wba = jnp.pad(w_in[:, qkv_c + kdim:], ((0, 0), (0, LANES - 2 * n_heads_c))).astype(BF16)
            w_out = w_out_odd[o].astype(BF16)
            lane_pad = lambda a: jnp.pad(a, (n_heads_c, LANES - 2 * n_heads_c)).reshape(1, LANES)
            al_pad, dtb_pad = lane_pad(a_log[o]), lane_pad(dt_bias[o])
            nw = gdn_norm_w[o].reshape(1, dk)
            proj = functools.partial(_gdn_proj, n_heads=n_heads_c, dk=dk)
            q, k, v, z, bga, tail = proj(xp, gm, wqkv, wz, wba, gdn_conv_w[o], al_pad, dtb_pad,
                                         tm=tm_p, seq=seq)
            u, w, qe, ke, qk, gc = _gdn_prep(q, k, v, bga, n_heads=n_heads_c, dk=dk,
                                             n_sub=_tile(seq // GDN_CHUNK, 4))
            og, s_new = _gdn_scan(u, w, qe, ke, qk, gc, batch=batch, seq=seq, n_heads=n_heads_c, dk=dk,
                                  rows=scan_rows, nbat=_tile(batch, 4))
            xp = _gdn_out(og, z, xp, nw, w_out, tm=tm_big, n_heads=n_heads_c, dk=dk)
            gcp.append(tail[:, HIST_ROWS - (gdn_conv - 1):])
            gsp.append(s_new)
            hist_arr = state_gdn_conv[o].reshape(dec_b, (gdn_conv - 1) * qkv_c)
            q, k, v, z, bga, raw = proj(xs, gm, wqkv, wz, wba, gdn_conv_w[o], al_pad, dtb_pad,
                                        tm=dec_b, hist=(hist_arr, gdn_conv - 1))
            og, s_new = _gdn_step(q, k, v, bga, state_gdn, o, n_heads=n_heads_c, dk=dk, nb=_tile(dec_b, 8))
            xs = _gdn_out(og, z, xs, nw, w_out, tm=dec_b, n_heads=n_heads_c, dk=dk)
            gcs.append(jnp.concatenate([state_gdn_conv[o, :, 1:], raw[:, None, :]], axis=1))
            gss.append(s_new)
        gn = norm_ffn[l].reshape(1, d)
        wg = w_up[l, :, :d_ff].astype(BF16)
        wu = w_up[l, :, d_ff:].astype(BF16)
        wd = w_down[l].astype(BF16)
        cb = ffn_conv_b[l].reshape(1, d_ff)
        last = l == depth - 1
        xp, tail = _ffn(xp, gn, wg, wu, ffn_conv_w[l], cb, wd, gf, tm=tm_big, final_norm=last, seq=seq)
        fcp.append(tail[:, HIST_ROWS - (ffn_conv - 1):])
        hist_arr = state_ffn_conv[l].reshape(dec_b, (ffn_conv - 1) * d_ff)
        xs, raw = _ffn(xs, gn, wg, wu, ffn_conv_w[l], cb, wd, gf, tm=dec_b, final_norm=last,
                       hist=(hist_arr, ffn_conv - 1))
        fcs.append(jnp.concatenate([state_ffn_conv[l, :, 1:], raw[:, None, :]], axis=1))
    return (xp.reshape(batch, seq, d), xs.reshape(dec_b, 1, d),
            jnp.stack(kp), jnp.stack(vp), jnp.stack(ks), jnp.stack(vs),
            jnp.stack(scp), jnp.stack(scs),
            jnp.stack(gcp), jnp.stack(gcs), jnp.stack(gsp), jnp.stack(gss),
            jnp.stack(fcp), jnp.stack(fcs))
```

```python
import functools
import math

import jax
import jax.numpy as jnp
from jax import lax
from jax.experimental import pallas as pl
from jax.experimental.pallas import tpu as pltpu

F32 = jnp.float32
BF16 = jnp.bfloat16

EPS = 1e-6
SUBLN_EPS = 1e-5
ROPE_THETA = 500000.0
GDN_CHUNK = 64
V7X_VMEM_LIMIT = 56 * 1024 * 1024
LANES = 128
HIST_ROWS = 8
BF16_ROWS = 16
LOG2E = math.log2(math.e)
VT_PAD = 16
ATTN_HEADS = 4
ATTN_SUB = 2


def _cparams(*sem):
    return pltpu.CompilerParams(dimension_semantics=sem, vmem_limit_bytes=V7X_VMEM_LIMIT)


def _const_spec(shape):
    nd = len(shape)
    return pl.BlockSpec(shape, lambda i: (0,) * nd, pipeline_mode=pl.Buffered(1))


def _rms(x, g, eps):
    return x * lax.rsqrt(jnp.mean(x * x, axis=-1, keepdims=True) + eps) * g


def _sigmoid(x):
    return 0.5 * jnp.tanh(0.5 * x) + 0.5


def _silu(x):
    h = 0.5 * x
    return h * jnp.tanh(h) + h


def _softplus(x):
    return jnp.maximum(x, 0.0) + jnp.log1p(jnp.exp(-jnp.abs(x)))


def _dot(a, b):
    return jnp.dot(a, b, preferred_element_type=F32)


def _dot_nt(a, b):
    return lax.dot_general(a, b, (((1,), (1,)), ((), ())), preferred_element_type=F32)


def _dot_tn(a, b):
    return lax.dot_general(a, b, (((0,), (0,)), ((), ())), preferred_element_type=F32)


def _split3(a):
    hi = a.astype(BF16)
    r = a - hi.astype(F32)
    mid = r.astype(BF16)
    lo = (r - mid.astype(F32)).astype(BF16)
    return hi, mid, lo


def _even_proj_kernel(x_ref, g_ref, w_ref, wvt_ref, c_ref, s1_ref, s2_ref,
                      q_ref, k_ref, v_ref, kb_ref, vt_ref, bg_ref, cx_ref,
                      *, tm, n_heads, e_a, w_b, half, qscale):
    xn = _rms(x_ref[...], g_ref[...], EPS).astype(BF16)
    n = n_heads * e_a
    c, s1, s2 = c_ref[...], s1_ref[...], s2_ref[...]

    def rope(t):
        return t * c + pltpu.roll(t, e_a - half, 1) * s1 + pltpu.roll(t, half, 1) * s2

    hqk = _dot(xn, w_ref[:, :2 * n])
    for h in range(n_heads):
        sl = slice(h * e_a, (h + 1) * e_a)
        q_ref[:, sl] = (rope(hqk[:, sl]) * qscale).astype(BF16)
        kr = rope(hqk[:, n + h * e_a:n + (h + 1) * e_a])
        k_ref[pl.ds(h, tm, stride=n_heads), :] = kr
        kb_ref[:, sl] = kr.astype(BF16)
    hv = _dot(xn, w_ref[:, 2 * n:3 * n])
    for h in range(n_heads):
        v_ref[pl.ds(h, tm, stride=n_heads), :] = hv[:, h * e_a:(h + 1) * e_a]
    hvt = _dot_nt(wvt_ref[...], xn).astype(BF16)
    ones_row = jnp.where(lax.broadcasted_iota(jnp.int32, (VT_PAD, tm), 0) == 0, 1.0, 0.0).astype(BF16)
    for h in range(n_heads):
        r0 = h * (e_a + VT_PAD)
        vt_ref[r0:r0 + e_a, :] = hvt[h * e_a:(h + 1) * e_a, :]
        vt_ref[r0 + e_a:r0 + e_a + VT_PAD, :] = ones_row
    o = 3 * n
    bg_ref[...] = _dot(xn, w_ref[:, o:o + w_b])
    hc = _dot(xn, w_ref[:, o + w_b:o + 3 * w_b])
    cx_ref[...] = hc[:, :w_b] * hc[:, w_b:]


def _even_proj(x, g, w, wvt, rope_tabs, *, tm, n_heads, e_a, w_b, half):
    t, d = x.shape
    n = n_heads * e_a
    c, s1, s2 = rope_tabs
    row = lambda width: pl.BlockSpec((tm, width), lambda i: (i, 0))
    tab_tiles = c.shape[0] // tm
    n_vt = n_heads * (e_a + VT_PAD)
    tab = pl.BlockSpec((tm, e_a), lambda i: (i % tab_tiles, 0))
    tok = pl.BlockSpec((tm * n_heads, e_a), lambda i: (i, 0))
    kern = functools.partial(_even_proj_kernel, tm=tm, n_heads=n_heads, e_a=e_a, w_b=w_b, half=half,
                             qscale=(e_a // 2) ** -0.5 * LOG2E)
    return pl.pallas_call(
        kern,
        grid=(t // tm,),
        in_specs=[row(d), _const_spec((1, d)), _const_spec(w.shape), _const_spec(wvt.shape),
                  tab, tab, tab],
        out_specs=[row(n), tok, tok, row(n), pl.BlockSpec((n_vt, tm), lambda i: (0, i)),
                   row(w_b), row(w_b)],
        out_shape=[jax.ShapeDtypeStruct((t, n), BF16), jax.ShapeDtypeStruct((t * n_heads, e_a), F32),
                   jax.ShapeDtypeStruct((t * n_heads, e_a), F32), jax.ShapeDtypeStruct((t, n), BF16),
                   jax.ShapeDtypeStruct((n_vt, t), BF16), jax.ShapeDtypeStruct((t, w_b), F32),
                   jax.ShapeDtypeStruct((t, w_b), F32)],
        compiler_params=_cparams("arbitrary"),
        name="even_proj",
    )(x, g, w, wvt, c, s1, s2)


def _rope_tables(pos, e_a, rot, dtype=F32):
    half = rot // 2
    dh = e_a // 2
    inv = ROPE_THETA ** (-jnp.arange(half, dtype=F32) * 2.0 / rot)
    ang = pos.astype(F32)[:, None] * inv[None, :]
    cos, sin = jnp.cos(ang).astype(dtype), jnp.sin(ang).astype(dtype)
    t = pos.shape[0]
    ones = jnp.ones((t, dh - rot), dtype)
    zeros_h = jnp.zeros((t, half), dtype)
    zeros_r = jnp.zeros((t, dh - rot), dtype)
    c = jnp.concatenate([cos, cos, ones], axis=1)
    s1 = jnp.concatenate([-sin, zeros_h, zeros_r], axis=1)
    s2 = jnp.concatenate([zeros_h, sin, zeros_r], axis=1)
    rep = lambda a: jnp.concatenate([a, a], axis=1)
    return rep(c), rep(s1), rep(s2)


def _lambda(lp_ref, lam_init):
    lp = lp_ref[...]
    a = jnp.sum(lp[0:1] * lp[1:2], axis=-1, keepdims=True)
    b = jnp.sum(lp[2:3] * lp[3:4], axis=-1, keepdims=True)
    return jnp.exp(a) - jnp.exp(b) + lam_init


def _attn_prompt_kernel(q_ref, k_ref, vt_ref, lp_ref, swc_ref, o_ref, m_ref, acc_ref,
                        *, blk, n_sub, nh, dh, lam_init):
    i = pl.program_id(2)
    e_a = 2 * dh
    vr = vt_ref.shape[0] // nh
    sub = blk // n_sub
    feat = lax.broadcasted_iota(jnp.int32, (e_a, blk), 0)
    qs = []
    for hh in range(nh):
        qt = q_ref[:, hh * e_a:(hh + 1) * e_a].astype(F32).T
        qs.append(jnp.concatenate([jnp.where(feat < dh, qt, 0.0), jnp.where(feat >= dh, qt, 0.0)],
                                  axis=1).astype(BF16))

    m_ref[...] = jnp.full(m_ref.shape, -1e30, F32)
    acc_ref[...] = jnp.zeros(acc_ref.shape, F32)

    def step(off, diag):
        def scores(hh, t):
            s = _dot(k_ref[pl.ds(off + t * sub, sub), hh * e_a:(hh + 1) * e_a], qs[hh])
            if diag:
                key = lax.broadcasted_iota(jnp.int32, s.shape, 0) + t * sub
                qry = lax.broadcasted_iota(jnp.int32, s.shape, 1)
                qry = jnp.where(qry >= blk, qry - blk, qry)
                s = jnp.where(key <= qry, s, jnp.finfo(F32).min)
            return s

        ss = [[scores(hh, t) for t in range(n_sub)] for hh in range(nh)]
        for hh in range(nh):
            for t in range(n_sub):
                m = m_ref[hh]
                m_new = jnp.maximum(m, jnp.max(ss[hh][t], axis=0, keepdims=True))
                p = jnp.exp2(ss[hh][t] - m_new)
                acc_ref[hh] = jnp.exp2(m - m_new) * acc_ref[hh] + _dot(
                    vt_ref[hh * vr:(hh + 1) * vr, pl.ds(off + t * sub, sub)], p.astype(BF16))
                m_ref[hh] = m_new

    def body(j, carry):
        step(pl.multiple_of(j * blk, blk), False)
        return carry

    lax.fori_loop(0, i, body, 0)
    step(pl.multiple_of(i * blk, blk), True)
    lam = _lambda(lp_ref, lam_init)
    for hh in range(nh):
        acc = acc_ref[hh]
        o = acc[:e_a] / acc[e_a:e_a + 1]
        att = o[:, :blk] - lam * o[:, blk:]
        y = att * lax.rsqrt(jnp.mean(att * att, axis=0, keepdims=True) + SUBLN_EPS) * swc_ref[...]
        o_ref[:, hh * e_a:(hh + 1) * e_a] = (y * (1.0 - lam_init)).T.astype(o_ref.dtype)


def _attn_prompt(q, kb, vt, lam_p, subln_w, *, batch, seq, n_heads, e_a, blk, lam_init):
    t = q.shape[0]
    nq = seq // blk
    swc = subln_w.reshape(e_a, 1)
    n_sub = ATTN_SUB if blk % (ATTN_SUB * LANES) == 0 else 1
    nh = ATTN_HEADS if n_heads % ATTN_HEADS == 0 else 1
    kern = functools.partial(_attn_prompt_kernel, blk=blk, n_sub=n_sub, nh=nh, dh=e_a // 2, lam_init=lam_init)
    return pl.pallas_call(
        kern,
        grid=(batch, n_heads // nh, nq),
        in_specs=[pl.BlockSpec((blk, nh * e_a), lambda b, h, i: (b * nq + i, h)),
                  pl.BlockSpec((seq, nh * e_a), lambda b, h, i: (b, h)),
                  pl.BlockSpec((nh * (e_a + VT_PAD), seq), lambda b, h, i: (h, b)),
                  pl.BlockSpec(lam_p.shape, lambda b, h, i: (0, 0)),
                  pl.BlockSpec(swc.shape, lambda b, h, i: (0, 0))],
        out_specs=pl.BlockSpec((blk, nh * e_a), lambda b, h, i: (b * nq + i, h)),
        out_shape=jax.ShapeDtypeStruct((t, n_heads * e_a), BF16),
        scratch_shapes=[pltpu.VMEM((nh, 1, 2 * blk), F32), pltpu.VMEM((nh, e_a + VT_PAD, 2 * blk), F32)],
        compiler_params=_cparams("arbitrary", "arbitrary", "arbitrary"),
        name="attn_prompt",
    )(q, kb, vt, lam_p, swc)


def _attn_sample_kernel(pt_ref, q_ref, kn_ref, vn_ref, lp_ref, sw_ref, *rest,
                        n_pages, n_heads, e_a, lam_init):
    del pt_ref
    k_refs = rest[:n_pages]
    v_refs = rest[n_pages:2 * n_pages]
    o_ref = rest[2 * n_pages]
    dh = e_a // 2
    nm = max(2 * n_heads, BF16_ROWS)
    rows_pp = k_refs[0].shape[1]
    rowi = lax.broadcasted_iota(jnp.int32, (nm, e_a), 0)
    lane = lax.broadcasted_iota(jnp.int32, (nm, e_a), 1)

    def per_head_rows(ref):
        t = ref[0]
        out = jnp.zeros((nm, e_a), F32)
        for h in range(n_heads):
            out = jnp.where((rowi >> 1) == h, jnp.broadcast_to(t[h:h + 1].astype(F32), (nm, e_a)), out)
        return out

    qm_f = jnp.where((lane // dh) == (rowi & 1), per_head_rows(q_ref), 0.0)
    qm = qm_f.astype(BF16)
    colh = lax.broadcasted_iota(jnp.int32, (nm, rows_pp), 1) % n_heads
    valid = colh == (lax.broadcasted_iota(jnp.int32, (nm, rows_pp), 0) >> 1)
    ss = [jnp.where(valid, _dot_nt(qm, k_refs[p][0].astype(BF16)), jnp.finfo(F32).min)
          for p in range(n_pages)]
    kn = per_head_rows(kn_ref).astype(BF16).astype(F32)
    vn = per_head_rows(vn_ref).astype(BF16).astype(F32)
    s_new = jnp.sum(qm.astype(F32) * kn, axis=-1, keepdims=True)
    m = s_new
    for s in ss:
        m = jnp.maximum(m, jnp.max(s, axis=-1, keepdims=True))
    p_new = jnp.exp2(s_new - m)
    l = p_new
    acc = p_new.astype(BF16).astype(F32) * vn
    for p in range(n_pages):
        pr = jnp.exp2(ss[p] - m)
        l = l + jnp.sum(pr, axis=-1, keepdims=True)
        acc = acc + _dot(pr.astype(BF16), v_refs[p][0].astype(BF16))
    o = acc / l
    lam = _lambda(lp_ref, lam_init)
    sw = sw_ref[...]
    for h in range(n_heads):
        att = o[2 * h:2 * h + 1] - lam * o[2 * h + 1:2 * h + 2]
        o_ref[0, h:h + 1, :] = (_rms(att, sw, SUBLN_EPS) * (1.0 - lam_init)).astype(o_ref.dtype)


def _attn_sample(q, k_new, v_new, cache_k, cache_v, layer, page_table, lam_p, subln_w,
                 *, n_heads, e_a, lam_init):
    b, n = q.shape
    n_pages = page_table.shape[1]
    n_phys, page = cache_k.shape[1], cache_k.shape[2]
    ck = cache_k.reshape(cache_k.shape[0] * n_phys, page * n_heads, e_a)
    cv = cache_v.reshape(cache_v.shape[0] * n_phys, page * n_heads, e_a)
    base = layer * n_phys
    tok = pl.BlockSpec((1, n_heads, e_a), lambda i, pt: (i, 0, 0))
    page_specs = [pl.BlockSpec((1, page * n_heads, e_a), functools.partial(
        lambda i, pt, p: (base + pt[i, p], 0, 0), p=p)) for p in range(n_pages)]
    kern = functools.partial(_attn_sample_kernel, n_pages=n_pages, n_heads=n_heads, e_a=e_a,
                             lam_init=lam_init)
    grid_spec = pltpu.PrefetchScalarGridSpec(
        num_scalar_prefetch=1,
        grid=(b,),
        in_specs=[tok, tok, tok,
                  pl.BlockSpec(lam_p.shape, lambda i, pt: (0, 0)),
                  pl.BlockSpec(subln_w.shape, lambda i, pt: (0, 0))] + page_specs + page_specs,
        out_specs=tok,
    )
    tok3 = lambda a: a.reshape(b, n_heads, e_a)
    out = pl.pallas_call(
        kern,
        grid_spec=grid_spec,
        out_shape=jax.ShapeDtypeStruct((b, n_heads, e_a), F32),
        compiler_params=_cparams("arbitrary"),
        name="attn_sample",
    )(page_table, tok3(q).astype(F32), tok3(k_new), tok3(v_new), lam_p, subln_w,
      *([ck] * n_pages), *([cv] * n_pages))
    return out.reshape(b, n)


def _seq_hist(buf_ref, cur, tiles_per_seq, tm):
    first = (pl.program_id(0) % tiles_per_seq) == 0

    @pl.when(first)
    def _():
        buf_ref[0:HIST_ROWS, :] = jnp.zeros((HIST_ROWS, buf_ref.shape[1]), buf_ref.dtype)

    @pl.when(jnp.logical_not(first))
    def _():
        buf_ref[0:HIST_ROWS, :] = buf_ref[tm:tm + HIST_ROWS, :]

    buf_ref[HIST_ROWS:HIST_ROWS + tm, :] = cur


def _even_tail_seq(att_ref, bg_ref, cx_ref, cw_ref, w_ref, buf_ref, *, tm, tiles_per_seq):
    n_att = att_ref.shape[1]
    cx = cx_ref[...]
    _seq_hist(buf_ref, cx, tiles_per_seq, tm)
    cw = cw_ref[...]
    width = cw.shape[0]
    conv = cw[width - 1:width] * cx
    for j in range(width - 1):
        conv = conv + cw[j:j + 1] * buf_ref[pl.ds(HIST_ROWS - (width - 1) + j, tm), :]
    yb = (bg_ref[...] * conv).astype(BF16)
    return _dot(att_ref[...].astype(BF16), w_ref[:n_att, :]) + _dot(yb, w_ref[n_att:, :])


def _even_out_step_kernel(att_ref, bg_ref, cx_ref, x_ref, cw_ref, w_ref, *rest, n_att):
    hist_refs, o_ref = rest[:-1], rest[-1]
    cw = cw_ref[...]
    width = cw.shape[0]
    conv = cw[width - 1:width] * cx_ref[...]
    for j in range(width - 1):
        conv = conv + cw[j:j + 1] * hist_refs[j][...]
    yb = (bg_ref[...] * conv).astype(BF16)
    o_ref[...] = (x_ref[...] + _dot(att_ref[...].astype(BF16), w_ref[:n_att, :])
                  + _dot(yb, w_ref[n_att:, :]))


def _even_out_step(att, bg, cx, x, conv_w, w_out, hist, *, tm):
    t, d = x.shape
    n_att, w_b = att.shape[1], bg.shape[1]
    row = lambda width: pl.BlockSpec((tm, width), lambda i: (i, 0))
    return pl.pallas_call(
        functools.partial(_even_out_step_kernel, n_att=n_att),
        grid=(t // tm,),
        in_specs=[row(n_att), row(w_b), row(w_b), row(d), _const_spec(conv_w.shape), _const_spec(w_out.shape)]
                 + [row(w_b)] * len(hist),
        out_specs=row(d),
        out_shape=jax.ShapeDtypeStruct((t, d), F32),
        compiler_params=_cparams("arbitrary"),
        name="even_out",
    )(att, bg, cx, x, conv_w, w_out, *hist)


def _gdn_gates(xn, wz_ref, wba_ref, al_ref, dtb_ref, z_ref, bga_ref, *, n_heads):
    z_ref[...] = _dot(xn, wz_ref[...]).astype(z_ref.dtype)
    ba = _dot(xn, wba_ref[...])
    beta = _sigmoid(ba)
    g = -jnp.exp(al_ref[...]) * _softplus(ba + dtb_ref[...])
    lane = lax.broadcasted_iota(jnp.int32, ba.shape, 1)
    bga_ref[...] = jnp.where(lane < n_heads, beta, g)


def _gdn_qkv_chunk(conv, c0, q_ref, k_ref, v_ref, *, n_heads, dk):
    kdim = n_heads * dk
    act = _silu(conv)
    for i in range(conv.shape[1] // dk):
        t = act[:, i * dk:(i + 1) * dk]
        ch = c0 + i * dk
        if ch < 2 * kdim:
            t = t * lax.rsqrt(jnp.sum(t * t, axis=-1, keepdims=True) + EPS)
        if ch < kdim:
            q_ref[:, ch:ch + dk] = t * (dk ** -0.5)
        elif ch < 2 * kdim:
            k_ref[:, ch - kdim:ch - kdim + dk] = t
        else:
            v_ref[:, ch - 2 * kdim:ch - 2 * kdim + dk] = t


def _gdn_proj_seq_kernel(x_ref, g_ref, wqkv_ref, wz_ref, wba_ref, cw_ref, al_ref, dtb_ref,
                         q_ref, k_ref, v_ref, z_ref, bga_ref, tail_ref, buf_ref,
                         *, tm, tiles_per_seq, n_heads, dk, cwid):
    xn = _rms(x_ref[...], g_ref[...], EPS).astype(BF16)
    first = (pl.program_id(0) % tiles_per_seq) == 0

    @pl.when(first)
    def _():
        buf_ref[0:HIST_ROWS, :] = jnp.zeros((HIST_ROWS, buf_ref.shape[1]), buf_ref.dtype)

    @pl.when(jnp.logical_not(first))
    def _():
        buf_ref[0:HIST_ROWS, :] = buf_ref[tm:tm + HIST_ROWS, :]

    cw = cw_ref[...]
    width = cw.shape[0]
    n_chunks = wqkv_ref.shape[1] // cwid
    mm = lambda c: _dot(xn, wqkv_ref[:, c * cwid:(c + 1) * cwid])
    h_next = mm(0)
    for c in range(n_chunks):
        cs = slice(c * cwid, (c + 1) * cwid)
        hm = h_next
        if c + 1 < n_chunks:
            h_next = mm(c + 1)
        buf_ref[HIST_ROWS:HIST_ROWS + tm, cs] = hm
        conv = cw[width - 1:width, cs] * hm
        for j in range(width - 1):
            conv = conv + cw[j:j + 1, cs] * buf_ref[pl.ds(HIST_ROWS - (width - 1) + j, tm), cs]
        _gdn_qkv_chunk(conv, c * cwid, q_ref, k_ref, v_ref, n_heads=n_heads, dk=dk)
    tail_ref[0] = buf_ref[tm:tm + HIST_ROWS, :]
    _gdn_gates(xn, wz_ref, wba_ref, al_ref, dtb_ref, z_ref, bga_ref, n_heads=n_heads)


def _gdn_proj_step_kernel(x_ref, g_ref, wqkv_ref, wz_ref, wba_ref, cw_ref, al_ref, dtb_ref, *rest,
                          n_heads, dk, n_hist):
    hist_refs = rest[:n_hist]
    q_ref, k_ref, v_ref, z_ref, bga_ref, raw_ref = rest[n_hist:]
    xn = _rms(x_ref[...], g_ref[...], EPS).astype(BF16)
    hm = _dot(xn, wqkv_ref[...])
    raw_ref[...] = hm
    cw = cw_ref[...]
    width = cw.shape[0]
    conv = cw[width - 1:width] * hm
    for j in range(width - 1):
        conv = conv + cw[j:j + 1] * hist_refs[j][...]
    _gdn_qkv_chunk(conv, 0, q_ref, k_ref, v_ref, n_heads=n_heads, dk=dk)
    _gdn_gates(xn, wz_ref, wba_ref, al_ref, dtb_ref, z_ref, bga_ref, n_heads=n_heads)


def _gdn_proj(x, g, wqkv, wz, wba, conv_w, al_pad, dtb_pad, *, tm, n_heads, dk, seq=None, hist=None):
    t, d = x.shape
    qkv_c = wqkv.shape[1]
    vdim = wz.shape[1]
    row = lambda width: pl.BlockSpec((tm, width), lambda i: (i, 0))
    common = [row(d), _const_spec((1, d)), _const_spec(wqkv.shape), _const_spec(wz.shape),
              _const_spec(wba.shape), _const_spec(conv_w.shape), _const_spec(al_pad.shape),
              _const_spec(dtb_pad.shape)]
    outs = [row(vdim), row(vdim), row(vdim), row(vdim), row(LANES)]
    out_shapes = ([jax.ShapeDtypeStruct((t, vdim), F32)] * 3 + [jax.ShapeDtypeStruct((t, vdim), BF16)]
                  + [jax.ShapeDtypeStruct((t, LANES), F32)])
    if hist is None:
        tps = seq // tm
        kern = functools.partial(_gdn_proj_seq_kernel, tm=tm, tiles_per_seq=tps, n_heads=n_heads, dk=dk,
                                 cwid=2 * dk)
        extra, extra_specs = [], []
        outs.append(pl.BlockSpec((1, HIST_ROWS, qkv_c), lambda i: (i // tps, 0, 0)))
        out_shapes.append(jax.ShapeDtypeStruct((t // seq, HIST_ROWS, qkv_c), F32))
        scratch = [pltpu.VMEM((tm + HIST_ROWS, qkv_c), F32)]
    else:
        hist_arr, n_hist = hist
        kern = functools.partial(_gdn_proj_step_kernel, n_heads=n_heads, dk=dk, n_hist=n_hist)
        extra = [hist_arr] * n_hist
        extra_specs = [pl.BlockSpec((tm, qkv_c), functools.partial(lambda i, j: (i, j), j=j))
                       for j in range(n_hist)]
        outs.append(row(qkv_c))
        out_shapes.append(jax.ShapeDtypeStruct((t, qkv_c), F32))
        scratch = []
    return pl.pallas_call(
        kern,
        grid=(t // tm,),
        in_specs=common + extra_specs,
        out_specs=outs,
        out_shape=out_shapes,
        scratch_shapes=scratch,
        compiler_params=_cparams("arbitrary"),
        name="gdn_proj",
    )(x, g, wqkv, wz, wba, conv_w, al_pad, dtb_pad, *extra)


def _pair_blockdiag(a, b):
    z = jnp.zeros_like(a)
    return jnp.concatenate([jnp.concatenate([a, z], axis=1), jnp.concatenate([z, b], axis=1)], axis=0)


def _gdn_prep_kernel(q_ref, k_ref, v_ref, bga_ref, u_ref, w_ref, qe_ref, ke_ref, qk_ref, gc_ref,
                     *, n_heads, dk, chunk, n_sub):
    n_pairs = n_heads // 2
    rows_blk = chunk * n_sub
    bga = bga_ref[...]
    r = lax.broadcasted_iota(jnp.int32, (rows_blk, rows_blk), 0)
    c = lax.broadcasted_iota(jnp.int32, (rows_blk, rows_blk), 1)
    sh = chunk.bit_length() - 1
    tril = jnp.where(jnp.logical_and(r >= c, (r >> sh) == (c >> sh)), 1.0, 0.0).astype(BF16)
    b3 = _split3(bga)
    gc = _dot(tril, b3[0]) + (_dot(tril, b3[1]) + _dot(tril, b3[2]))
    gc_ref[...] = gc
    lane = lax.broadcasted_iota(jnp.int32, (rows_blk, LANES), 1)
    is_g = jnp.logical_and(lane >= n_heads, lane < 2 * n_heads)
    odd = ((lane - n_heads) & 1) == 1
    gce = jnp.where(jnp.logical_and(is_g, jnp.logical_not(odd)), gc, 0.0)
    gco = jnp.where(jnp.logical_and(is_g, odd), gc, 0.0)
    ar = lax.broadcasted_iota(jnp.int32, (BF16_ROWS, LANES), 0)
    al = lax.broadcasted_iota(jnp.int32, (BF16_ROWS, LANES), 1)
    asel = jnp.where(jnp.logical_and(jnp.logical_and(al >= n_heads, ar < n_pairs),
                                     ((al - n_heads) >> 1) == ar), 1.0, 0.0).astype(BF16)
    prow = lax.broadcasted_iota(jnp.int32, (chunk, 2 * chunk), 0)
    plane = lax.broadcasted_iota(jnp.int32, (chunk, 2 * chunk), 1)
    pcol = plane & (chunk - 1)
    lo_half = plane < chunk
    incl = prow >= pcol
    strict = prow > pcol

    def bd(m):
        return jnp.concatenate([jnp.where(lo_half, m, 0.0), jnp.where(lo_half, 0.0, m)], axis=0).astype(BF16)

    items = [(ci, p) for ci in range(n_sub) for p in range(n_pairs)]
    lmat, tinv, vbn, kbe = {}, {}, {}, {}
    for ci, p in items:
        rs = slice(ci * chunk, (ci + 1) * chunk)
        cs = slice(2 * p * dk, 2 * (p + 1) * dk)
        g3 = _split3(jnp.concatenate([gce[rs], gco[rs]], axis=0))
        grow = _dot_nt(asel, g3[0]) + (_dot_nt(asel, g3[1]) + _dot_nt(asel, g3[2]))
        k2, q2, v2 = k_ref[rs, cs], q_ref[rs, cs], v_ref[rs, cs]
        ha, hb = 2 * p, 2 * p + 1

        def nat(a, b, rows=chunk):
            return jnp.concatenate([jnp.broadcast_to(a, (rows, dk)), jnp.broadcast_to(b, (rows, dk))], axis=1)

        gca, gcb = gc[rs, n_heads + ha:n_heads + ha + 1], gc[rs, n_heads + hb:n_heads + hb + 1]
        bnat = nat(bga[rs, ha:ha + 1], bga[rs, hb:hb + 1])
        gnat = nat(gca, gcb)
        glast = nat(gca[chunk - 1:chunk], gcb[chunk - 1:chunk], rows=1)
        egn = jnp.exp(gnat)
        kb2 = k2 * bnat
        k2b = k2.astype(BF16)
        rhs_t = _pair_blockdiag(k2b[:, :dk], k2b[:, dk:])
        decay = jnp.where(incl, jnp.exp(jnp.where(incl, jnp.where(lo_half, gca, gcb) - grow[p:p + 1], 0.0)), 0.0)
        lm = jnp.where(strict, _dot_nt(kb2.astype(BF16), rhs_t) * decay, 0.0)
        qk_ref[rs, 2 * p * chunk:2 * (p + 1) * chunk] = jnp.where(
            incl, _dot_nt(q2.astype(BF16), rhs_t) * decay, 0.0).astype(BF16)
        qe_ref[rs, cs] = (q2 * egn).astype(BF16)
        ke_ref[rs, cs] = (k2 * jnp.exp(glast - gnat)).astype(BF16)
        lmat[ci, p] = lm
        tinv[ci, p] = jnp.where(prow == pcol, 1.0, 0.0) - jnp.where((prow >> 1) == (pcol >> 1), lm, 0.0)
        vbn[ci, p] = (v2 * bnat).astype(BF16)
        kbe[ci, p] = (kb2 * egn).astype(BF16)
    s = 1
    while 2 * (1 << s) <= chunk:
        joins = jnp.logical_and((prow >> (s + 1)) == (pcol >> (s + 1)), (prow >> s) != (pcol >> s))
        x = {it: _dot(tinv[it].astype(BF16), bd(jnp.where(joins, lmat[it], 0.0))) for it in items}
        tinv = {it: tinv[it] - _dot(x[it].astype(BF16), bd(tinv[it])) for it in items}
        s += 1
    for ci, p in items:
        rs = slice(ci * chunk, (ci + 1) * chunk)
        cs = slice(2 * p * dk, 2 * (p + 1) * dk)
        tb = tinv[ci, p].astype(BF16)
        u_ref[rs, cs] = _dot(tb, _pair_blockdiag(vbn[ci, p][:, :dk], vbn[ci, p][:, dk:]))
        w_ref[rs, cs] = _dot(tb, _pair_blockdiag(kbe[ci, p][:, :dk], kbe[ci, p][:, dk:])).astype(BF16)


def _gdn_prep(q, k, v, bga, *, n_heads, dk, n_sub):
    t, vdim = q.shape
    chunk = GDN_CHUNK
    assert 2 * chunk == LANES and n_heads % 2 == 0 and 2 * n_heads <= LANES
    rows = chunk * n_sub
    qk_w = n_heads * chunk
    row = lambda width: pl.BlockSpec((rows, width), lambda i: (i, 0))
    kern = functools.partial(_gdn_prep_kernel, n_heads=n_heads, dk=dk, chunk=chunk, n_sub=n_sub)
    return pl.pallas_call(
        kern,
        grid=(t // rows,),
        in_specs=[row(vdim), row(vdim), row(vdim), row(LANES)],
        out_specs=[row(vdim), row(vdim), row(vdim), row(vdim), row(qk_w), row(LANES)],
        out_shape=[jax.ShapeDtypeStruct((t, vdim), F32)] + [jax.ShapeDtypeStruct((t, vdim), BF16)] * 3
                  + [jax.ShapeDtypeStruct((t, qk_w), BF16), jax.ShapeDtypeStruct((t, LANES), F32)],
        compiler_params=_cparams("arbitrary"),
        name="gdn_prep",
    )(q, k, v, bga)


def _gdn_scan_kernel(u_ref, w_ref, qe_ref, ke_ref, qk_ref, gc_ref, o_ref, sout_ref, s_ref,
                     *, n_heads, dk, chunk, n_sub, nbat):
    j = pl.program_id(1)

    @pl.when(j == 0)
    def _():
        s_ref[...] = jnp.zeros(s_ref.shape, s_ref.dtype)

    sls = [slice(h * dk, (h + 1) * dk) for h in range(n_heads)]
    items = [(bi, h) for bi in range(nbat) for h in range(n_heads)]

    def body(c, carry):
        r0 = pl.multiple_of(c * chunk, chunk)
        rows = pl.ds(r0, chunk)
        st = {it: s_ref[it[0], it[1]] for it in items}
        sb = {it: st[it].astype(BF16) for it in items}
        ws = {(bi, h): _dot(jnp.concatenate([w_ref[bi, rows, sls[h]], qe_ref[bi, rows, sls[h]]], axis=0),
                            sb[bi, h]) for bi, h in items}
        vb = {(bi, h): (u_ref[bi, rows, sls[h]] - ws[bi, h][:chunk]).astype(BF16) for bi, h in items}
        for bi in range(nbat):
            for p in range(n_heads // 2):
                ha, hb = 2 * p, 2 * p + 1
                intra = _dot(qk_ref[bi, rows, 2 * p * chunk:2 * (p + 1) * chunk],
                             _pair_blockdiag(vb[bi, ha], vb[bi, hb]))
                o_ref[bi, rows, 2 * p * dk:2 * (p + 1) * dk] = (
                    jnp.concatenate([ws[bi, ha][chunk:], ws[bi, hb][chunk:]], axis=1) + intra)
        for bi in range(nbat):
            eg = jnp.exp(gc_ref[bi, pl.ds(r0 + chunk - 1, 1), :])
            for h in range(n_heads):
                s_ref[bi, h] = (st[bi, h] * eg[:, n_heads + h:n_heads + h + 1]
                                + _dot_tn(ke_ref[bi, rows, sls[h]], vb[bi, h]))
        return carry

    lax.fori_loop(0, n_sub, body, 0)

    @pl.when(j == pl.num_programs(1) - 1)
    def _():
        sout_ref[...] = s_ref[...]


def _gdn_scan(u, w, qe, ke, qk, gc, *, batch, seq, n_heads, dk, rows, nbat):
    t, vdim = u.shape
    chunk = GDN_CHUNK
    seq3 = lambda a: a.reshape(batch, seq, a.shape[1])
    row = lambda width: pl.BlockSpec((nbat, rows, width), lambda b, j: (b, j, 0))
    kern = functools.partial(_gdn_scan_kernel, n_heads=n_heads, dk=dk, chunk=chunk, n_sub=rows // chunk,
                             nbat=nbat)
    o, s_new = pl.pallas_call(
        kern,
        grid=(batch // nbat, seq // rows),
        in_specs=[row(vdim)] * 4 + [row(qk.shape[1]), row(LANES)],
        out_specs=[row(vdim), pl.BlockSpec((nbat, n_heads, dk, dk), lambda b, j: (b, 0, 0, 0))],
        out_shape=[jax.ShapeDtypeStruct((batch, seq, vdim), F32),
                   jax.ShapeDtypeStruct((batch, n_heads, dk, dk), F32)],
        scratch_shapes=[pltpu.VMEM((nbat, n_heads, dk, dk), F32)],
        compiler_params=_cparams("arbitrary", "arbitrary"),
        name="gdn_scan",
    )(seq3(u), seq3(w), seq3(qe), seq3(ke), seq3(qk), seq3(gc))
    return o.reshape(t, vdim), s_new


def _gdn_step_kernel(q_ref, k_ref, v_ref, bga_ref, s_ref, o_ref, sout_ref, *, n_heads, dk, nb):
    s_ref = s_ref.at[0]
    pad = jnp.zeros((dk - n_heads, dk), F32)

    def body(b, carry):
        qt = jnp.concatenate([q_ref[b], pad], axis=0).T
        kt = jnp.concatenate([k_ref[b], pad], axis=0).T
        vb = v_ref[b]
        bg = bga_ref[pl.ds(b, 1), :]
        eg = jnp.exp(bg)
        rows = []
        for h in range(n_heads):
            st = s_ref[b, h] * eg[:, n_heads + h:n_heads + h + 1]
            kc = kt[:, h:h + 1]
            ks = jnp.sum(kc * st, axis=0, keepdims=True)
            delta = (vb[h:h + 1, :] - ks) * bg[:, h:h + 1]
            st = st + kc * delta
            sout_ref[b, h] = st
            rows.append(jnp.sum(qt[:, h:h + 1] * st, axis=0, keepdims=True))
        o_ref[b] = jnp.concatenate(rows, axis=0)
        return carry

    for b in range(nb):
        body(b, 0)


def _gdn_step(q, k, v, bga, states, layer, *, n_heads, dk, nb):
    b = q.shape[0]
    q3, k3, v3 = (a.reshape(b, n_heads, dk) for a in (q, k, v))
    vec = pl.BlockSpec((nb, n_heads, dk), lambda i: (i, 0, 0))
    st = pl.BlockSpec((nb, n_heads, dk, dk), lambda i: (i, 0, 0, 0))
    kern = functools.partial(_gdn_step_kernel, n_heads=n_heads, dk=dk, nb=nb)
    o, s_new = pl.pallas_call(
        kern,
        grid=(b // nb,),
        in_specs=[vec, vec, vec, pl.BlockSpec((nb, LANES), lambda i: (i, 0)),
                  pl.BlockSpec((1, nb, n_heads, dk, dk), lambda i: (layer, i, 0, 0, 0))],
        out_specs=[vec, st],
        out_shape=[jax.ShapeDtypeStruct((b, n_heads, dk), F32), jax.ShapeDtypeStruct(states.shape[1:], F32)],
        compiler_params=_cparams("arbitrary"),
        name="gdn_step",
    )(q3, k3, v3, bga, states)
    return o.reshape(b, n_heads * dk), s_new


def _gdn_tail(o_ref, z_ref, nw_ref, w_ref, *, n_heads, dk):
    nw = nw_ref[...]
    parts = []
    for h in range(n_heads):
        sl = slice(h * dk, (h + 1) * dk)
        parts.append((_rms(o_ref[:, sl], nw, EPS) * _silu(z_ref[:, sl].astype(F32))).astype(BF16))
    return _dot(jnp.concatenate(parts, axis=1), w_ref[...])


def _gdn_out_kernel(o_ref, z_ref, x_ref, nw_ref, w_ref, y_ref, *, n_heads, dk):
    y_ref[...] = x_ref[...] + _gdn_tail(o_ref, z_ref, nw_ref, w_ref, n_heads=n_heads, dk=dk)


def _gdn_out(o, z, x, norm_w, w_out, *, tm, n_heads, dk):
    t, d = x.shape
    vdim = o.shape[1]
    row = lambda width: pl.BlockSpec((tm, width), lambda i: (i, 0))
    kern = functools.partial(_gdn_out_kernel, n_heads=n_heads, dk=dk)
    return pl.pallas_call(
        kern,
        grid=(t // tm,),
        in_specs=[row(vdim), row(vdim), row(d), _const_spec(norm_w.shape), _const_spec(w_out.shape)],
        out_specs=row(d),
        out_shape=jax.ShapeDtypeStruct((t, d), F32),
        compiler_params=_cparams("arbitrary"),
        name="gdn_out",
    )(o, z, x, norm_w, w_out)


def _ffn_tail(x, conv, up, cb_ref, wd_ref, gf_ref, o_ref, final_norm):
    act = (_silu(conv + cb_ref[...]) * up).astype(BF16)
    y = x + _dot(act, wd_ref[...])
    if final_norm:
        y = _rms(y, gf_ref[...], EPS)
    o_ref[...] = y


def _ffn_seq_kernel(x_ref, g_ref, wg_ref, wu_ref, cw_ref, cb_ref, wd_ref, gf_ref, *rest,
                    tm, tiles_per_seq, final_norm, cwid, mix):
    x = x_ref[...]
    if mix[0] == "gdn":
        o_in, z_in, nw_ref, wo_ref, o_ref, tail_ref, buf_ref, act_ref = rest
        x = x + _gdn_tail(o_in, z_in, nw_ref, wo_ref, n_heads=mix[1], dk=mix[2])
    else:
        att_in, bg_in, cx_in, scw_ref, wo_ref, o_ref, tail_ref, buf_ref, act_ref, cbuf_ref = rest
        x = x + _even_tail_seq(att_in, bg_in, cx_in, scw_ref, wo_ref, cbuf_ref, tm=tm,
                               tiles_per_seq=tiles_per_seq)
    xn = _rms(x, g_ref[...], EPS).astype(BF16)
    first = (pl.program_id(0) % tiles_per_seq) == 0

    @pl.when(first)
    def _():
        buf_ref[0:HIST_ROWS, :] = jnp.zeros((HIST_ROWS, buf_ref.shape[1]), buf_ref.dtype)

    @pl.when(jnp.logical_not(first))
    def _():
        buf_ref[0:HIST_ROWS, :] = buf_ref[tm:tm + HIST_ROWS, :]

    cw = cw_ref[...]
    cb = cb_ref[...]
    width = cw.shape[0]
    n_chunks = wg_ref.shape[1] // cwid

    def mm(c):
        cs = slice(c * cwid, (c + 1) * cwid)
        return _dot(xn, wg_ref[:, cs]), _dot(xn, wu_ref[:, cs])

    nxt = mm(0)
    for c in range(n_chunks):
        cs = slice(c * cwid, (c + 1) * cwid)
        gate, up = nxt
        if c + 1 < n_chunks:
            nxt = mm(c + 1)
        buf_ref[HIST_ROWS:HIST_ROWS + tm, cs] = gate
        conv = cw[width - 1:width, cs] * gate
        for j in range(width - 1):
            conv = conv + cw[j:j + 1, cs] * buf_ref[pl.ds(HIST_ROWS - (width - 1) + j, tm), cs]
        act_ref[:, cs] = (_silu(conv + cb[:, cs]) * up).astype(BF16)
    tail_ref[0] = buf_ref[tm:tm + HIST_ROWS, :]
    y = x + _dot(act_ref[...], wd_ref[...])
    if final_norm:
        y = _rms(y, gf_ref[...], EPS)
    o_ref[...] = y


def _ffn_step_kernel(x_ref, g_ref, wg_ref, wu_ref, cw_ref, cb_ref, wd_ref, gf_ref, *rest,
                     n_hist, final_norm):
    hist_refs = rest[:n_hist]
    o_ref, raw_ref = rest[n_hist:]
    x = x_ref[...]
    xn = _rms(x, g_ref[...], EPS).astype(BF16)
    gate = _dot(xn, wg_ref[...])
    raw_ref[...] = gate
    cw = cw_ref[...]
    width = cw.shape[0]
    conv = cw[width - 1:width] * gate
    for j in range(width - 1):
        conv = conv + cw[j:j + 1] * hist_refs[j][...]
    _ffn_tail(x, conv, _dot(xn, wu_ref[...]), cb_ref, wd_ref, gf_ref, o_ref, final_norm)


def _ffn(x, g, wg, wu, conv_w, conv_b, wd, gf, *, tm, final_norm, seq=None, hist=None, mix=None):
    t, d = x.shape
    d_ff = wg.shape[1]
    row = lambda width: pl.BlockSpec((tm, width), lambda i: (i, 0))
    common = [row(d), _const_spec((1, d)), _const_spec(wg.shape), _const_spec(wu.shape),
              _const_spec(conv_w.shape), _const_spec(conv_b.shape), _const_spec(wd.shape),
              _const_spec((1, d))]
    if hist is None:
        tps = seq // tm
        assert d_ff % (2 * LANES) == 0
        scratch = [pltpu.VMEM((tm + HIST_ROWS, d_ff), F32), pltpu.VMEM((tm, d_ff), BF16)]
        if mix[0] == "gdn":
            _, o_in, z_in, nw, w_out, n_heads, dk = mix
            extra = [o_in, z_in, nw, w_out]
            extra_specs = [row(o_in.shape[1]), row(z_in.shape[1]), _const_spec(nw.shape), _const_spec(w_out.shape)]
            mix_static = ("gdn", n_heads, dk)
        else:
            _, att, bg, cx, scw, w_out = mix
            extra = [att, bg, cx, scw, w_out]
            extra_specs = [row(att.shape[1]), row(bg.shape[1]), row(cx.shape[1]), _const_spec(scw.shape),
                           _const_spec(w_out.shape)]
            scratch.append(pltpu.VMEM((tm + HIST_ROWS, cx.shape[1]), F32))
            mix_static = ("even",)
        kern = functools.partial(_ffn_seq_kernel, tm=tm, tiles_per_seq=tps, final_norm=final_norm,
                                 cwid=2 * LANES, mix=mix_static)
        outs = [row(d), pl.BlockSpec((1, HIST_ROWS, d_ff), lambda i: (i // tps, 0, 0))]
        out_shapes = [jax.ShapeDtypeStruct((t, d), F32),
                      jax.ShapeDtypeStruct((t // seq, HIST_ROWS, d_ff), F32)]
    else:
        assert mix is None
        hist_arr, n_hist = hist
        kern = functools.partial(_ffn_step_kernel, n_hist=n_hist, final_norm=final_norm)
        extra = [hist_arr] * n_hist
        extra_specs = [pl.BlockSpec((tm, d_ff), functools.partial(lambda i, j: (i, j), j=j))
                       for j in range(n_hist)]
        outs = [row(d), row(d_ff)]
        out_shapes = [jax.ShapeDtypeStruct((t, d), F32), jax.ShapeDtypeStruct((t, d_ff), F32)]
        scratch = []
    return pl.pallas_call(
        kern,
        grid=(t // tm,),
        in_specs=common + extra_specs,
        out_specs=outs,
        out_shape=out_shapes,
        scratch_shapes=scratch,
        compiler_params=_cparams("arbitrary"),
        name="conv_ffn",
    )(x, g, wg, wu, conv_w, conv_b, wd, gf, *extra)


def _tile(n, pref):
    t = min(n, pref)
    while n % t:
        t //= 2
    return t


def kernel(x_prompt, x_sample, cache_k, cache_v, state_sconv, state_gdn_conv, state_gdn, state_ffn_conv, page_table, norm_mix, norm_ffn, norm_final, w_in_even, w_out_even, lambda_q1, lambda_k1, lambda_q2, lambda_k2, subln_w, sconv_w, w_in_odd, gdn_conv_w, a_log, dt_bias, gdn_norm_w, w_out_odd, w_up, ffn_conv_w, ffn_conv_b, w_down):
    batch, seq, d = x_prompt.shape
    dec_b = x_sample.shape[0]
    depth = norm_mix.shape[0]
    n_heads_a, e_a = cache_k.shape[3], cache_k.shape[4]
    n_att = n_heads_a * e_a
    rot = (e_a // 2) // 4
    w_b = state_sconv.shape[-1]
    sconv = sconv_w.shape[1]
    page = cache_k.shape[2]
    past_len = page_table.shape[1] * page
    n_heads_c, dk = state_gdn.shape[2], state_gdn.shape[3]
    kdim = n_heads_c * dk
    qkv_c = gdn_conv_w.shape[-1]
    gdn_conv = gdn_conv_w.shape[1]
    d_ff = ffn_conv_w.shape[-1]
    ffn_conv = ffn_conv_w.shape[1]
    t_p = batch * seq

    tm_p = _tile(seq, 256)
    tm_big = _tile(seq, 512)
    blk = _tile(seq, 512)
    scan_rows = _tile(seq, 256)

    xp = x_prompt.reshape(t_p, d)
    xs = x_sample.reshape(dec_b, d)
    rope_p = _rope_tables(jnp.arange(seq), e_a, rot)
    rope_s = _rope_tables(jnp.full((dec_b,), past_len), e_a, rot)
    gf = norm_final.reshape(1, d)

    kp, vp, ks, vs, scp, scs = [], [], [], [], [], []
    gcp, gcs, gsp, gss, fcp, fcs = [], [], [], [], [], []
    for l in range(depth):
        gm = norm_mix[l].reshape(1, d)
        if l % 2 == 0:
            e = l // 2
            lam_init = 0.8 - 0.6 * math.exp(-0.3 * l)
            w_in = w_in_even[e].astype(BF16)
            wvt = w_in_even[e][:, 2 * n_att:3 * n_att].T.astype(BF16)
            w_out = w_out_even[e].astype(BF16)
            lam_p = jnp.stack([lambda_q1[e], lambda_k1[e], lambda_q2[e], lambda_k2[e]])
            sw = subln_w[e].reshape(1, e_a)
            proj = functools.partial(_even_proj, n_heads=n_heads_a, e_a=e_a, w_b=w_b, half=rot // 2)
            q, k, v, kb, vt, bg, cx = proj(xp, gm, w_in, wvt, rope_p, tm=tm_p)
            att = _attn_prompt(q, kb, vt, lam_p, sw, batch=batch, seq=seq, n_heads=n_heads_a, e_a=e_a,
                               blk=blk, lam_init=lam_init)
            mix_p = ("even", att, bg, cx, sconv_w[e], w_out)
            kp.append(k.reshape(batch, seq, n_heads_a, e_a))
            vp.append(v.reshape(batch, seq, n_heads_a, e_a))
            cx3 = cx.reshape(batch, seq, w_b)
            if seq >= sconv - 1:
                scp.append(cx3[:, seq - (sconv - 1):])
            else:
                scp.append(jnp.concatenate([jnp.zeros((batch, sconv - 1 - seq, w_b), F32), cx3], axis=1))
            q, k, v, kb, vt, bg, cx = proj(xs, gm, w_in, wvt, rope_s, tm=dec_b)
            att = _attn_sample(q, k, v, cache_k, cache_v, e, page_table, lam_p, sw,
                               n_heads=n_heads_a, e_a=e_a, lam_init=lam_init)
            hist = [state_sconv[e, :, j] for j in range(sconv - 1)]
            xs = _even_out_step(att, bg, cx, xs, sconv_w[e], w_out, hist, tm=dec_b)
            ks.append(k.reshape(dec_b, 1, n_heads_a, e_a))
            vs.append(v.reshape(dec_b, 1, n_heads_a, e_a))
            scs.append(jnp.concatenate([state_sconv[e, :, 1:], cx[:, None, :]], axis=1))
        else:
            o = l // 2
            w_in = w_in_odd[o]
            wqkv = w_in[:, :qkv_c].astype(BF16)
            wz = w_in[:, qkv_c:qkv_c + kdim].astype(BF16)
            wba = jnp.pad(w_in[:, qkv_c + kdim:], ((0, 0), (0, LANES - 2 * n_heads_c))).astype(BF16)
            w_out = w_out_odd[o].astype(BF16)
            lane_pad = lambda a: jnp.pad(a, (n_heads_c, LANES - 2 * n_heads_c)).reshape(1, LANES)
            al_pad, dtb_pad = lane_pad(a_log[o]), lane_pad(dt_bias[o])
            nw = gdn_norm_w[o].reshape(1, dk)
            proj = functools.partial(_gdn_proj, n_heads=n_heads_c, dk=dk)
            q, k, v, z, bga, tail = proj(xp, gm, wqkv, wz, wba, gdn_conv_w[o], al_pad, dtb_pad,
                                         tm=tm_p, seq=seq)
            u, w, qe, ke, qk, gc = _gdn_prep(q, k, v, bga, n_heads=n_heads_c, dk=dk,
                                             n_sub=_tile(seq // GDN_CHUNK, 4))
            og, s_new = _gdn_scan(u, w, qe, ke, qk, gc, batch=batch, seq=seq, n_heads=n_heads_c, dk=dk,
                                  rows=scan_rows, nbat=_tile(batch, 4))
            mix_p = ("gdn", og, z, nw, w_out, n_heads_c, dk)
            gcp.append(tail[:, HIST_ROWS - (gdn_conv - 1):])
            gsp.append(s_new)
            hist_arr = state_gdn_conv[o].reshape(dec_b, (gdn_conv - 1) * qkv_c)
            q, k, v, z, bga, raw = proj(xs, gm, wqkv, wz, wba, gdn_conv_w[o], al_pad, dtb_pad,
                                        tm=dec_b, hist=(hist_arr, gdn_conv - 1))
            og, s_new = _gdn_step(q, k, v, bga, state_gdn, o, n_heads=n_heads_c, dk=dk, nb=_tile(dec_b, 8))
            xs = _gdn_out(og, z, xs, nw, w_out, tm=dec_b, n_heads=n_heads_c, dk=dk)
            gcs.append(jnp.concatenate([state_gdn_conv[o, :, 1:], raw[:, None, :]], axis=1))
            gss.append(s_new)
        gn = norm_ffn[l].reshape(1, d)
        wg = w_up[l, :, :d_ff].astype(BF16)
        wu = w_up[l, :, d_ff:].astype(BF16)
        wd = w_down[l].astype(BF16)
        cb = ffn_conv_b[l].reshape(1, d_ff)
        last = l == depth - 1
        xp, tail = _ffn(xp, gn, wg, wu, ffn_conv_w[l], cb, wd, gf, tm=tm_big, final_norm=last, seq=seq,
                        mix=mix_p)
        fcp.append(tail[:, HIST_ROWS - (ffn_conv - 1):])
        hist_arr = state_ffn_conv[l].reshape(dec_b, (ffn_conv - 1) * d_ff)
        xs, raw = _ffn(xs, gn, wg, wu, ffn_conv_w[l], cb, wd, gf, tm=dec_b, final_norm=last,
                       hist=(hist_arr, ffn_conv - 1))
        fcs.append(jnp.concatenate([state_ffn_conv[l, :, 1:], raw[:, None, :]], axis=1))
    return (xp.reshape(batch, seq, d), xs.reshape(dec_b, 1, d),
            jnp.stack(kp), jnp.stack(vp), jnp.stack(ks), jnp.stack(vs),
            jnp.stack(scp), jnp.stack(scs),
            jnp.stack(gcp), jnp.stack(gcs), jnp.stack(gsp), jnp.stack(gss),
            jnp.stack(fcp), jnp.stack(fcs))
```

```python
import functools
import math

import jax
import jax.numpy as jnp
from jax import lax
from jax.experimental import pallas as pl
from jax.experimental.pallas import tpu as pltpu

F32 = jnp.float32
BF16 = jnp.bfloat16

EPS = 1e-6
SUBLN_EPS = 1e-5
ROPE_THETA = 500000.0
GDN_CHUNK = 64
V7X_VMEM_LIMIT = 56 * 1024 * 1024
LANES = 128
HIST_ROWS = 8
BF16_ROWS = 16
LOG2E = math.log2(math.e)
VT_PAD = 16
ATTN_HEADS = 4
ATTN_SUB = 2


def _cparams(*sem):
    return pltpu.CompilerParams(dimension_semantics=sem, vmem_limit_bytes=V7X_VMEM_LIMIT)


def _const_spec(shape):
    nd = len(shape)
    return pl.BlockSpec(shape, lambda i: (0,) * nd, pipeline_mode=pl.Buffered(1))


def _rms(x, g, eps):
    return x * lax.rsqrt(jnp.mean(x * x, axis=-1, keepdims=True) + eps) * g


def _sigmoid(x):
    return 0.5 * jnp.tanh(0.5 * x) + 0.5


def _silu(x):
    h = 0.5 * x
    return h * jnp.tanh(h) + h


def _softplus(x):
    return jnp.maximum(x, 0.0) + jnp.log1p(jnp.exp(-jnp.abs(x)))


def _dot(a, b):
    return jnp.dot(a, b, preferred_element_type=F32)


def _dot_nt(a, b):
    return lax.dot_general(a, b, (((1,), (1,)), ((), ())), preferred_element_type=F32)


def _dot_tn(a, b):
    return lax.dot_general(a, b, (((0,), (0,)), ((), ())), preferred_element_type=F32)


def _split3(a):
    hi = a.astype(BF16)
    r = a - hi.astype(F32)
    mid = r.astype(BF16)
    lo = (r - mid.astype(F32)).astype(BF16)
    return hi, mid, lo


def _even_proj_kernel(x_ref, g_ref, w_ref, c_ref, s1_ref, s2_ref,
                      q_ref, k_ref, v_ref, kb_ref, vt_ref, bg_ref, cx_ref,
                      *, tm, n_heads, e_a, w_b, half, qscale):
    xn = _rms(x_ref[...], g_ref[...], EPS).astype(BF16)
    n = n_heads * e_a
    c, s1, s2 = c_ref[...], s1_ref[...], s2_ref[...]

    def rope(t):
        return t * c + pltpu.roll(t, e_a - half, 1) * s1 + pltpu.roll(t, half, 1) * s2

    hqk = _dot(xn, w_ref[:, :2 * n])
    for h in range(n_heads):
        sl = slice(h * e_a, (h + 1) * e_a)
        q_ref[:, sl] = (rope(hqk[:, sl]) * qscale).astype(BF16)
        kr = rope(hqk[:, n + h * e_a:n + (h + 1) * e_a])
        k_ref[pl.ds(h, tm, stride=n_heads), :] = kr
        kb_ref[:, sl] = kr.astype(BF16)
    hv = _dot(xn, w_ref[:, 2 * n:3 * n])
    for h in range(n_heads):
        v_ref[pl.ds(h, tm, stride=n_heads), :] = hv[:, h * e_a:(h + 1) * e_a]
    hvt = hv.T.astype(BF16)
    ones_row = jnp.where(lax.broadcasted_iota(jnp.int32, (VT_PAD, tm), 0) == 0, 1.0, 0.0).astype(BF16)
    for h in range(n_heads):
        r0 = h * (e_a + VT_PAD)
        vt_ref[r0:r0 + e_a, :] = hvt[h * e_a:(h + 1) * e_a, :]
        vt_ref[r0 + e_a:r0 + e_a + VT_PAD, :] = ones_row
    o = 3 * n
    bg_ref[...] = _dot(xn, w_ref[:, o:o + w_b])
    hc = _dot(xn, w_ref[:, o + w_b:o + 3 * w_b])
    cx_ref[...] = hc[:, :w_b] * hc[:, w_b:]


def _even_proj(x, g, w, rope_tabs, *, tm, n_heads, e_a, w_b, half):
    t, d = x.shape
    n = n_heads * e_a
    c, s1, s2 = rope_tabs
    row = lambda width: pl.BlockSpec((tm, width), lambda i: (i, 0))
    tab_tiles = c.shape[0] // tm
    n_vt = n_heads * (e_a + VT_PAD)
    tab = pl.BlockSpec((tm, e_a), lambda i: (i % tab_tiles, 0))
    tok = pl.BlockSpec((tm * n_heads, e_a), lambda i: (i, 0))
    kern = functools.partial(_even_proj_kernel, tm=tm, n_heads=n_heads, e_a=e_a, w_b=w_b, half=half,
                             qscale=(e_a // 2) ** -0.5 * LOG2E)
    return pl.pallas_call(
        kern,
        grid=(t // tm,),
        in_specs=[row(d), _const_spec((1, d)), _const_spec(w.shape), tab, tab, tab],
        out_specs=[row(n), tok, tok, row(n), pl.BlockSpec((n_vt, tm), lambda i: (0, i)),
                   row(w_b), row(w_b)],
        out_shape=[jax.ShapeDtypeStruct((t, n), BF16), jax.ShapeDtypeStruct((t * n_heads, e_a), F32),
                   jax.ShapeDtypeStruct((t * n_heads, e_a), F32), jax.ShapeDtypeStruct((t, n), BF16),
                   jax.ShapeDtypeStruct((n_vt, t), BF16), jax.ShapeDtypeStruct((t, w_b), F32),
                   jax.ShapeDtypeStruct((t, w_b), F32)],
        compiler_params=_cparams("arbitrary"),
        name="even_proj",
    )(x, g, w, c, s1, s2)


def _rope_tables(pos, e_a, rot, dtype=F32):
    half = rot // 2
    dh = e_a // 2
    inv = ROPE_THETA ** (-jnp.arange(half, dtype=F32) * 2.0 / rot)
    ang = pos.astype(F32)[:, None] * inv[None, :]
    cos, sin = jnp.cos(ang).astype(dtype), jnp.sin(ang).astype(dtype)
    t = pos.shape[0]
    ones = jnp.ones((t, dh - rot), dtype)
    zeros_h = jnp.zeros((t, half), dtype)
    zeros_r = jnp.zeros((t, dh - rot), dtype)
    c = jnp.concatenate([cos, cos, ones], axis=1)
    s1 = jnp.concatenate([-sin, zeros_h, zeros_r], axis=1)
    s2 = jnp.concatenate([zeros_h, sin, zeros_r], axis=1)
    rep = lambda a: jnp.concatenate([a, a], axis=1)
    return rep(c), rep(s1), rep(s2)


def _lambda(lp_ref, lam_init):
    lp = lp_ref[...]
    a = jnp.sum(lp[0:1] * lp[1:2], axis=-1, keepdims=True)
    b = jnp.sum(lp[2:3] * lp[3:4], axis=-1, keepdims=True)
    return jnp.exp(a) - jnp.exp(b) + lam_init


def _attn_prompt_kernel(q_ref, k_ref, vt_ref, lp_ref, swc_ref, o_ref, m_ref, acc_ref,
                        *, blk, n_sub, nh, dh, lam_init):
    i = pl.program_id(2)
    e_a = 2 * dh
    vr = vt_ref.shape[0] // nh
    sub = blk // n_sub
    feat = lax.broadcasted_iota(jnp.int32, (e_a, blk), 0)
    qs = []
    for hh in range(nh):
        qt = q_ref[:, hh * e_a:(hh + 1) * e_a].astype(F32).T
        qs.append(jnp.concatenate([jnp.where(feat < dh, qt, 0.0), jnp.where(feat >= dh, qt, 0.0)],
                                  axis=1).astype(BF16))

    m_ref[...] = jnp.full(m_ref.shape, -1e30, F32)
    acc_ref[...] = jnp.zeros(acc_ref.shape, F32)

    def step(off, diag):
        def scores(hh, t):
            s = _dot(k_ref[pl.ds(off + t * sub, sub), hh * e_a:(hh + 1) * e_a], qs[hh])
            if diag:
                key = lax.broadcasted_iota(jnp.int32, s.shape, 0) + t * sub
                qry = lax.broadcasted_iota(jnp.int32, s.shape, 1)
                qry = jnp.where(qry >= blk, qry - blk, qry)
                s = jnp.where(key <= qry, s, jnp.finfo(F32).min)
            return s

        ss = [[scores(hh, t) for t in range(n_sub)] for hh in range(nh)]
        for hh in range(nh):
            for t in range(n_sub):
                m = m_ref[hh]
                m_new = jnp.maximum(m, jnp.max(ss[hh][t], axis=0, keepdims=True))
                p = jnp.exp2(ss[hh][t] - m_new)
                acc_ref[hh] = jnp.exp2(m - m_new) * acc_ref[hh] + _dot(
                    vt_ref[hh * vr:(hh + 1) * vr, pl.ds(off + t * sub, sub)], p.astype(BF16))
                m_ref[hh] = m_new

    def body(j, carry):
        step(pl.multiple_of(j * blk, blk), False)
        return carry

    lax.fori_loop(0, i, body, 0)
    step(pl.multiple_of(i * blk, blk), True)
    lam = _lambda(lp_ref, lam_init)
    for hh in range(nh):
        acc = acc_ref[hh]
        o = acc[:e_a] / acc[e_a:e_a + 1]
        att = o[:, :blk] - lam * o[:, blk:]
        y = att * lax.rsqrt(jnp.mean(att * att, axis=0, keepdims=True) + SUBLN_EPS) * swc_ref[...]
        o_ref[:, hh * e_a:(hh + 1) * e_a] = (y * (1.0 - lam_init)).T.astype(o_ref.dtype)


def _attn_prompt(q, kb, vt, lam_p, subln_w, *, batch, seq, n_heads, e_a, blk, lam_init):
    t = q.shape[0]
    nq = seq // blk
    swc = subln_w.reshape(e_a, 1)
    n_sub = ATTN_SUB if blk % (ATTN_SUB * LANES) == 0 else 1
    nh = ATTN_HEADS if n_heads % ATTN_HEADS == 0 else 1
    kern = functools.partial(_attn_prompt_kernel, blk=blk, n_sub=n_sub, nh=nh, dh=e_a // 2, lam_init=lam_init)
    return pl.pallas_call(
        kern,
        grid=(batch, n_heads // nh, nq),
        in_specs=[pl.BlockSpec((blk, nh * e_a), lambda b, h, i: (b * nq + i, h)),
                  pl.BlockSpec((seq, nh * e_a), lambda b, h, i: (b, h)),
                  pl.BlockSpec((nh * (e_a + VT_PAD), seq), lambda b, h, i: (h, b)),
                  pl.BlockSpec(lam_p.shape, lambda b, h, i: (0, 0)),
                  pl.BlockSpec(swc.shape, lambda b, h, i: (0, 0))],
        out_specs=pl.BlockSpec((blk, nh * e_a), lambda b, h, i: (b * nq + i, h)),
        out_shape=jax.ShapeDtypeStruct((t, n_heads * e_a), BF16),
        scratch_shapes=[pltpu.VMEM((nh, 1, 2 * blk), F32), pltpu.VMEM((nh, e_a + VT_PAD, 2 * blk), F32)],
        compiler_params=_cparams("arbitrary", "arbitrary", "arbitrary"),
        name="attn_prompt",
    )(q, kb, vt, lam_p, swc)


def _attn_sample_kernel(pt_ref, q_ref, kn_ref, vn_ref, lp_ref, sw_ref, *rest,
                        n_pages, n_heads, e_a, lam_init):
    del pt_ref
    k_refs = rest[:n_pages]
    v_refs = rest[n_pages:2 * n_pages]
    o_ref = rest[2 * n_pages]
    dh = e_a // 2
    nm = max(2 * n_heads, BF16_ROWS)
    rows_pp = k_refs[0].shape[1]
    rowi = lax.broadcasted_iota(jnp.int32, (nm, e_a), 0)
    lane = lax.broadcasted_iota(jnp.int32, (nm, e_a), 1)

    def per_head_rows(ref):
        t = ref[0]
        out = jnp.zeros((nm, e_a), F32)
        for h in range(n_heads):
            out = jnp.where((rowi >> 1) == h, jnp.broadcast_to(t[h:h + 1].astype(F32), (nm, e_a)), out)
        return out

    qm_f = jnp.where((lane // dh) == (rowi & 1), per_head_rows(q_ref), 0.0)
    qm = qm_f.astype(BF16)
    colh = lax.broadcasted_iota(jnp.int32, (nm, rows_pp), 1) % n_heads
    valid = colh == (lax.broadcasted_iota(jnp.int32, (nm, rows_pp), 0) >> 1)
    ss = [jnp.where(valid, _dot_nt(qm, k_refs[p][0].astype(BF16)), jnp.finfo(F32).min)
          for p in range(n_pages)]
    kn = per_head_rows(kn_ref).astype(BF16).astype(F32)
    vn = per_head_rows(vn_ref).astype(BF16).astype(F32)
    s_new = jnp.sum(qm.astype(F32) * kn, axis=-1, keepdims=True)
    m = s_new
    for s in ss:
        m = jnp.maximum(m, jnp.max(s, axis=-1, keepdims=True))
    p_new = jnp.exp2(s_new - m)
    l = p_new
    acc = p_new.astype(BF16).astype(F32) * vn
    for p in range(n_pages):
        pr = jnp.exp2(ss[p] - m)
        l = l + jnp.sum(pr, axis=-1, keepdims=True)
        acc = acc + _dot(pr.astype(BF16), v_refs[p][0].astype(BF16))
    o = acc / l
    lam = _lambda(lp_ref, lam_init)
    sw = sw_ref[...]
    for h in range(n_heads):
        att = o[2 * h:2 * h + 1] - lam * o[2 * h + 1:2 * h + 2]
        o_ref[0, h:h + 1, :] = (_rms(att, sw, SUBLN_EPS) * (1.0 - lam_init)).astype(o_ref.dtype)


def _attn_sample(q, k_new, v_new, cache_k, cache_v, layer, page_table, lam_p, subln_w,
                 *, n_heads, e_a, lam_init):
    b, n = q.shape
    n_pages = page_table.shape[1]
    n_phys, page = cache_k.shape[1], cache_k.shape[2]
    ck = cache_k.reshape(cache_k.shape[0] * n_phys, page * n_heads, e_a)
    cv = cache_v.reshape(cache_v.shape[0] * n_phys, page * n_heads, e_a)
    base = layer * n_phys
    tok = pl.BlockSpec((1, n_heads, e_a), lambda i, pt: (i, 0, 0))
    page_specs = [pl.BlockSpec((1, page * n_heads, e_a), functools.partial(
        lambda i, pt, p: (base + pt[i, p], 0, 0), p=p)) for p in range(n_pages)]
    kern = functools.partial(_attn_sample_kernel, n_pages=n_pages, n_heads=n_heads, e_a=e_a,
                             lam_init=lam_init)
    grid_spec = pltpu.PrefetchScalarGridSpec(
        num_scalar_prefetch=1,
        grid=(b,),
        in_specs=[tok, tok, tok,
                  pl.BlockSpec(lam_p.shape, lambda i, pt: (0, 0)),
                  pl.BlockSpec(subln_w.shape, lambda i, pt: (0, 0))] + page_specs + page_specs,
        out_specs=tok,
    )
    tok3 = lambda a: a.reshape(b, n_heads, e_a)
    out = pl.pallas_call(
        kern,
        grid_spec=grid_spec,
        out_shape=jax.ShapeDtypeStruct((b, n_heads, e_a), F32),
        compiler_params=_cparams("arbitrary"),
        name="attn_sample",
    )(page_table, tok3(q).astype(F32), tok3(k_new), tok3(v_new), lam_p, subln_w,
      *([ck] * n_pages), *([cv] * n_pages))
    return out.reshape(b, n)


def _seq_hist(buf_ref, cur, tiles_per_seq, tm):
    first = (pl.program_id(0) % tiles_per_seq) == 0

    @pl.when(first)
    def _():
        buf_ref[0:HIST_ROWS, :] = jnp.zeros((HIST_ROWS, buf_ref.shape[1]), buf_ref.dtype)

    @pl.when(jnp.logical_not(first))
    def _():
        buf_ref[0:HIST_ROWS, :] = buf_ref[tm:tm + HIST_ROWS, :]

    buf_ref[HIST_ROWS:HIST_ROWS + tm, :] = cur


def _even_tail_seq(att_ref, bg_ref, cx_ref, cw_ref, w_ref, buf_ref, *, tm, tiles_per_seq):
    n_att = att_ref.shape[1]
    cx = cx_ref[...]
    _seq_hist(buf_ref, cx, tiles_per_seq, tm)
    cw = cw_ref[...]
    width = cw.shape[0]
    conv = cw[width - 1:width] * cx
    for j in range(width - 1):
        conv = conv + cw[j:j + 1] * buf_ref[pl.ds(HIST_ROWS - (width - 1) + j, tm), :]
    yb = (bg_ref[...] * conv).astype(BF16)
    return _dot(att_ref[...].astype(BF16), w_ref[:n_att, :]) + _dot(yb, w_ref[n_att:, :])


def _even_out_step_kernel(att_ref, bg_ref, cx_ref, x_ref, cw_ref, w_ref, *rest, n_att):
    hist_refs, o_ref = rest[:-1], rest[-1]
    cw = cw_ref[...]
    width = cw.shape[0]
    conv = cw[width - 1:width] * cx_ref[...]
    for j in range(width - 1):
        conv = conv + cw[j:j + 1] * hist_refs[j][...]
    yb = (bg_ref[...] * conv).astype(BF16)
    o_ref[...] = (x_ref[...] + _dot(att_ref[...].astype(BF16), w_ref[:n_att, :])
                  + _dot(yb, w_ref[n_att:, :]))


def _even_out_step(att, bg, cx, x, conv_w, w_out, hist, *, tm):
    t, d = x.shape
    n_att, w_b = att.shape[1], bg.shape[1]
    row = lambda width: pl.BlockSpec((tm, width), lambda i: (i, 0))
    return pl.pallas_call(
        functools.partial(_even_out_step_kernel, n_att=n_att),
        grid=(t // tm,),
        in_specs=[row(n_att), row(w_b), row(w_b), row(d), _const_spec(conv_w.shape), _const_spec(w_out.shape)]
                 + [row(w_b)] * len(hist),
        out_specs=row(d),
        out_shape=jax.ShapeDtypeStruct((t, d), F32),
        compiler_params=_cparams("arbitrary"),
        name="even_out",
    )(att, bg, cx, x, conv_w, w_out, *hist)


def _gdn_gates(xn, wz_ref, wba_ref, al_ref, dtb_ref, z_ref, bga_ref, *, n_heads):
    z_ref[...] = _dot(xn, wz_ref[...]).astype(z_ref.dtype)
    ba = _dot(xn, wba_ref[...])
    beta = _sigmoid(ba)
    g = -jnp.exp(al_ref[...]) * _softplus(ba + dtb_ref[...])
    lane = lax.broadcasted_iota(jnp.int32, ba.shape, 1)
    bga_ref[...] = jnp.where(lane < n_heads, beta, g)


def _gdn_qkv_chunk(conv, c0, q_ref, k_ref, v_ref, *, n_heads, dk):
    kdim = n_heads * dk
    act = _silu(conv)
    for i in range(conv.shape[1] // dk):
        t = act[:, i * dk:(i + 1) * dk]
        ch = c0 + i * dk
        if ch < 2 * kdim:
            t = t * lax.rsqrt(jnp.sum(t * t, axis=-1, keepdims=True) + EPS)
        if ch < kdim:
            q_ref[:, ch:ch + dk] = t * (dk ** -0.5)
        elif ch < 2 * kdim:
            k_ref[:, ch - kdim:ch - kdim + dk] = t
        else:
            v_ref[:, ch - 2 * kdim:ch - 2 * kdim + dk] = t


def _gdn_proj_seq_kernel(x_ref, g_ref, wqkv_ref, wz_ref, wba_ref, cw_ref, al_ref, dtb_ref,
                         q_ref, k_ref, v_ref, z_ref, bga_ref, tail_ref, buf_ref,
                         *, tm, tiles_per_seq, n_heads, dk, cwid):
    xn = _rms(x_ref[...], g_ref[...], EPS).astype(BF16)
    first = (pl.program_id(0) % tiles_per_seq) == 0

    @pl.when(first)
    def _():
        buf_ref[0:HIST_ROWS, :] = jnp.zeros((HIST_ROWS, buf_ref.shape[1]), buf_ref.dtype)

    @pl.when(jnp.logical_not(first))
    def _():
        buf_ref[0:HIST_ROWS, :] = buf_ref[tm:tm + HIST_ROWS, :]

    cw = cw_ref[...]
    width = cw.shape[0]
    n_chunks = wqkv_ref.shape[1] // cwid
    mm = lambda c: _dot(xn, wqkv_ref[:, c * cwid:(c + 1) * cwid])
    h_next = mm(0)
    for c in range(n_chunks):
        cs = slice(c * cwid, (c + 1) * cwid)
        hm = h_next
        if c + 1 < n_chunks:
            h_next = mm(c + 1)
        buf_ref[HIST_ROWS:HIST_ROWS + tm, cs] = hm
        conv = cw[width - 1:width, cs] * hm
        for j in range(width - 1):
            conv = conv + cw[j:j + 1, cs] * buf_ref[pl.ds(HIST_ROWS - (width - 1) + j, tm), cs]
        _gdn_qkv_chunk(conv, c * cwid, q_ref, k_ref, v_ref, n_heads=n_heads, dk=dk)
    tail_ref[0] = buf_ref[tm:tm + HIST_ROWS, :]
    _gdn_gates(xn, wz_ref, wba_ref, al_ref, dtb_ref, z_ref, bga_ref, n_heads=n_heads)


def _gdn_proj_step_kernel(x_ref, g_ref, wqkv_ref, wz_ref, wba_ref, cw_ref, al_ref, dtb_ref, *rest,
                          n_heads, dk, n_hist):
    hist_refs = rest[:n_hist]
    q_ref, k_ref, v_ref, z_ref, bga_ref, raw_ref = rest[n_hist:]
    xn = _rms(x_ref[...], g_ref[...], EPS).astype(BF16)
    hm = _dot(xn, wqkv_ref[...])
    raw_ref[...] = hm
    cw = cw_ref[...]
    width = cw.shape[0]
    conv = cw[width - 1:width] * hm
    for j in range(width - 1):
        conv = conv + cw[j:j + 1] * hist_refs[j][...]
    _gdn_qkv_chunk(conv, 0, q_ref, k_ref, v_ref, n_heads=n_heads, dk=dk)
    _gdn_gates(xn, wz_ref, wba_ref, al_ref, dtb_ref, z_ref, bga_ref, n_heads=n_heads)


def _gdn_proj(x, g, wqkv, wz, wba, conv_w, al_pad, dtb_pad, *, tm, n_heads, dk, seq=None, hist=None):
    t, d = x.shape
    qkv_c = wqkv.shape[1]
    vdim = wz.shape[1]
    row = lambda width: pl.BlockSpec((tm, width), lambda i: (i, 0))
    common = [row(d), _const_spec((1, d)), _const_spec(wqkv.shape), _const_spec(wz.shape),
              _const_spec(wba.shape), _const_spec(conv_w.shape), _const_spec(al_pad.shape),
              _const_spec(dtb_pad.shape)]
    outs = [row(vdim), row(vdim), row(vdim), row(vdim), row(LANES)]
    out_shapes = ([jax.ShapeDtypeStruct((t, vdim), F32)] * 3 + [jax.ShapeDtypeStruct((t, vdim), BF16)]
                  + [jax.ShapeDtypeStruct((t, LANES), F32)])
    if hist is None:
        tps = seq // tm
        kern = functools.partial(_gdn_proj_seq_kernel, tm=tm, tiles_per_seq=tps, n_heads=n_heads, dk=dk,
                                 cwid=2 * dk)
        extra, extra_specs = [], []
        outs.append(pl.BlockSpec((1, HIST_ROWS, qkv_c), lambda i: (i // tps, 0, 0)))
        out_shapes.append(jax.ShapeDtypeStruct((t // seq, HIST_ROWS, qkv_c), F32))
        scratch = [pltpu.VMEM((tm + HIST_ROWS, qkv_c), F32)]
    else:
        hist_arr, n_hist = hist
        kern = functools.partial(_gdn_proj_step_kernel, n_heads=n_heads, dk=dk, n_hist=n_hist)
        extra = [hist_arr] * n_hist
        extra_specs = [pl.BlockSpec((tm, qkv_c), functools.partial(lambda i, j: (i, j), j=j))
                       for j in range(n_hist)]
        outs.append(row(qkv_c))
        out_shapes.append(jax.ShapeDtypeStruct((t, qkv_c), F32))
        scratch = []
    return pl.pallas_call(
        kern,
        grid=(t // tm,),
        in_specs=common + extra_specs,
        out_specs=outs,
        out_shape=out_shapes,
        scratch_shapes=scratch,
        compiler_params=_cparams("arbitrary"),
        name="gdn_proj",
    )(x, g, wqkv, wz, wba, conv_w, al_pad, dtb_pad, *extra)


def _pair_blockdiag(a, b):
    z = jnp.zeros_like(a)
    return jnp.concatenate([jnp.concatenate([a, z], axis=1), jnp.concatenate([z, b], axis=1)], axis=0)


def _gdn_prep_kernel(q_ref, k_ref, v_ref, bga_ref, u_ref, w_ref, qe_ref, ke_ref, qk_ref, gc_ref,
                     *, n_heads, dk, chunk, n_sub):
    n_pairs = n_heads // 2
    rows_blk = chunk * n_sub
    bga = bga_ref[...]
    r = lax.broadcasted_iota(jnp.int32, (rows_blk, rows_blk), 0)
    c = lax.broadcasted_iota(jnp.int32, (rows_blk, rows_blk), 1)
    sh = chunk.bit_length() - 1
    tril = jnp.where(jnp.logical_and(r >= c, (r >> sh) == (c >> sh)), 1.0, 0.0).astype(BF16)
    b3 = _split3(bga)
    gc = _dot(tril, b3[0]) + (_dot(tril, b3[1]) + _dot(tril, b3[2]))
    gc_ref[...] = gc
    lane = lax.broadcasted_iota(jnp.int32, (rows_blk, LANES), 1)
    is_g = jnp.logical_and(lane >= n_heads, lane < 2 * n_heads)
    odd = ((lane - n_heads) & 1) == 1
    gce = jnp.where(jnp.logical_and(is_g, jnp.logical_not(odd)), gc, 0.0)
    gco = jnp.where(jnp.logical_and(is_g, odd), gc, 0.0)
    ar = lax.broadcasted_iota(jnp.int32, (BF16_ROWS, LANES), 0)
    al = lax.broadcasted_iota(jnp.int32, (BF16_ROWS, LANES), 1)
    asel = jnp.where(jnp.logical_and(jnp.logical_and(al >= n_heads, ar < n_pairs),
                                     ((al - n_heads) >> 1) == ar), 1.0, 0.0).astype(BF16)
    prow = lax.broadcasted_iota(jnp.int32, (chunk, 2 * chunk), 0)
    plane = lax.broadcasted_iota(jnp.int32, (chunk, 2 * chunk), 1)
    pcol = plane & (chunk - 1)
    lo_half = plane < chunk
    incl = prow >= pcol
    strict = prow > pcol

    def bd(m):
        return jnp.concatenate([jnp.where(lo_half, m, 0.0), jnp.where(lo_half, 0.0, m)], axis=0).astype(BF16)

    items = [(ci, p) for ci in range(n_sub) for p in range(n_pairs)]
    lmat, tinv, vbn, kbe = {}, {}, {}, {}
    for ci, p in items:
        rs = slice(ci * chunk, (ci + 1) * chunk)
        cs = slice(2 * p * dk, 2 * (p + 1) * dk)
        g3 = _split3(jnp.concatenate([gce[rs], gco[rs]], axis=0))
        grow = _dot_nt(asel, g3[0]) + (_dot_nt(asel, g3[1]) + _dot_nt(asel, g3[2]))
        k2, q2, v2 = k_ref[rs, cs], q_ref[rs, cs], v_ref[rs, cs]
        ha, hb = 2 * p, 2 * p + 1

        def nat(a, b, rows=chunk):
            return jnp.concatenate([jnp.broadcast_to(a, (rows, dk)), jnp.broadcast_to(b, (rows, dk))], axis=1)

        gca, gcb = gc[rs, n_heads + ha:n_heads + ha + 1], gc[rs, n_heads + hb:n_heads + hb + 1]
        bnat = nat(bga[rs, ha:ha + 1], bga[rs, hb:hb + 1])
        gnat = nat(gca, gcb)
        glast = nat(gca[chunk - 1:chunk], gcb[chunk - 1:chunk], rows=1)
        egn = jnp.exp(gnat)
        kb2 = k2 * bnat
        k2b = k2.astype(BF16)
        rhs_t = _pair_blockdiag(k2b[:, :dk], k2b[:, dk:])
        decay = jnp.where(incl, jnp.exp(jnp.where(incl, jnp.where(lo_half, gca, gcb) - grow[p:p + 1], 0.0)), 0.0)
        lm = jnp.where(strict, _dot_nt(kb2.astype(BF16), rhs_t) * decay, 0.0)
        qk_ref[rs, 2 * p * chunk:2 * (p + 1) * chunk] = jnp.where(
            incl, _dot_nt(q2.astype(BF16), rhs_t) * decay, 0.0).astype(BF16)
        qe_ref[rs, cs] = (q2 * egn).astype(BF16)
        ke_ref[rs, cs] = (k2 * jnp.exp(glast - gnat)).astype(BF16)
        lmat[ci, p] = lm
        tinv[ci, p] = jnp.where(prow == pcol, 1.0, 0.0) - jnp.where((prow >> 1) == (pcol >> 1), lm, 0.0)
        vbn[ci, p] = (v2 * bnat).astype(BF16)
        kbe[ci, p] = (kb2 * egn).astype(BF16)
    s = 1
    while 2 * (1 << s) <= chunk:
        joins = jnp.logical_and((prow >> (s + 1)) == (pcol >> (s + 1)), (prow >> s) != (pcol >> s))
        x = {it: _dot(tinv[it].astype(BF16), bd(jnp.where(joins, lmat[it], 0.0))) for it in items}
        tinv = {it: tinv[it] - _dot(x[it].astype(BF16), bd(tinv[it])) for it in items}
        s += 1
    for ci, p in items:
        rs = slice(ci * chunk, (ci + 1) * chunk)
        cs = slice(2 * p * dk, 2 * (p + 1) * dk)
        tb = tinv[ci, p].astype(BF16)
        u_ref[rs, cs] = _dot(tb, _pair_blockdiag(vbn[ci, p][:, :dk], vbn[ci, p][:, dk:]))
        w_ref[rs, cs] = _dot(tb, _pair_blockdiag(kbe[ci, p][:, :dk], kbe[ci, p][:, dk:])).astype(BF16)


def _gdn_prep(q, k, v, bga, *, n_heads, dk, n_sub):
    t, vdim = q.shape
    chunk = GDN_CHUNK
    assert 2 * chunk == LANES and n_heads % 2 == 0 and 2 * n_heads <= LANES
    rows = chunk * n_sub
    qk_w = n_heads * chunk
    row = lambda width: pl.BlockSpec((rows, width), lambda i: (i, 0))
    kern = functools.partial(_gdn_prep_kernel, n_heads=n_heads, dk=dk, chunk=chunk, n_sub=n_sub)
    return pl.pallas_call(
        kern,
        grid=(t // rows,),
        in_specs=[row(vdim), row(vdim), row(vdim), row(LANES)],
        out_specs=[row(vdim), row(vdim), row(vdim), row(vdim), row(qk_w), row(LANES)],
        out_shape=[jax.ShapeDtypeStruct((t, vdim), F32)] + [jax.ShapeDtypeStruct((t, vdim), BF16)] * 3
                  + [jax.ShapeDtypeStruct((t, qk_w), BF16), jax.ShapeDtypeStruct((t, LANES), F32)],
        compiler_params=_cparams("arbitrary"),
        name="gdn_prep",
    )(q, k, v, bga)


def _gdn_scan_kernel(u_ref, w_ref, qe_ref, ke_ref, qk_ref, gc_ref, o_ref, sout_ref, s_ref,
                     *, n_heads, dk, chunk, n_sub, nbat):
    j = pl.program_id(1)

    @pl.when(j == 0)
    def _():
        s_ref[...] = jnp.zeros(s_ref.shape, s_ref.dtype)

    sls = [slice(h * dk, (h + 1) * dk) for h in range(n_heads)]
    items = [(bi, h) for bi in range(nbat) for h in range(n_heads)]

    def body(c, carry):
        r0 = pl.multiple_of(c * chunk, chunk)
        rows = pl.ds(r0, chunk)
        st = {it: s_ref[it[0], it[1]] for it in items}
        sb = {it: st[it].astype(BF16) for it in items}
        ws = {(bi, h): _dot(jnp.concatenate([w_ref[bi, rows, sls[h]], qe_ref[bi, rows, sls[h]]], axis=0),
                            sb[bi, h]) for bi, h in items}
        vb = {(bi, h): (u_ref[bi, rows, sls[h]] - ws[bi, h][:chunk]).astype(BF16) for bi, h in items}
        for bi in range(nbat):
            for p in range(n_heads // 2):
                ha, hb = 2 * p, 2 * p + 1
                intra = _dot(qk_ref[bi, rows, 2 * p * chunk:2 * (p + 1) * chunk],
                             _pair_blockdiag(vb[bi, ha], vb[bi, hb]))
                o_ref[bi, rows, 2 * p * dk:2 * (p + 1) * dk] = (
                    jnp.concatenate([ws[bi, ha][chunk:], ws[bi, hb][chunk:]], axis=1) + intra)
        for bi in range(nbat):
            eg = jnp.exp(gc_ref[bi, pl.ds(r0 + chunk - 1, 1), :])
            for h in range(n_heads):
                s_ref[bi, h] = (st[bi, h] * eg[:, n_heads + h:n_heads + h + 1]
                                + _dot_tn(ke_ref[bi, rows, sls[h]], vb[bi, h]))
        return carry

    lax.fori_loop(0, n_sub, body, 0)

    @pl.when(j == pl.num_programs(1) - 1)
    def _():
        sout_ref[...] = s_ref[...]


def _gdn_scan(u, w, qe, ke, qk, gc, *, batch, seq, n_heads, dk, rows, nbat):
    t, vdim = u.shape
    chunk = GDN_CHUNK
    seq3 = lambda a: a.reshape(batch, seq, a.shape[1])
    row = lambda width: pl.BlockSpec((nbat, rows, width), lambda b, j: (b, j, 0))
    kern = functools.partial(_gdn_scan_kernel, n_heads=n_heads, dk=dk, chunk=chunk, n_sub=rows // chunk,
                             nbat=nbat)
    o, s_new = pl.pallas_call(
        kern,
        grid=(batch // nbat, seq // rows),
        in_specs=[row(vdim)] * 4 + [row(qk.shape[1]), row(LANES)],
        out_specs=[row(vdim), pl.BlockSpec((nbat, n_heads, dk, dk), lambda b, j: (b, 0, 0, 0))],
        out_shape=[jax.ShapeDtypeStruct((batch, seq, vdim), F32),
                   jax.ShapeDtypeStruct((batch, n_heads, dk, dk), F32)],
        scratch_shapes=[pltpu.VMEM((nbat, n_heads, dk, dk), F32)],
        compiler_params=_cparams("arbitrary", "arbitrary"),
        name="gdn_scan",
    )(seq3(u), seq3(w), seq3(qe), seq3(ke), seq3(qk), seq3(gc))
    return o.reshape(t, vdim), s_new


def _gdn_step_kernel(q_ref, k_ref, v_ref, bga_ref, s_ref, o_ref, sout_ref, *, n_heads, dk, nb):
    s_ref = s_ref.at[0]
    pad = jnp.zeros((dk - n_heads, dk), F32)

    def body(b, carry):
        qt = jnp.concatenate([q_ref[b], pad], axis=0).T
        kt = jnp.concatenate([k_ref[b], pad], axis=0).T
        vb = v_ref[b]
        bg = bga_ref[pl.ds(b, 1), :]
        eg = jnp.exp(bg)
        rows = []
        for h in range(n_heads):
            st = s_ref[b, h] * eg[:, n_heads + h:n_heads + h + 1]
            kc = kt[:, h:h + 1]
            ks = jnp.sum(kc * st, axis=0, keepdims=True)
            delta = (vb[h:h + 1, :] - ks) * bg[:, h:h + 1]
            st = st + kc * delta
            sout_ref[b, h] = st
            rows.append(jnp.sum(qt[:, h:h + 1] * st, axis=0, keepdims=True))
        o_ref[b] = jnp.concatenate(rows, axis=0)
        return carry

    for b in range(nb):
        body(b, 0)


def _gdn_step(q, k, v, bga, states, layer, *, n_heads, dk, nb):
    b = q.shape[0]
    q3, k3, v3 = (a.reshape(b, n_heads, dk) for a in (q, k, v))
    vec = pl.BlockSpec((nb, n_heads, dk), lambda i: (i, 0, 0))
    st = pl.BlockSpec((nb, n_heads, dk, dk), lambda i: (i, 0, 0, 0))
    kern = functools.partial(_gdn_step_kernel, n_heads=n_heads, dk=dk, nb=nb)
    o, s_new = pl.pallas_call(
        kern,
        grid=(b // nb,),
        in_specs=[vec, vec, vec, pl.BlockSpec((nb, LANES), lambda i: (i, 0)),
                  pl.BlockSpec((1, nb, n_heads, dk, dk), lambda i: (layer, i, 0, 0, 0))],
        out_specs=[vec, st],
        out_shape=[jax.ShapeDtypeStruct((b, n_heads, dk), F32), jax.ShapeDtypeStruct(states.shape[1:], F32)],
        compiler_params=_cparams("arbitrary"),
        name="gdn_step",
    )(q3, k3, v3, bga, states)
    return o.reshape(b, n_heads * dk), s_new


def _gdn_tail(o_ref, z_ref, nw_ref, w_ref, *, n_heads, dk):
    nw = nw_ref[...]
    parts = []
    for h in range(n_heads):
        sl = slice(h * dk, (h + 1) * dk)
        parts.append((_rms(o_ref[:, sl], nw, EPS) * _silu(z_ref[:, sl].astype(F32))).astype(BF16))
    return _dot(jnp.concatenate(parts, axis=1), w_ref[...])


def _gdn_out_kernel(o_ref, z_ref, x_ref, nw_ref, w_ref, y_ref, *, n_heads, dk):
    y_ref[...] = x_ref[...] + _gdn_tail(o_ref, z_ref, nw_ref, w_ref, n_heads=n_heads, dk=dk)


def _gdn_out(o, z, x, norm_w, w_out, *, tm, n_heads, dk):
    t, d = x.shape
    vdim = o.shape[1]
    row = lambda width: pl.BlockSpec((tm, width), lambda i: (i, 0))
    kern = functools.partial(_gdn_out_kernel, n_heads=n_heads, dk=dk)
    return pl.pallas_call(
        kern,
        grid=(t // tm,),
        in_specs=[row(vdim), row(vdim), row(d), _const_spec(norm_w.shape), _const_spec(w_out.shape)],
        out_specs=row(d),
        out_shape=jax.ShapeDtypeStruct((t, d), F32),
        compiler_params=_cparams("arbitrary"),
        name="gdn_out",
    )(o, z, x, norm_w, w_out)


def _ffn_tail(x, conv, up, cb_ref, wd_ref, gf_ref, o_ref, final_norm):
    act = (_silu(conv + cb_ref[...]) * up).astype(BF16)
    y = x + _dot(act, wd_ref[...])
    if final_norm:
        y = _rms(y, gf_ref[...], EPS)
    o_ref[...] = y


def _ffn_seq_kernel(x_ref, g_ref, wg_ref, wu_ref, cw_ref, cb_ref, wd_ref, gf_ref, *rest,
                    tm, tiles_per_seq, final_norm, cwid, mix):
    x = x_ref[...]
    if mix[0] == "gdn":
        o_in, z_in, nw_ref, wo_ref, o_ref, tail_ref, buf_ref, act_ref = rest
        x = x + _gdn_tail(o_in, z_in, nw_ref, wo_ref, n_heads=mix[1], dk=mix[2])
    else:
        att_in, bg_in, cx_in, scw_ref, wo_ref, o_ref, tail_ref, buf_ref, act_ref, cbuf_ref = rest
        x = x + _even_tail_seq(att_in, bg_in, cx_in, scw_ref, wo_ref, cbuf_ref, tm=tm,
                               tiles_per_seq=tiles_per_seq)
    xn = _rms(x, g_ref[...], EPS).astype(BF16)
    first = (pl.program_id(0) % tiles_per_seq) == 0

    @pl.when(first)
    def _():
        buf_ref[0:HIST_ROWS, :] = jnp.zeros((HIST_ROWS, buf_ref.shape[1]), buf_ref.dtype)

    @pl.when(jnp.logical_not(first))
    def _():
        buf_ref[0:HIST_ROWS, :] = buf_ref[tm:tm + HIST_ROWS, :]

    cw = cw_ref[...]
    cb = cb_ref[...]
    width = cw.shape[0]
    n_chunks = wg_ref.shape[1] // cwid

    def mm(c):
        cs = slice(c * cwid, (c + 1) * cwid)
        return _dot(xn, wg_ref[:, cs]), _dot(xn, wu_ref[:, cs])

    nxt = mm(0)
    for c in range(n_chunks):
        cs = slice(c * cwid, (c + 1) * cwid)
        gate, up = nxt
        if c + 1 < n_chunks:
            nxt = mm(c + 1)
        buf_ref[HIST_ROWS:HIST_ROWS + tm, cs] = gate
        conv = cw[width - 1:width, cs] * gate
        for j in range(width - 1):
            conv = conv + cw[j:j + 1, cs] * buf_ref[pl.ds(HIST_ROWS - (width - 1) + j, tm), cs]
        act_ref[:, cs] = (_silu(conv + cb[:, cs]) * up).astype(BF16)
    tail_ref[0] = buf_ref[tm:tm + HIST_ROWS, :]
    y = x + _dot(act_ref[...], wd_ref[...])
    if final_norm:
        y = _rms(y, gf_ref[...], EPS)
    o_ref[...] = y


def _ffn_step_kernel(x_ref, g_ref, wg_ref, wu_ref, cw_ref, cb_ref, wd_ref, gf_ref, *rest,
                     n_hist, final_norm):
    hist_refs = rest[:n_hist]
    o_ref, raw_ref = rest[n_hist:]
    x = x_ref[...]
    xn = _rms(x, g_ref[...], EPS).astype(BF16)
    gate = _dot(xn, wg_ref[...])
    raw_ref[...] = gate
    cw = cw_ref[...]
    width = cw.shape[0]
    conv = cw[width - 1:width] * gate
    for j in range(width - 1):
        conv = conv + cw[j:j + 1] * hist_refs[j][...]
    _ffn_tail(x, conv, _dot(xn, wu_ref[...]), cb_ref, wd_ref, gf_ref, o_ref, final_norm)


def _ffn(x, g, wg, wu, conv_w, conv_b, wd, gf, *, tm, final_norm, seq=None, hist=None, mix=None):
    t, d = x.shape
    d_ff = wg.shape[1]
    row = lambda width: pl.BlockSpec((tm, width), lambda i: (i, 0))
    common = [row(d), _const_spec((1, d)), _const_spec(wg.shape), _const_spec(wu.shape),
              _const_spec(conv_w.shape), _const_spec(conv_b.shape), _const_spec(wd.shape),
              _const_spec((1, d))]
    if hist is None:
        tps = seq // tm
        assert d_ff % (2 * LANES) == 0
        scratch = [pltpu.VMEM((tm + HIST_ROWS, d_ff), F32), pltpu.VMEM((tm, d_ff), BF16)]
        if mix[0] == "gdn":
            _, o_in, z_in, nw, w_out, n_heads, dk = mix
            extra = [o_in, z_in, nw, w_out]
            extra_specs = [row(o_in.shape[1]), row(z_in.shape[1]), _const_spec(nw.shape), _const_spec(w_out.shape)]
            mix_static = ("gdn", n_heads, dk)
        else:
            _, att, bg, cx, scw, w_out = mix
            extra = [att, bg, cx, scw, w_out]
            extra_specs = [row(att.shape[1]), row(bg.shape[1]), row(cx.shape[1]), _const_spec(scw.shape),
                           _const_spec(w_out.shape)]
            scratch.append(pltpu.VMEM((tm + HIST_ROWS, cx.shape[1]), F32))
            mix_static = ("even",)
        kern = functools.partial(_ffn_seq_kernel, tm=tm, tiles_per_seq=tps, final_norm=final_norm,
                                 cwid=2 * LANES, mix=mix_static)
        outs = [row(d), pl.BlockSpec((1, HIST_ROWS, d_ff), lambda i: (i // tps, 0, 0))]
        out_shapes = [jax.ShapeDtypeStruct((t, d), F32),
                      jax.ShapeDtypeStruct((t // seq, HIST_ROWS, d_ff), F32)]
    else:
        assert mix is None
        hist_arr, n_hist = hist
        kern = functools.partial(_ffn_step_kernel, n_hist=n_hist, final_norm=final_norm)
        extra = [hist_arr] * n_hist
        extra_specs = [pl.BlockSpec((tm, d_ff), functools.partial(lambda i, j: (i, j), j=j))
                       for j in range(n_hist)]
        outs = [row(d), row(d_ff)]
        out_shapes = [jax.ShapeDtypeStruct((t, d), F32), jax.ShapeDtypeStruct((t, d_ff), F32)]
        scratch = []
    return pl.pallas_call(
        kern,
        grid=(t // tm,),
        in_specs=common + extra_specs,
        out_specs=outs,
        out_shape=out_shapes,
        scratch_shapes=scratch,
        compiler_params=_cparams("arbitrary"),
        name="conv_ffn",
    )(x, g, wg, wu, conv_w, conv_b, wd, gf, *extra)


def _tile(n, pref):
    t = min(n, pref)
    while n % t:
        t //= 2
    return t


def kernel(x_prompt, x_sample, cache_k, cache_v, state_sconv, state_gdn_conv, state_gdn, state_ffn_conv, page_table, norm_mix, norm_ffn, norm_final, w_in_even, w_out_even, lambda_q1, lambda_k1, lambda_q2, lambda_k2, subln_w, sconv_w, w_in_odd, gdn_conv_w, a_log, dt_bias, gdn_norm_w, w_out_odd, w_up, ffn_conv_w, ffn_conv_b, w_down):
    batch, seq, d = x_prompt.shape
    dec_b = x_sample.shape[0]
    depth = norm_mix.shape[0]
    n_heads_a, e_a = cache_k.shape[3], cache_k.shape[4]
    n_att = n_heads_a * e_a
    rot = (e_a // 2) // 4
    w_b = state_sconv.shape[-1]
    sconv = sconv_w.shape[1]
    page = cache_k.shape[2]
    past_len = page_table.shape[1] * page
    n_heads_c, dk = state_gdn.shape[2], state_gdn.shape[3]
    kdim = n_heads_c * dk
    qkv_c = gdn_conv_w.shape[-1]
    gdn_conv = gdn_conv_w.shape[1]
    d_ff = ffn_conv_w.shape[-1]
    ffn_conv = ffn_conv_w.shape[1]
    t_p = batch * seq

    tm_p = _tile(seq, 512)
    blk = _tile(seq, 512)
    scan_rows = _tile(seq, 256)

    xp = x_prompt.reshape(t_p, d)
    xs = x_sample.reshape(dec_b, d)
    rope_p = _rope_tables(jnp.arange(seq), e_a, rot)
    rope_s = _rope_tables(jnp.full((dec_b,), past_len), e_a, rot)
    gf = norm_final.reshape(1, d)

    kp, vp, ks, vs, scp, scs = [], [], [], [], [], []
    gcp, gcs, gsp, gss, fcp, fcs = [], [], [], [], [], []
    for l in range(depth):
        gm = norm_mix[l].reshape(1, d)
        if l % 2 == 0:
            e = l // 2
            lam_init = 0.8 - 0.6 * math.exp(-0.3 * l)
            w_in = w_in_even[e].astype(BF16)
            w_out = w_out_even[e].astype(BF16)
            lam_p = jnp.stack([lambda_q1[e], lambda_k1[e], lambda_q2[e], lambda_k2[e]])
            sw = subln_w[e].reshape(1, e_a)
            proj = functools.partial(_even_proj, n_heads=n_heads_a, e_a=e_a, w_b=w_b, half=rot // 2)
            q, k, v, kb, vt, bg, cx = proj(xp, gm, w_in, rope_p, tm=tm_p)
            att = _attn_prompt(q, kb, vt, lam_p, sw, batch=batch, seq=seq, n_heads=n_heads_a, e_a=e_a,
                               blk=blk, lam_init=lam_init)
            mix_p = ("even", att, bg, cx, sconv_w[e], w_out)
            kp.append(k.reshape(batch, seq, n_heads_a, e_a))
            vp.append(v.reshape(batch, seq, n_heads_a, e_a))
            cx3 = cx.reshape(batch, seq, w_b)
            if seq >= sconv - 1:
                scp.append(cx3[:, seq - (sconv - 1):])
            else:
                scp.append(jnp.concatenate([jnp.zeros((batch, sconv - 1 - seq, w_b), F32), cx3], axis=1))
            q, k, v, kb, vt, bg, cx = proj(xs, gm, w_in, rope_s, tm=dec_b)
            att = _attn_sample(q, k, v, cache_k, cache_v, e, page_table, lam_p, sw,
                               n_heads=n_heads_a, e_a=e_a, lam_init=lam_init)
            hist = [state_sconv[e, :, j] for j in range(sconv - 1)]
            xs = _even_out_step(att, bg, cx, xs, sconv_w[e], w_out, hist, tm=dec_b)
            ks.append(k.reshape(dec_b, 1, n_heads_a, e_a))
            vs.append(v.reshape(dec_b, 1, n_heads_a, e_a))
            scs.append(jnp.concatenate([state_sconv[e, :, 1:], cx[:, None, :]], axis=1))
        else:
            o = l // 2
            w_in = w_in_odd[o]
            wqkv = w_in[:, :qkv_c].astype(BF16)
            wz = w_in[:, qkv_c:qkv_c + kdim].astype(BF16)
            wba = jnp.pad(w_in[:, qkv_c + kdim:], ((0, 0), (0, LANES - 2 * n_heads_c))).astype(BF16)
            w_out = w_out_odd[o].astype(BF16)
            lane_pad = lambda a: jnp.pad(a, (n_heads_c, LANES - 2 * n_heads_c)).reshape(1, LANES)
            al_pad, dtb_pad = lane_pad(a_log[o]), lane_pad(dt_bias[o])
            nw = gdn_norm_w[o].reshape(1, dk)
            proj = functools.partial(_gdn_proj, n_heads=n_heads_c, dk=dk)
            q, k, v, z, bga, tail = proj(xp, gm, wqkv, wz, wba, gdn_conv_w[o], al_pad, dtb_pad,
                                         tm=tm_p, seq=seq)
            u, w, qe, ke, qk, gc = _gdn_prep(q, k, v, bga, n_heads=n_heads_c, dk=dk,
                                             n_sub=_tile(seq // GDN_CHUNK, 4))
            og, s_new = _gdn_scan(u, w, qe, ke, qk, gc, batch=batch, seq=seq, n_heads=n_heads_c, dk=dk,
                                  rows=scan_rows, nbat=_tile(batch, 4))
            mix_p = ("gdn", og, z, nw, w_out, n_heads_c, dk)
            gcp.append(tail[:, HIST_ROWS - (gdn_conv - 1):])
            gsp.append(s_new)
            hist_arr = state_gdn_conv[o].reshape(dec_b, (gdn_conv - 1) * qkv_c)
            q, k, v, z, bga, raw = proj(xs, gm, wqkv, wz, wba, gdn_conv_w[o], al_pad, dtb_pad,
                                        tm=dec_b, hist=(hist_arr, gdn_conv - 1))
            og, s_new = _gdn_step(q, k, v, bga, state_gdn, o, n_heads=n_heads_c, dk=dk, nb=_tile(dec_b, 8))
            xs = _gdn_out(og, z, xs, nw, w_out, tm=dec_b, n_heads=n_heads_c, dk=dk)
            gcs.append(jnp.concatenate([state_gdn_conv[o, :, 1:], raw[:, None, :]], axis=1))
            gss.append(s_new)
        gn = norm_ffn[l].reshape(1, d)
        wg = w_up[l, :, :d_ff].astype(BF16)
        wu = w_up[l, :, d_ff:].astype(BF16)
        wd = w_down[l].astype(BF16)
        cb = ffn_conv_b[l].reshape(1, d_ff)
        last = l == depth - 1
        xp, tail = _ffn(xp, gn, wg, wu, ffn_conv_w[l], cb, wd, gf, tm=tm_p, final_norm=last, seq=seq,
                        mix=mix_p)
        fcp.append(tail[:, HIST_ROWS - (ffn_conv - 1):])
        hist_arr = state_ffn_conv[l].reshape(dec_b, (ffn_conv - 1) * d_ff)
        xs, raw = _ffn(xs, gn, wg, wu, ffn_conv_w[l], cb, wd, gf, tm=dec_b, final_norm=last,
                       hist=(hist_arr, ffn_conv - 1))
        fcs.append(jnp.concatenate([state_ffn_conv[l, :, 1:], raw[:, None, :]], axis=1))
    return (xp.reshape(batch, seq, d), xs.reshape(dec_b, 1, d),
            jnp.stack(kp), jnp.stack(vp), jnp.stack(ks), jnp.stack(vs),
            jnp.stack(scp), jnp.stack(scs),
            jnp.stack(gcp), jnp.stack(gcs), jnp.stack(gsp), jnp.stack(gss),
            jnp.stack(fcp), jnp.stack(fcs))
```

```python
import functools
import math

import jax
import jax.numpy as jnp
from jax import lax
from jax.experimental import pallas as pl
from jax.experimental.pallas import tpu as pltpu

F32 = jnp.float32
BF16 = jnp.bfloat16

EPS = 1e-6
SUBLN_EPS = 1e-5
ROPE_THETA = 500000.0
GDN_CHUNK = 64
V7X_VMEM_LIMIT = 56 * 1024 * 1024
LANES = 128
HIST_ROWS = 8
BF16_ROWS = 16
LOG2E = math.log2(math.e)
VT_PAD = 16
ATTN_HEADS = 4
ATTN_SUB = 2


def _cparams(*sem):
    return pltpu.CompilerParams(dimension_semantics=sem, vmem_limit_bytes=V7X_VMEM_LIMIT)


def _const_spec(shape):
    nd = len(shape)
    return pl.BlockSpec(shape, lambda i: (0,) * nd, pipeline_mode=pl.Buffered(1))


def _rms(x, g, eps):
    return x * lax.rsqrt(jnp.mean(x * x, axis=-1, keepdims=True) + eps) * g


def _sigmoid(x):
    return 0.5 * jnp.tanh(0.5 * x) + 0.5


def _silu(x):
    h = 0.5 * x
    return h * jnp.tanh(h) + h


def _softplus(x):
    return jnp.maximum(x, 0.0) + jnp.log1p(jnp.exp(-jnp.abs(x)))


def _dot(a, b):
    return jnp.dot(a, b, preferred_element_type=F32)


def _dot_nt(a, b):
    return lax.dot_general(a, b, (((1,), (1,)), ((), ())), preferred_element_type=F32)


def _dot_tn(a, b):
    return lax.dot_general(a, b, (((0,), (0,)), ((), ())), preferred_element_type=F32)


def _split3(a):
    hi = a.astype(BF16)
    r = a - hi.astype(F32)
    mid = r.astype(BF16)
    lo = (r - mid.astype(F32)).astype(BF16)
    return hi, mid, lo


def _even_proj_kernel(x_ref, g_ref, w_ref, c_ref, s1_ref, s2_ref, *rest,
                      tm, n_heads, e_a, w_b, half, qscale, stacked):
    if stacked:
        q_ref, k_ref, v_ref, kb_ref, vt_ref, bg_ref, cx_ref = rest[2:]
        k_ref, v_ref = k_ref.at[0], v_ref.at[0]
    else:
        q_ref, k_ref, v_ref, kb_ref, vt_ref, bg_ref, cx_ref = rest
    xn = _rms(x_ref[...], g_ref[...], EPS).astype(BF16)
    n = n_heads * e_a
    c, s1, s2 = c_ref[...], s1_ref[...], s2_ref[...]

    def rope(t):
        return t * c + pltpu.roll(t, e_a - half, 1) * s1 + pltpu.roll(t, half, 1) * s2

    hqk = _dot(xn, w_ref[:, :2 * n])
    for h in range(n_heads):
        sl = slice(h * e_a, (h + 1) * e_a)
        q_ref[:, sl] = (rope(hqk[:, sl]) * qscale).astype(BF16)
        kr = rope(hqk[:, n + h * e_a:n + (h + 1) * e_a])
        k_ref[pl.ds(h, tm, stride=n_heads), :] = kr
        kb_ref[:, sl] = kr.astype(BF16)
    hv = _dot(xn, w_ref[:, 2 * n:3 * n])
    for h in range(n_heads):
        v_ref[pl.ds(h, tm, stride=n_heads), :] = hv[:, h * e_a:(h + 1) * e_a]
    hvt = hv.T.astype(BF16)
    ones_row = jnp.where(lax.broadcasted_iota(jnp.int32, (VT_PAD, tm), 0) == 0, 1.0, 0.0).astype(BF16)
    for h in range(n_heads):
        r0 = h * (e_a + VT_PAD)
        vt_ref[r0:r0 + e_a, :] = hvt[h * e_a:(h + 1) * e_a, :]
        vt_ref[r0 + e_a:r0 + e_a + VT_PAD, :] = ones_row
    o = 3 * n
    bg_ref[...] = _dot(xn, w_ref[:, o:o + w_b])
    hc = _dot(xn, w_ref[:, o + w_b:o + 3 * w_b])
    cx_ref[...] = hc[:, :w_b] * hc[:, w_b:]


def _even_proj(x, g, w, rope_tabs, *, tm, n_heads, e_a, w_b, half, stack=None):
    t, d = x.shape
    n = n_heads * e_a
    c, s1, s2 = rope_tabs
    row = lambda width: pl.BlockSpec((tm, width), lambda i: (i, 0))
    tab_tiles = c.shape[0] // tm
    n_vt = n_heads * (e_a + VT_PAD)
    tab = pl.BlockSpec((tm, e_a), lambda i: (i % tab_tiles, 0))
    tok = pl.BlockSpec((tm * n_heads, e_a), lambda i: (i, 0))
    kern = functools.partial(_even_proj_kernel, tm=tm, n_heads=n_heads, e_a=e_a, w_b=w_b, half=half,
                             qscale=(e_a // 2) ** -0.5 * LOG2E, stacked=stack is not None)
    if stack is None:
        extra, extra_specs, aliases = [], [], {}
        kv_shape = jax.ShapeDtypeStruct((t * n_heads, e_a), F32)
    else:
        k_all, v_all, layer = stack
        extra, extra_specs = [k_all, v_all], [pl.BlockSpec(memory_space=pl.ANY)] * 2
        aliases = {6: 1, 7: 2}
        tok = pl.BlockSpec((1, tm * n_heads, e_a), lambda i: (layer, i, 0))
        kv_shape = jax.ShapeDtypeStruct(k_all.shape, F32)
    return pl.pallas_call(
        kern,
        grid=(t // tm,),
        in_specs=[row(d), _const_spec((1, d)), _const_spec(w.shape), tab, tab, tab] + extra_specs,
        out_specs=[row(n), tok, tok, row(n), pl.BlockSpec((n_vt, tm), lambda i: (0, i)),
                   row(w_b), row(w_b)],
        input_output_aliases=aliases,
        out_shape=[jax.ShapeDtypeStruct((t, n), BF16), kv_shape, kv_shape, jax.ShapeDtypeStruct((t, n), BF16),
                   jax.ShapeDtypeStruct((n_vt, t), BF16), jax.ShapeDtypeStruct((t, w_b), F32),
                   jax.ShapeDtypeStruct((t, w_b), F32)],
        compiler_params=_cparams("arbitrary"),
        name="even_proj",
    )(x, g, w, c, s1, s2, *extra)


def _rope_tables(pos, e_a, rot, dtype=F32):
    half = rot // 2
    dh = e_a // 2
    inv = ROPE_THETA ** (-jnp.arange(half, dtype=F32) * 2.0 / rot)
    ang = pos.astype(F32)[:, None] * inv[None, :]
    cos, sin = jnp.cos(ang).astype(dtype), jnp.sin(ang).astype(dtype)
    t = pos.shape[0]
    ones = jnp.ones((t, dh - rot), dtype)
    zeros_h = jnp.zeros((t, half), dtype)
    zeros_r = jnp.zeros((t, dh - rot), dtype)
    c = jnp.concatenate([cos, cos, ones], axis=1)
    s1 = jnp.concatenate([-sin, zeros_h, zeros_r], axis=1)
    s2 = jnp.concatenate([zeros_h, sin, zeros_r], axis=1)
    rep = lambda a: jnp.concatenate([a, a], axis=1)
    return rep(c), rep(s1), rep(s2)


def _lambda(lp_ref, lam_init):
    lp = lp_ref[...]
    a = jnp.sum(lp[0:1] * lp[1:2], axis=-1, keepdims=True)
    b = jnp.sum(lp[2:3] * lp[3:4], axis=-1, keepdims=True)
    return jnp.exp(a) - jnp.exp(b) + lam_init


def _attn_prompt_kernel(q_ref, k_ref, vt_ref, lp_ref, swc_ref, o_ref, m_ref, acc_ref,
                        *, blk, n_sub, nh, dh, lam_init):
    i = pl.program_id(2)
    e_a = 2 * dh
    vr = vt_ref.shape[0] // nh
    sub = blk // n_sub
    feat = lax.broadcasted_iota(jnp.int32, (e_a, blk), 0)
    qs = []
    for hh in range(nh):
        qt = q_ref[:, hh * e_a:(hh + 1) * e_a].astype(F32).T
        qs.append(jnp.concatenate([jnp.where(feat < dh, qt, 0.0), jnp.where(feat >= dh, qt, 0.0)],
                                  axis=1).astype(BF16))

    m_ref[...] = jnp.full(m_ref.shape, -1e30, F32)
    acc_ref[...] = jnp.zeros(acc_ref.shape, F32)

    def step(off, diag):
        def scores(hh, t):
            s = _dot(k_ref[pl.ds(off + t * sub, sub), hh * e_a:(hh + 1) * e_a], qs[hh])
            if diag:
                key = lax.broadcasted_iota(jnp.int32, s.shape, 0) + t * sub
                qry = lax.broadcasted_iota(jnp.int32, s.shape, 1)
                qry = jnp.where(qry >= blk, qry - blk, qry)
                s = jnp.where(key <= qry, s, jnp.finfo(F32).min)
            return s

        ss = [[scores(hh, t) for t in range(n_sub)] for hh in range(nh)]
        for hh in range(nh):
            for t in range(n_sub):
                m = m_ref[hh]
                m_new = jnp.maximum(m, jnp.max(ss[hh][t], axis=0, keepdims=True))
                p = jnp.exp2(ss[hh][t] - m_new)
                acc_ref[hh] = jnp.exp2(m - m_new) * acc_ref[hh] + _dot(
                    vt_ref[hh * vr:(hh + 1) * vr, pl.ds(off + t * sub, sub)], p.astype(BF16))
                m_ref[hh] = m_new

    def body(j, carry):
        step(pl.multiple_of(j * blk, blk), False)
        return carry

    lax.fori_loop(0, i, body, 0)
    step(pl.multiple_of(i * blk, blk), True)
    lam = _lambda(lp_ref, lam_init)
    for hh in range(nh):
        acc = acc_ref[hh]
        o = acc[:e_a] / acc[e_a:e_a + 1]
        att = o[:, :blk] - lam * o[:, blk:]
        y = att * lax.rsqrt(jnp.mean(att * att, axis=0, keepdims=True) + SUBLN_EPS) * swc_ref[...]
        o_ref[:, hh * e_a:(hh + 1) * e_a] = (y * (1.0 - lam_init)).T.astype(o_ref.dtype)


def _attn_prompt(q, kb, vt, lam_p, subln_w, *, batch, seq, n_heads, e_a, blk, lam_init):
    t = q.shape[0]
    nq = seq // blk
    swc = subln_w.reshape(e_a, 1)
    n_sub = ATTN_SUB if blk % (ATTN_SUB * LANES) == 0 else 1
    nh = ATTN_HEADS if n_heads % ATTN_HEADS == 0 else 1
    kern = functools.partial(_attn_prompt_kernel, blk=blk, n_sub=n_sub, nh=nh, dh=e_a // 2, lam_init=lam_init)
    return pl.pallas_call(
        kern,
        grid=(batch, n_heads // nh, nq),
        in_specs=[pl.BlockSpec((blk, nh * e_a), lambda b, h, i: (b * nq + i, h)),
                  pl.BlockSpec((seq, nh * e_a), lambda b, h, i: (b, h)),
                  pl.BlockSpec((nh * (e_a + VT_PAD), seq), lambda b, h, i: (h, b)),
                  pl.BlockSpec(lam_p.shape, lambda b, h, i: (0, 0)),
                  pl.BlockSpec(swc.shape, lambda b, h, i: (0, 0))],
        out_specs=pl.BlockSpec((blk, nh * e_a), lambda b, h, i: (b * nq + i, h)),
        out_shape=jax.ShapeDtypeStruct((t, n_heads * e_a), BF16),
        scratch_shapes=[pltpu.VMEM((nh, 1, 2 * blk), F32), pltpu.VMEM((nh, e_a + VT_PAD, 2 * blk), F32)],
        compiler_params=_cparams("arbitrary", "arbitrary", "arbitrary"),
        name="attn_prompt",
    )(q, kb, vt, lam_p, swc)


def _attn_sample_kernel(pt_ref, q_ref, kn_ref, vn_ref, lp_ref, sw_ref, *rest,
                        n_pages, n_heads, e_a, lam_init):
    del pt_ref
    k_refs = rest[:n_pages]
    v_refs = rest[n_pages:2 * n_pages]
    o_ref = rest[2 * n_pages]
    dh = e_a // 2
    nm = max(2 * n_heads, BF16_ROWS)
    rows_pp = k_refs[0].shape[1]
    rowi = lax.broadcasted_iota(jnp.int32, (nm, e_a), 0)
    lane = lax.broadcasted_iota(jnp.int32, (nm, e_a), 1)

    def per_head_rows(ref):
        t = ref[0]
        out = jnp.zeros((nm, e_a), F32)
        for h in range(n_heads):
            out = jnp.where((rowi >> 1) == h, jnp.broadcast_to(t[h:h + 1].astype(F32), (nm, e_a)), out)
        return out

    qm_f = jnp.where((lane // dh) == (rowi & 1), per_head_rows(q_ref), 0.0)
    qm = qm_f.astype(BF16)
    colh = lax.broadcasted_iota(jnp.int32, (nm, rows_pp), 1) % n_heads
    valid = colh == (lax.broadcasted_iota(jnp.int32, (nm, rows_pp), 0) >> 1)
    ss = [jnp.where(valid, _dot_nt(qm, k_refs[p][0].astype(BF16)), jnp.finfo(F32).min)
          for p in range(n_pages)]
    kn = per_head_rows(kn_ref).astype(BF16).astype(F32)
    vn = per_head_rows(vn_ref).astype(BF16).astype(F32)
    s_new = jnp.sum(qm.astype(F32) * kn, axis=-1, keepdims=True)
    m = s_new
    for s in ss:
        m = jnp.maximum(m, jnp.max(s, axis=-1, keepdims=True))
    p_new = jnp.exp2(s_new - m)
    l = p_new
    acc = p_new.astype(BF16).astype(F32) * vn
    for p in range(n_pages):
        pr = jnp.exp2(ss[p] - m)
        l = l + jnp.sum(pr, axis=-1, keepdims=True)
        acc = acc + _dot(pr.astype(BF16), v_refs[p][0].astype(BF16))
    o = acc / l
    lam = _lambda(lp_ref, lam_init)
    sw = sw_ref[...]
    for h in range(n_heads):
        att = o[2 * h:2 * h + 1] - lam * o[2 * h + 1:2 * h + 2]
        o_ref[0, h:h + 1, :] = (_rms(att, sw, SUBLN_EPS) * (1.0 - lam_init)).astype(o_ref.dtype)


def _attn_sample(q, k_new, v_new, cache_k, cache_v, layer, page_table, lam_p, subln_w,
                 *, n_heads, e_a, lam_init):
    b, n = q.shape
    n_pages = page_table.shape[1]
    n_phys, page = cache_k.shape[1], cache_k.shape[2]
    ck = cache_k.reshape(cache_k.shape[0] * n_phys, page * n_heads, e_a)
    cv = cache_v.reshape(cache_v.shape[0] * n_phys, page * n_heads, e_a)
    base = layer * n_phys
    tok = pl.BlockSpec((1, n_heads, e_a), lambda i, pt: (i, 0, 0))
    page_specs = [pl.BlockSpec((1, page * n_heads, e_a), functools.partial(
        lambda i, pt, p: (base + pt[i, p], 0, 0), p=p)) for p in range(n_pages)]
    kern = functools.partial(_attn_sample_kernel, n_pages=n_pages, n_heads=n_heads, e_a=e_a,
                             lam_init=lam_init)
    grid_spec = pltpu.PrefetchScalarGridSpec(
        num_scalar_prefetch=1,
        grid=(b,),
        in_specs=[tok, tok, tok,
                  pl.BlockSpec(lam_p.shape, lambda i, pt: (0, 0)),
                  pl.BlockSpec(subln_w.shape, lambda i, pt: (0, 0))] + page_specs + page_specs,
        out_specs=tok,
    )
    tok3 = lambda a: a.reshape(b, n_heads, e_a)
    out = pl.pallas_call(
        kern,
        grid_spec=grid_spec,
        out_shape=jax.ShapeDtypeStruct((b, n_heads, e_a), F32),
        compiler_params=_cparams("arbitrary"),
        name="attn_sample",
    )(page_table, tok3(q).astype(F32), tok3(k_new), tok3(v_new), lam_p, subln_w,
      *([ck] * n_pages), *([cv] * n_pages))
    return out.reshape(b, n)


def _seq_hist(buf_ref, cur, tiles_per_seq, tm):
    first = (pl.program_id(0) % tiles_per_seq) == 0

    @pl.when(first)
    def _():
        buf_ref[0:HIST_ROWS, :] = jnp.zeros((HIST_ROWS, buf_ref.shape[1]), buf_ref.dtype)

    @pl.when(jnp.logical_not(first))
    def _():
        buf_ref[0:HIST_ROWS, :] = buf_ref[tm:tm + HIST_ROWS, :]

    buf_ref[HIST_ROWS:HIST_ROWS + tm, :] = cur


def _even_tail_seq(att_ref, bg_ref, cx_ref, cw_ref, w_ref, buf_ref, *, tm, tiles_per_seq):
    n_att = att_ref.shape[1]
    cx = cx_ref[...]
    _seq_hist(buf_ref, cx, tiles_per_seq, tm)
    cw = cw_ref[...]
    width = cw.shape[0]
    conv = cw[width - 1:width] * cx
    for j in range(width - 1):
        conv = conv + cw[j:j + 1] * buf_ref[pl.ds(HIST_ROWS - (width - 1) + j, tm), :]
    yb = (bg_ref[...] * conv).astype(BF16)
    return _dot(att_ref[...].astype(BF16), w_ref[:n_att, :]) + _dot(yb, w_ref[n_att:, :])


def _even_out_step_kernel(att_ref, bg_ref, cx_ref, x_ref, cw_ref, w_ref, *rest, n_att):
    hist_refs, o_ref = rest[:-1], rest[-1]
    cw = cw_ref[...]
    width = cw.shape[0]
    conv = cw[width - 1:width] * cx_ref[...]
    for j in range(width - 1):
        conv = conv + cw[j:j + 1] * hist_refs[j][...]
    yb = (bg_ref[...] * conv).astype(BF16)
    o_ref[...] = (x_ref[...] + _dot(att_ref[...].astype(BF16), w_ref[:n_att, :])
                  + _dot(yb, w_ref[n_att:, :]))


def _even_out_step(att, bg, cx, x, conv_w, w_out, hist, *, tm):
    t, d = x.shape
    n_att, w_b = att.shape[1], bg.shape[1]
    row = lambda width: pl.BlockSpec((tm, width), lambda i: (i, 0))
    return pl.pallas_call(
        functools.partial(_even_out_step_kernel, n_att=n_att),
        grid=(t // tm,),
        in_specs=[row(n_att), row(w_b), row(w_b), row(d), _const_spec(conv_w.shape), _const_spec(w_out.shape)]
                 + [row(w_b)] * len(hist),
        out_specs=row(d),
        out_shape=jax.ShapeDtypeStruct((t, d), F32),
        compiler_params=_cparams("arbitrary"),
        name="even_out",
    )(att, bg, cx, x, conv_w, w_out, *hist)


def _gdn_gates(xn, wz_ref, wba_ref, al_ref, dtb_ref, z_ref, bga_ref, *, n_heads):
    z_ref[...] = _dot(xn, wz_ref[...]).astype(z_ref.dtype)
    ba = _dot(xn, wba_ref[...])
    beta = _sigmoid(ba)
    g = -jnp.exp(al_ref[...]) * _softplus(ba + dtb_ref[...])
    lane = lax.broadcasted_iota(jnp.int32, ba.shape, 1)
    bga_ref[...] = jnp.where(lane < n_heads, beta, g)


def _gdn_qkv_chunk(conv, c0, q_ref, k_ref, v_ref, *, n_heads, dk):
    kdim = n_heads * dk
    act = _silu(conv)
    for i in range(conv.shape[1] // dk):
        t = act[:, i * dk:(i + 1) * dk]
        ch = c0 + i * dk
        if ch < 2 * kdim:
            t = t * lax.rsqrt(jnp.sum(t * t, axis=-1, keepdims=True) + EPS)
        if ch < kdim:
            q_ref[:, ch:ch + dk] = t * (dk ** -0.5)
        elif ch < 2 * kdim:
            k_ref[:, ch - kdim:ch - kdim + dk] = t
        else:
            v_ref[:, ch - 2 * kdim:ch - 2 * kdim + dk] = t


def _gdn_proj_seq_kernel(x_ref, g_ref, wqkv_ref, wz_ref, wba_ref, cw_ref, al_ref, dtb_ref,
                         q_ref, k_ref, v_ref, z_ref, bga_ref, tail_ref, buf_ref,
                         *, tm, tiles_per_seq, n_heads, dk, cwid):
    xn = _rms(x_ref[...], g_ref[...], EPS).astype(BF16)
    first = (pl.program_id(0) % tiles_per_seq) == 0

    @pl.when(first)
    def _():
        buf_ref[0:HIST_ROWS, :] = jnp.zeros((HIST_ROWS, buf_ref.shape[1]), buf_ref.dtype)

    @pl.when(jnp.logical_not(first))
    def _():
        buf_ref[0:HIST_ROWS, :] = buf_ref[tm:tm + HIST_ROWS, :]

    cw = cw_ref[...]
    width = cw.shape[0]
    n_chunks = wqkv_ref.shape[1] // cwid
    mm = lambda c: _dot(xn, wqkv_ref[:, c * cwid:(c + 1) * cwid])
    h_next = mm(0)
    for c in range(n_chunks):
        cs = slice(c * cwid, (c + 1) * cwid)
        hm = h_next
        if c + 1 < n_chunks:
            h_next = mm(c + 1)
        buf_ref[HIST_ROWS:HIST_ROWS + tm, cs] = hm
        conv = cw[width - 1:width, cs] * hm
        for j in range(width - 1):
            conv = conv + cw[j:j + 1, cs] * buf_ref[pl.ds(HIST_ROWS - (width - 1) + j, tm), cs]
        _gdn_qkv_chunk(conv, c * cwid, q_ref, k_ref, v_ref, n_heads=n_heads, dk=dk)
    tail_ref[0] = buf_ref[tm:tm + HIST_ROWS, :]
    _gdn_gates(xn, wz_ref, wba_ref, al_ref, dtb_ref, z_ref, bga_ref, n_heads=n_heads)


def _gdn_proj_step_kernel(x_ref, g_ref, wqkv_ref, wz_ref, wba_ref, cw_ref, al_ref, dtb_ref, *rest,
                          n_heads, dk, n_hist):
    hist_refs = rest[:n_hist]
    q_ref, k_ref, v_ref, z_ref, bga_ref, raw_ref = rest[n_hist:]
    xn = _rms(x_ref[...], g_ref[...], EPS).astype(BF16)
    hm = _dot(xn, wqkv_ref[...])
    raw_ref[...] = hm
    cw = cw_ref[...]
    width = cw.shape[0]
    conv = cw[width - 1:width] * hm
    for j in range(width - 1):
        conv = conv + cw[j:j + 1] * hist_refs[j][...]
    _gdn_qkv_chunk(conv, 0, q_ref, k_ref, v_ref, n_heads=n_heads, dk=dk)
    _gdn_gates(xn, wz_ref, wba_ref, al_ref, dtb_ref, z_ref, bga_ref, n_heads=n_heads)


def _gdn_proj(x, g, wqkv, wz, wba, conv_w, al_pad, dtb_pad, *, tm, n_heads, dk, seq=None, hist=None):
    t, d = x.shape
    qkv_c = wqkv.shape[1]
    vdim = wz.shape[1]
    row = lambda width: pl.BlockSpec((tm, width), lambda i: (i, 0))
    common = [row(d), _const_spec((1, d)), _const_spec(wqkv.shape), _const_spec(wz.shape),
              _const_spec(wba.shape), _const_spec(conv_w.shape), _const_spec(al_pad.shape),
              _const_spec(dtb_pad.shape)]
    outs = [row(vdim), row(vdim), row(vdim), row(vdim), row(LANES)]
    out_shapes = ([jax.ShapeDtypeStruct((t, vdim), F32)] * 3 + [jax.ShapeDtypeStruct((t, vdim), BF16)]
                  + [jax.ShapeDtypeStruct((t, LANES), F32)])
    if hist is None:
        tps = seq // tm
        kern = functools.partial(_gdn_proj_seq_kernel, tm=tm, tiles_per_seq=tps, n_heads=n_heads, dk=dk,
                                 cwid=2 * dk)
        extra, extra_specs = [], []
        outs.append(pl.BlockSpec((1, HIST_ROWS, qkv_c), lambda i: (i // tps, 0, 0)))
        out_shapes.append(jax.ShapeDtypeStruct((t // seq, HIST_ROWS, qkv_c), F32))
        scratch = [pltpu.VMEM((tm + HIST_ROWS, qkv_c), F32)]
    else:
        hist_arr, n_hist = hist
        kern = functools.partial(_gdn_proj_step_kernel, n_heads=n_heads, dk=dk, n_hist=n_hist)
        extra = [hist_arr] * n_hist
        extra_specs = [pl.BlockSpec((tm, qkv_c), functools.partial(lambda i, j: (i, j), j=j))
                       for j in range(n_hist)]
        outs.append(row(qkv_c))
        out_shapes.append(jax.ShapeDtypeStruct((t, qkv_c), F32))
        scratch = []
    return pl.pallas_call(
        kern,
        grid=(t // tm,),
        in_specs=common + extra_specs,
        out_specs=outs,
        out_shape=out_shapes,
        scratch_shapes=scratch,
        compiler_params=_cparams("arbitrary"),
        name="gdn_proj",
    )(x, g, wqkv, wz, wba, conv_w, al_pad, dtb_pad, *extra)


def _pair_blockdiag(a, b):
    z = jnp.zeros_like(a)
    return jnp.concatenate([jnp.concatenate([a, z], axis=1), jnp.concatenate([z, b], axis=1)], axis=0)


def _gdn_prep_kernel(q_ref, k_ref, v_ref, bga_ref, u_ref, w_ref, qe_ref, ke_ref, qk_ref, gc_ref,
                     *, n_heads, dk, chunk, n_sub):
    n_pairs = n_heads // 2
    rows_blk = chunk * n_sub
    bga = bga_ref[...]
    r = lax.broadcasted_iota(jnp.int32, (rows_blk, rows_blk), 0)
    c = lax.broadcasted_iota(jnp.int32, (rows_blk, rows_blk), 1)
    sh = chunk.bit_length() - 1
    tril = jnp.where(jnp.logical_and(r >= c, (r >> sh) == (c >> sh)), 1.0, 0.0).astype(BF16)
    b3 = _split3(bga)
    gc = _dot(tril, b3[0]) + (_dot(tril, b3[1]) + _dot(tril, b3[2]))
    gc_ref[...] = gc
    lane = lax.broadcasted_iota(jnp.int32, (rows_blk, LANES), 1)
    is_g = jnp.logical_and(lane >= n_heads, lane < 2 * n_heads)
    odd = ((lane - n_heads) & 1) == 1
    gce = jnp.where(jnp.logical_and(is_g, jnp.logical_not(odd)), gc, 0.0)
    gco = jnp.where(jnp.logical_and(is_g, odd), gc, 0.0)
    ar = lax.broadcasted_iota(jnp.int32, (BF16_ROWS, LANES), 0)
    al = lax.broadcasted_iota(jnp.int32, (BF16_ROWS, LANES), 1)
    asel = jnp.where(jnp.logical_and(jnp.logical_and(al >= n_heads, ar < n_pairs),
                                     ((al - n_heads) >> 1) == ar), 1.0, 0.0).astype(BF16)
    prow = lax.broadcasted_iota(jnp.int32, (chunk, 2 * chunk), 0)
    plane = lax.broadcasted_iota(jnp.int32, (chunk, 2 * chunk), 1)
    pcol = plane & (chunk - 1)
    lo_half = plane < chunk
    incl = prow >= pcol
    strict = prow > pcol

    def bd(m):
        return jnp.concatenate([jnp.where(lo_half, m, 0.0), jnp.where(lo_half, 0.0, m)], axis=0).astype(BF16)

    items = [(ci, p) for ci in range(n_sub) for p in range(n_pairs)]
    lmat, tinv, vbn, kbe = {}, {}, {}, {}
    for ci, p in items:
        rs = slice(ci * chunk, (ci + 1) * chunk)
        cs = slice(2 * p * dk, 2 * (p + 1) * dk)
        g3 = _split3(jnp.concatenate([gce[rs], gco[rs]], axis=0))
        grow = _dot_nt(asel, g3[0]) + (_dot_nt(asel, g3[1]) + _dot_nt(asel, g3[2]))
        k2, q2, v2 = k_ref[rs, cs], q_ref[rs, cs], v_ref[rs, cs]
        ha, hb = 2 * p, 2 * p + 1

        def nat(a, b, rows=chunk):
            return jnp.concatenate([jnp.broadcast_to(a, (rows, dk)), jnp.broadcast_to(b, (rows, dk))], axis=1)

        gca, gcb = gc[rs, n_heads + ha:n_heads + ha + 1], gc[rs, n_heads + hb:n_heads + hb + 1]
        bnat = nat(bga[rs, ha:ha + 1], bga[rs, hb:hb + 1])
        gnat = nat(gca, gcb)
        glast = nat(gca[chunk - 1:chunk], gcb[chunk - 1:chunk], rows=1)
        egn = jnp.exp(gnat)
        kb2 = k2 * bnat
        k2b = k2.astype(BF16)
        rhs_t = _pair_blockdiag(k2b[:, :dk], k2b[:, dk:])
        decay = jnp.where(incl, jnp.exp(jnp.where(incl, jnp.where(lo_half, gca, gcb) - grow[p:p + 1], 0.0)), 0.0)
        lm = jnp.where(strict, _dot_nt(kb2.astype(BF16), rhs_t) * decay, 0.0)
        qk_ref[rs, 2 * p * chunk:2 * (p + 1) * chunk] = jnp.where(
            incl, _dot_nt(q2.astype(BF16), rhs_t) * decay, 0.0).astype(BF16)
        qe_ref[rs, cs] = (q2 * egn).astype(BF16)
        ke_ref[rs, cs] = (k2 * jnp.exp(glast - gnat)).astype(BF16)
        lmat[ci, p] = lm
        tinv[ci, p] = jnp.where(prow == pcol, 1.0, 0.0) - jnp.where((prow >> 1) == (pcol >> 1), lm, 0.0)
        vbn[ci, p] = (v2 * bnat).astype(BF16)
        kbe[ci, p] = (kb2 * egn).astype(BF16)
    s = 1
    while 2 * (1 << s) <= chunk:
        joins = jnp.logical_and((prow >> (s + 1)) == (pcol >> (s + 1)), (prow >> s) != (pcol >> s))
        x = {it: _dot(tinv[it].astype(BF16), bd(jnp.where(joins, lmat[it], 0.0))) for it in items}
        tinv = {it: tinv[it] - _dot(x[it].astype(BF16), bd(tinv[it])) for it in items}
        s += 1
    for ci, p in items:
        rs = slice(ci * chunk, (ci + 1) * chunk)
        cs = slice(2 * p * dk, 2 * (p + 1) * dk)
        tb = tinv[ci, p].astype(BF16)
        u_ref[rs, cs] = _dot(tb, _pair_blockdiag(vbn[ci, p][:, :dk], vbn[ci, p][:, dk:]))
        w_ref[rs, cs] = _dot(tb, _pair_blockdiag(kbe[ci, p][:, :dk], kbe[ci, p][:, dk:])).astype(BF16)


def _gdn_prep(q, k, v, bga, *, n_heads, dk, n_sub):
    t, vdim = q.shape
    chunk = GDN_CHUNK
    assert 2 * chunk == LANES and n_heads % 2 == 0 and 2 * n_heads <= LANES
    rows = chunk * n_sub
    qk_w = n_heads * chunk
    row = lambda width: pl.BlockSpec((rows, width), lambda i: (i, 0))
    kern = functools.partial(_gdn_prep_kernel, n_heads=n_heads, dk=dk, chunk=chunk, n_sub=n_sub)
    return pl.pallas_call(
        kern,
        grid=(t // rows,),
        in_specs=[row(vdim), row(vdim), row(vdim), row(LANES)],
        out_specs=[row(vdim), row(vdim), row(vdim), row(vdim), row(qk_w), row(LANES)],
        out_shape=[jax.ShapeDtypeStruct((t, vdim), F32)] + [jax.ShapeDtypeStruct((t, vdim), BF16)] * 3
                  + [jax.ShapeDtypeStruct((t, qk_w), BF16), jax.ShapeDtypeStruct((t, LANES), F32)],
        compiler_params=_cparams("arbitrary"),
        name="gdn_prep",
    )(q, k, v, bga)


def _gdn_scan_kernel(u_ref, w_ref, qe_ref, ke_ref, qk_ref, gc_ref, o_ref, sout_ref, s_ref,
                     *, n_heads, dk, chunk, n_sub, nbat):
    j = pl.program_id(1)

    @pl.when(j == 0)
    def _():
        s_ref[...] = jnp.zeros(s_ref.shape, s_ref.dtype)

    sls = [slice(h * dk, (h + 1) * dk) for h in range(n_heads)]
    items = [(bi, h) for bi in range(nbat) for h in range(n_heads)]

    def body(c, carry):
        r0 = pl.multiple_of(c * chunk, chunk)
        rows = pl.ds(r0, chunk)
        st = {it: s_ref[it[0], it[1]] for it in items}
        sb = {it: st[it].astype(BF16) for it in items}
        ws = {(bi, h): _dot(jnp.concatenate([w_ref[bi, rows, sls[h]], qe_ref[bi, rows, sls[h]]], axis=0),
                            sb[bi, h]) for bi, h in items}
        vb = {(bi, h): (u_ref[bi, rows, sls[h]] - ws[bi, h][:chunk]).astype(BF16) for bi, h in items}
        for bi in range(nbat):
            for p in range(n_heads // 2):
                ha, hb = 2 * p, 2 * p + 1
                intra = _dot(qk_ref[bi, rows, 2 * p * chunk:2 * (p + 1) * chunk],
                             _pair_blockdiag(vb[bi, ha], vb[bi, hb]))
                o_ref[bi, rows, 2 * p * dk:2 * (p + 1) * dk] = (
                    jnp.concatenate([ws[bi, ha][chunk:], ws[bi, hb][chunk:]], axis=1) + intra)
        for bi in range(nbat):
            eg = jnp.exp(gc_ref[bi, pl.ds(r0 + chunk - 1, 1), :])
            for h in range(n_heads):
                s_ref[bi, h] = (st[bi, h] * eg[:, n_heads + h:n_heads + h + 1]
                                + _dot_tn(ke_ref[bi, rows, sls[h]], vb[bi, h]))
        return carry

    lax.fori_loop(0, n_sub, body, 0)

    @pl.when(j == pl.num_programs(1) - 1)
    def _():
        sout_ref[...] = s_ref[...]


def _gdn_scan(u, w, qe, ke, qk, gc, *, batch, seq, n_heads, dk, rows, nbat):
    t, vdim = u.shape
    chunk = GDN_CHUNK
    seq3 = lambda a: a.reshape(batch, seq, a.shape[1])
    row = lambda width: pl.BlockSpec((nbat, rows, width), lambda b, j: (b, j, 0))
    kern = functools.partial(_gdn_scan_kernel, n_heads=n_heads, dk=dk, chunk=chunk, n_sub=rows // chunk,
                             nbat=nbat)
    o, s_new = pl.pallas_call(
        kern,
        grid=(batch // nbat, seq // rows),
        in_specs=[row(vdim)] * 4 + [row(qk.shape[1]), row(LANES)],
        out_specs=[row(vdim), pl.BlockSpec((nbat, n_heads, dk, dk), lambda b, j: (b, 0, 0, 0))],
        out_shape=[jax.ShapeDtypeStruct((batch, seq, vdim), F32),
                   jax.ShapeDtypeStruct((batch, n_heads, dk, dk), F32)],
        scratch_shapes=[pltpu.VMEM((nbat, n_heads, dk, dk), F32)],
        compiler_params=_cparams("arbitrary", "arbitrary"),
        name="gdn_scan",
    )(seq3(u), seq3(w), seq3(qe), seq3(ke), seq3(qk), seq3(gc))
    return o.reshape(t, vdim), s_new


def _gdn_step_kernel(q_ref, k_ref, v_ref, bga_ref, s_ref, prev_ref, o_ref, sout_ref, *, n_heads, dk, nb):
    del prev_ref
    s_ref, sout_ref = s_ref.at[0], sout_ref.at[0]
    pad = jnp.zeros((dk - n_heads, dk), F32)

    def body(b, carry):
        qt = jnp.concatenate([q_ref[b], pad], axis=0).T
        kt = jnp.concatenate([k_ref[b], pad], axis=0).T
        vb = v_ref[b]
        bg = bga_ref[pl.ds(b, 1), :]
        eg = jnp.exp(bg)
        rows = []
        for h in range(n_heads):
            st = s_ref[b, h] * eg[:, n_heads + h:n_heads + h + 1]
            kc = kt[:, h:h + 1]
            ks = jnp.sum(kc * st, axis=0, keepdims=True)
            delta = (vb[h:h + 1, :] - ks) * bg[:, h:h + 1]
            st = st + kc * delta
            sout_ref[b, h] = st
            rows.append(jnp.sum(qt[:, h:h + 1] * st, axis=0, keepdims=True))
        o_ref[b] = jnp.concatenate(rows, axis=0)
        return carry

    for b in range(nb):
        body(b, 0)


def _gdn_step(q, k, v, bga, states, new_states, layer, *, n_heads, dk, nb):
    b = q.shape[0]
    q3, k3, v3 = (a.reshape(b, n_heads, dk) for a in (q, k, v))
    vec = pl.BlockSpec((nb, n_heads, dk), lambda i: (i, 0, 0))
    st = pl.BlockSpec((1, nb, n_heads, dk, dk), lambda i: (layer, i, 0, 0, 0))
    kern = functools.partial(_gdn_step_kernel, n_heads=n_heads, dk=dk, nb=nb)
    o, s_new = pl.pallas_call(
        kern,
        grid=(b // nb,),
        in_specs=[vec, vec, vec, pl.BlockSpec((nb, LANES), lambda i: (i, 0)),
                  st, pl.BlockSpec(memory_space=pl.ANY)],
        out_specs=[vec, st],
        input_output_aliases={5: 1},
        out_shape=[jax.ShapeDtypeStruct((b, n_heads, dk), F32), jax.ShapeDtypeStruct(new_states.shape, F32)],
        compiler_params=_cparams("arbitrary"),
        name="gdn_step",
    )(q3, k3, v3, bga, states, new_states)
    return o.reshape(b, n_heads * dk), s_new


def _gdn_tail(o_ref, z_ref, nw_ref, w_ref, *, n_heads, dk):
    nw = nw_ref[...]
    parts = []
    for h in range(n_heads):
        sl = slice(h * dk, (h + 1) * dk)
        parts.append((_rms(o_ref[:, sl], nw, EPS) * _silu(z_ref[:, sl].astype(F32))).astype(BF16))
    return _dot(jnp.concatenate(parts, axis=1), w_ref[...])


def _gdn_out_kernel(o_ref, z_ref, x_ref, nw_ref, w_ref, y_ref, *, n_heads, dk):
    y_ref[...] = x_ref[...] + _gdn_tail(o_ref, z_ref, nw_ref, w_ref, n_heads=n_heads, dk=dk)


def _gdn_out(o, z, x, norm_w, w_out, *, tm, n_heads, dk):
    t, d = x.shape
    vdim = o.shape[1]
    row = lambda width: pl.BlockSpec((tm, width), lambda i: (i, 0))
    kern = functools.partial(_gdn_out_kernel, n_heads=n_heads, dk=dk)
    return pl.pallas_call(
        kern,
        grid=(t // tm,),
        in_specs=[row(vdim), row(vdim), row(d), _const_spec(norm_w.shape), _const_spec(w_out.shape)],
        out_specs=row(d),
        out_shape=jax.ShapeDtypeStruct((t, d), F32),
        compiler_params=_cparams("arbitrary"),
        name="gdn_out",
    )(o, z, x, norm_w, w_out)


def _ffn_tail(x, conv, up, cb_ref, wd_ref, gf_ref, o_ref, final_norm):
    act = (_silu(conv + cb_ref[...]) * up).astype(BF16)
    y = x + _dot(act, wd_ref[...])
    if final_norm:
        y = _rms(y, gf_ref[...], EPS)
    o_ref[...] = y


def _ffn_seq_kernel(x_ref, g_ref, wg_ref, wu_ref, cw_ref, cb_ref, wd_ref, gf_ref, *rest,
                    tm, tiles_per_seq, final_norm, cwid, mix):
    x = x_ref[...]
    if mix[0] == "gdn":
        o_in, z_in, nw_ref, wo_ref, o_ref, tail_ref, buf_ref, act_ref = rest
        x = x + _gdn_tail(o_in, z_in, nw_ref, wo_ref, n_heads=mix[1], dk=mix[2])
    else:
        att_in, bg_in, cx_in, scw_ref, wo_ref, o_ref, tail_ref, buf_ref, act_ref, cbuf_ref = rest
        x = x + _even_tail_seq(att_in, bg_in, cx_in, scw_ref, wo_ref, cbuf_ref, tm=tm,
                               tiles_per_seq=tiles_per_seq)
    xn = _rms(x, g_ref[...], EPS).astype(BF16)
    first = (pl.program_id(0) % tiles_per_seq) == 0

    @pl.when(first)
    def _():
        buf_ref[0:HIST_ROWS, :] = jnp.zeros((HIST_ROWS, buf_ref.shape[1]), buf_ref.dtype)

    @pl.when(jnp.logical_not(first))
    def _():
        buf_ref[0:HIST_ROWS, :] = buf_ref[tm:tm + HIST_ROWS, :]

    cw = cw_ref[...]
    cb = cb_ref[...]
    width = cw.shape[0]
    n_chunks = wg_ref.shape[1] // cwid

    def mm(c):
        cs = slice(c * cwid, (c + 1) * cwid)
        return _dot(xn, wg_ref[:, cs]), _dot(xn, wu_ref[:, cs])

    nxt = mm(0)
    for c in range(n_chunks):
        cs = slice(c * cwid, (c + 1) * cwid)
        gate, up = nxt
        if c + 1 < n_chunks:
            nxt = mm(c + 1)
        buf_ref[HIST_ROWS:HIST_ROWS + tm, cs] = gate
        conv = cw[width - 1:width, cs] * gate
        for j in range(width - 1):
            conv = conv + cw[j:j + 1, cs] * buf_ref[pl.ds(HIST_ROWS - (width - 1) + j, tm), cs]
        act_ref[:, cs] = (_silu(conv + cb[:, cs]) * up).astype(BF16)
    tail_ref[0] = buf_ref[tm:tm + HIST_ROWS, :]
    y = x + _dot(act_ref[...], wd_ref[...])
    if final_norm:
        y = _rms(y, gf_ref[...], EPS)
    o_ref[...] = y


def _ffn_step_kernel(x_ref, g_ref, wg_ref, wu_ref, cw_ref, cb_ref, wd_ref, gf_ref, *rest,
                     n_hist, final_norm):
    hist_refs = rest[:n_hist]
    o_ref, raw_ref = rest[n_hist:]
    x = x_ref[...]
    xn = _rms(x, g_ref[...], EPS).astype(BF16)
    gate = _dot(xn, wg_ref[...])
    raw_ref[...] = gate
    cw = cw_ref[...]
    width = cw.shape[0]
    conv = cw[width - 1:width] * gate
    for j in range(width - 1):
        conv = conv + cw[j:j + 1] * hist_refs[j][...]
    _ffn_tail(x, conv, _dot(xn, wu_ref[...]), cb_ref, wd_ref, gf_ref, o_ref, final_norm)


def _ffn(x, g, wg, wu, conv_w, conv_b, wd, gf, *, tm, final_norm, seq=None, hist=None, mix=None):
    t, d = x.shape
    d_ff = wg.shape[1]
    row = lambda width: pl.BlockSpec((tm, width), lambda i: (i, 0))
    common = [row(d), _const_spec((1, d)), _const_spec(wg.shape), _const_spec(wu.shape),
              _const_spec(conv_w.shape), _const_spec(conv_b.shape), _const_spec(wd.shape),
              _const_spec((1, d))]
    if hist is None:
        tps = seq // tm
        assert d_ff % (2 * LANES) == 0
        scratch = [pltpu.VMEM((tm + HIST_ROWS, d_ff), F32), pltpu.VMEM((tm, d_ff), BF16)]
        if mix[0] == "gdn":
            _, o_in, z_in, nw, w_out, n_heads, dk = mix
            extra = [o_in, z_in, nw, w_out]
            extra_specs = [row(o_in.shape[1]), row(z_in.shape[1]), _const_spec(nw.shape), _const_spec(w_out.shape)]
            mix_static = ("gdn", n_heads, dk)
        else:
            _, att, bg, cx, scw, w_out = mix
            extra = [att, bg, cx, scw, w_out]
            extra_specs = [row(att.shape[1]), row(bg.shape[1]), row(cx.shape[1]), _const_spec(scw.shape),
                           _const_spec(w_out.shape)]
            scratch.append(pltpu.VMEM((tm + HIST_ROWS, cx.shape[1]), F32))
            mix_static = ("even",)
        kern = functools.partial(_ffn_seq_kernel, tm=tm, tiles_per_seq=tps, final_norm=final_norm,
                                 cwid=2 * LANES, mix=mix_static)
        outs = [row(d), pl.BlockSpec((1, HIST_ROWS, d_ff), lambda i: (i // tps, 0, 0))]
        out_shapes = [jax.ShapeDtypeStruct((t, d), F32),
                      jax.ShapeDtypeStruct((t // seq, HIST_ROWS, d_ff), F32)]
    else:
        assert mix is None
        hist_arr, n_hist = hist
        kern = functools.partial(_ffn_step_kernel, n_hist=n_hist, final_norm=final_norm)
        extra = [hist_arr] * n_hist
        extra_specs = [pl.BlockSpec((tm, d_ff), functools.partial(lambda i, j: (i, j), j=j))
                       for j in range(n_hist)]
        outs = [row(d), row(d_ff)]
        out_shapes = [jax.ShapeDtypeStruct((t, d), F32), jax.ShapeDtypeStruct((t, d_ff), F32)]
        scratch = []
    return pl.pallas_call(
        kern,
        grid=(t // tm,),
        in_specs=common + extra_specs,
        out_specs=outs,
        out_shape=out_shapes,
        scratch_shapes=scratch,
        compiler_params=_cparams("arbitrary"),
        name="conv_ffn",
    )(x, g, wg, wu, conv_w, conv_b, wd, gf, *extra)


def _tile(n, pref):
    t = min(n, pref)
    while n % t:
        t //= 2
    return t


def kernel(x_prompt, x_sample, cache_k, cache_v, state_sconv, state_gdn_conv, state_gdn, state_ffn_conv, page_table, norm_mix, norm_ffn, norm_final, w_in_even, w_out_even, lambda_q1, lambda_k1, lambda_q2, lambda_k2, subln_w, sconv_w, w_in_odd, gdn_conv_w, a_log, dt_bias, gdn_norm_w, w_out_odd, w_up, ffn_conv_w, ffn_conv_b, w_down):
    batch, seq, d = x_prompt.shape
    dec_b = x_sample.shape[0]
    depth = norm_mix.shape[0]
    n_heads_a, e_a = cache_k.shape[3], cache_k.shape[4]
    n_att = n_heads_a * e_a
    rot = (e_a // 2) // 4
    w_b = state_sconv.shape[-1]
    sconv = sconv_w.shape[1]
    page = cache_k.shape[2]
    past_len = page_table.shape[1] * page
    n_heads_c, dk = state_gdn.shape[2], state_gdn.shape[3]
    kdim = n_heads_c * dk
    qkv_c = gdn_conv_w.shape[-1]
    gdn_conv = gdn_conv_w.shape[1]
    d_ff = ffn_conv_w.shape[-1]
    ffn_conv = ffn_conv_w.shape[1]
    t_p = batch * seq

    tm_p = _tile(seq, 512)
    blk = _tile(seq, 512)
    scan_rows = _tile(seq, 256)

    xp = x_prompt.reshape(t_p, d)
    xs = x_sample.reshape(dec_b, d)
    rope_p = _rope_tables(jnp.arange(seq), e_a, rot)
    rope_s = _rope_tables(jnp.full((dec_b,), past_len), e_a, rot)
    gf = norm_final.reshape(1, d)

    ks, vs, scp, scs = [], [], [], []
    gcp, gcs, gsp, fcp, fcs = [], [], [], [], []
    kp = jnp.zeros((w_in_even.shape[0], t_p * n_heads_a, e_a), F32)
    vp = jnp.zeros_like(kp)
    gss = jnp.zeros(state_gdn.shape, F32)
    for l in range(depth):
        gm = norm_mix[l].reshape(1, d)
        if l % 2 == 0:
            e = l // 2
            lam_init = 0.8 - 0.6 * math.exp(-0.3 * l)
            w_in = w_in_even[e].astype(BF16)
            w_out = w_out_even[e].astype(BF16)
            lam_p = jnp.stack([lambda_q1[e], lambda_k1[e], lambda_q2[e], lambda_k2[e]])
            sw = subln_w[e].reshape(1, e_a)
            proj = functools.partial(_even_proj, n_heads=n_heads_a, e_a=e_a, w_b=w_b, half=rot // 2)
            q, kp, vp, kb, vt, bg, cx = proj(xp, gm, w_in, rope_p, tm=tm_p, stack=(kp, vp, e))
            att = _attn_prompt(q, kb, vt, lam_p, sw, batch=batch, seq=seq, n_heads=n_heads_a, e_a=e_a,
                               blk=blk, lam_init=lam_init)
            mix_p = ("even", att, bg, cx, sconv_w[e], w_out)
            cx3 = cx.reshape(batch, seq, w_b)
            if seq >= sconv - 1:
                scp.append(cx3[:, seq - (sconv - 1):])
            else:
                scp.append(jnp.concatenate([jnp.zeros((batch, sconv - 1 - seq, w_b), F32), cx3], axis=1))
            q, k, v, kb, vt, bg, cx = proj(xs, gm, w_in, rope_s, tm=dec_b)
            att = _attn_sample(q, k, v, cache_k, cache_v, e, page_table, lam_p, sw,
                               n_heads=n_heads_a, e_a=e_a, lam_init=lam_init)
            hist = [state_sconv[e, :, j] for j in range(sconv - 1)]
            xs = _even_out_step(att, bg, cx, xs, sconv_w[e], w_out, hist, tm=dec_b)
            ks.append(k.reshape(dec_b, 1, n_heads_a, e_a))
            vs.append(v.reshape(dec_b, 1, n_heads_a, e_a))
            scs.append(jnp.concatenate([state_sconv[e, :, 1:], cx[:, None, :]], axis=1))
        else:
            o = l // 2
            w_in = w_in_odd[o]
            wqkv = w_in[:, :qkv_c].astype(BF16)
            wz = w_in[:, qkv_c:qkv_c + kdim].astype(BF16)
            wba = jnp.pad(w_in[:, qkv_c + kdim:], ((0, 0), (0, LANES - 2 * n_heads_c))).astype(BF16)
            w_out = w_out_odd[o].astype(BF16)
            lane_pad = lambda a: jnp.pad(a, (n_heads_c, LANES - 2 * n_heads_c)).reshape(1, LANES)
            al_pad, dtb_pad = lane_pad(a_log[o]), lane_pad(dt_bias[o])
            nw = gdn_norm_w[o].reshape(1, dk)
            proj = functools.partial(_gdn_proj, n_heads=n_heads_c, dk=dk)
            q, k, v, z, bga, tail = proj(xp, gm, wqkv, wz, wba, gdn_conv_w[o], al_pad, dtb_pad,
                                         tm=tm_p, seq=seq)
            u, w, qe, ke, qk, gc = _gdn_prep(q, k, v, bga, n_heads=n_heads_c, dk=dk,
                                             n_sub=_tile(seq // GDN_CHUNK, 4))
            og, s_new = _gdn_scan(u, w, qe, ke, qk, gc, batch=batch, seq=seq, n_heads=n_heads_c, dk=dk,
                                  rows=scan_rows, nbat=_tile(batch, 4))
            mix_p = ("gdn", og, z, nw, w_out, n_heads_c, dk)
            gcp.append(tail[:, HIST_ROWS - (gdn_conv - 1):])
            gsp.append(s_new)
            hist_arr = state_gdn_conv[o].reshape(dec_b, (gdn_conv - 1) * qkv_c)
            q, k, v, z, bga, raw = proj(xs, gm, wqkv, wz, wba, gdn_conv_w[o], al_pad, dtb_pad,
                                        tm=dec_b, hist=(hist_arr, gdn_conv - 1))
            og, gss = _gdn_step(q, k, v, bga, state_gdn, gss, o, n_heads=n_heads_c, dk=dk, nb=_tile(dec_b, 8))
            xs = _gdn_out(og, z, xs, nw, w_out, tm=dec_b, n_heads=n_heads_c, dk=dk)
            gcs.append(jnp.concatenate([state_gdn_conv[o, :, 1:], raw[:, None, :]], axis=1))
        gn = norm_ffn[l].reshape(1, d)
        wg = w_up[l, :, :d_ff].astype(BF16)
        wu = w_up[l, :, d_ff:].astype(BF16)
        wd = w_down[l].astype(BF16)
        cb = ffn_conv_b[l].reshape(1, d_ff)
        last = l == depth - 1
        xp, tail = _ffn(xp, gn, wg, wu, ffn_conv_w[l], cb, wd, gf, tm=tm_p, final_norm=last, seq=seq,
                        mix=mix_p)
        fcp.append(tail[:, HIST_ROWS - (ffn_conv - 1):])
        hist_arr = state_ffn_conv[l].reshape(dec_b, (ffn_conv - 1) * d_ff)
        xs, raw = _ffn(xs, gn, wg, wu, ffn_conv_w[l], cb, wd, gf, tm=dec_b, final_norm=last,
                       hist=(hist_arr, ffn_conv - 1))
        fcs.append(jnp.concatenate([state_ffn_conv[l, :, 1:], raw[:, None, :]], axis=1))
    return (xp.reshape(batch, seq, d), xs.reshape(dec_b, 1, d),
            kp.reshape(-1, batch, seq, n_heads_a, e_a), vp.reshape(-1, batch, seq, n_heads_a, e_a),
            jnp.stack(ks), jnp.stack(vs),
            jnp.stack(scp), jnp.stack(scs),
            jnp.stack(gcp), jnp.stack(gcs), jnp.stack(gsp), gss,
            jnp.stack(fcp), jnp.stack(fcs))
```

```python
import functools
import math

import jax
import jax.numpy as jnp
from jax import lax
from jax.experimental import pallas as pl
from jax.experimental.pallas import tpu as pltpu

F32 = jnp.float32
BF16 = jnp.bfloat16

EPS = 1e-6
SUBLN_EPS = 1e-5
ROPE_THETA = 500000.0
GDN_CHUNK = 64
V7X_VMEM_LIMIT = 56 * 1024 * 1024
LANES = 128
HIST_ROWS = 8
BF16_ROWS = 16
LOG2E = math.log2(math.e)
VT_PAD = 16
ATTN_HEADS = 4
ATTN_SUB = 2


def _cparams(*sem):
    return pltpu.CompilerParams(dimension_semantics=sem, vmem_limit_bytes=V7X_VMEM_LIMIT)


def _const_spec(shape):
    nd = len(shape)
    return pl.BlockSpec(shape, lambda i: (0,) * nd, pipeline_mode=pl.Buffered(1))


def _rms(x, g, eps):
    return x * lax.rsqrt(jnp.mean(x * x, axis=-1, keepdims=True) + eps) * g


def _sigmoid(x):
    return 0.5 * jnp.tanh(0.5 * x) + 0.5


def _silu(x):
    h = 0.5 * x
    return h * jnp.tanh(h) + h


def _softplus(x):
    return jnp.maximum(x, 0.0) + jnp.log1p(jnp.exp(-jnp.abs(x)))


def _dot(a, b):
    return jnp.dot(a, b, preferred_element_type=F32)


def _dot_nt(a, b):
    return lax.dot_general(a, b, (((1,), (1,)), ((), ())), preferred_element_type=F32)


def _dot_tn(a, b):
    return lax.dot_general(a, b, (((0,), (0,)), ((), ())), preferred_element_type=F32)


def _split3(a):
    hi = a.astype(BF16)
    r = a - hi.astype(F32)
    mid = r.astype(BF16)
    lo = (r - mid.astype(F32)).astype(BF16)
    return hi, mid, lo


def _even_proj_kernel(x_ref, g_ref, w_ref, c_ref, s1_ref, s2_ref, *rest,
                      tm, n_heads, e_a, w_b, half, qscale, stacked):
    if stacked:
        q_ref, k_ref, v_ref, kb_ref, vt_ref, bg_ref, cx_ref = rest[2:]
        k_ref, v_ref = k_ref.at[0], v_ref.at[0]
    else:
        q_ref, k_ref, v_ref, kb_ref, vt_ref, bg_ref, cx_ref = rest
    xn = _rms(x_ref[...], g_ref[...], EPS).astype(BF16)
    n = n_heads * e_a
    c, s1, s2 = c_ref[...], s1_ref[...], s2_ref[...]

    def rope(t):
        return t * c + pltpu.roll(t, e_a - half, 1) * s1 + pltpu.roll(t, half, 1) * s2

    hqk = _dot(xn, w_ref[:, :2 * n])
    for h in range(n_heads):
        sl = slice(h * e_a, (h + 1) * e_a)
        q_ref[:, sl] = (rope(hqk[:, sl]) * qscale).astype(BF16)
        kr = rope(hqk[:, n + h * e_a:n + (h + 1) * e_a])
        k_ref[pl.ds(h, tm, stride=n_heads), :] = kr
        kb_ref[:, sl] = kr.astype(BF16)
    hv = _dot(xn, w_ref[:, 2 * n:3 * n])
    for h in range(n_heads):
        v_ref[pl.ds(h, tm, stride=n_heads), :] = hv[:, h * e_a:(h + 1) * e_a]
    hvt = hv.T.astype(BF16)
    ones_row = jnp.where(lax.broadcasted_iota(jnp.int32, (VT_PAD, tm), 0) == 0, 1.0, 0.0).astype(BF16)
    for h in range(n_heads):
        r0 = h * (e_a + VT_PAD)
        vt_ref[r0:r0 + e_a, :] = hvt[h * e_a:(h + 1) * e_a, :]
        vt_ref[r0 + e_a:r0 + e_a + VT_PAD, :] = ones_row
    o = 3 * n
    bg_ref[...] = _dot(xn, w_ref[:, o:o + w_b])
    hc = _dot(xn, w_ref[:, o + w_b:o + 3 * w_b])
    cx_ref[...] = hc[:, :w_b] * hc[:, w_b:]


def _even_proj(x, g, w, rope_tabs, *, tm, n_heads, e_a, w_b, half, stack=None):
    t, d = x.shape
    n = n_heads * e_a
    c, s1, s2 = rope_tabs
    row = lambda width: pl.BlockSpec((tm, width), lambda i: (i, 0))
    tab_tiles = c.shape[0] // tm
    n_vt = n_heads * (e_a + VT_PAD)
    tab = pl.BlockSpec((tm, e_a), lambda i: (i % tab_tiles, 0))
    tok = pl.BlockSpec((tm * n_heads, e_a), lambda i: (i, 0))
    kern = functools.partial(_even_proj_kernel, tm=tm, n_heads=n_heads, e_a=e_a, w_b=w_b, half=half,
                             qscale=(e_a // 2) ** -0.5 * LOG2E, stacked=stack is not None)
    if stack is None:
        extra, extra_specs, aliases = [], [], {}
        kv_shape = jax.ShapeDtypeStruct((t * n_heads, e_a), F32)
    else:
        k_all, v_all, layer = stack
        extra, extra_specs = [k_all, v_all], [pl.BlockSpec(memory_space=pl.ANY)] * 2
        aliases = {6: 1, 7: 2}
        tok = pl.BlockSpec((1, tm * n_heads, e_a), lambda i: (layer, i, 0))
        kv_shape = jax.ShapeDtypeStruct(k_all.shape, F32)
    return pl.pallas_call(
        kern,
        grid=(t // tm,),
        in_specs=[row(d), _const_spec((1, d)), _const_spec(w.shape), tab, tab, tab] + extra_specs,
        out_specs=[row(n), tok, tok, row(n), pl.BlockSpec((n_vt, tm), lambda i: (0, i)),
                   row(w_b), row(w_b)],
        input_output_aliases=aliases,
        out_shape=[jax.ShapeDtypeStruct((t, n), BF16), kv_shape, kv_shape, jax.ShapeDtypeStruct((t, n), BF16),
                   jax.ShapeDtypeStruct((n_vt, t), BF16), jax.ShapeDtypeStruct((t, w_b), F32),
                   jax.ShapeDtypeStruct((t, w_b), F32)],
        compiler_params=_cparams("arbitrary"),
        name="even_proj",
    )(x, g, w, c, s1, s2, *extra)


def _rope_tables(pos, e_a, rot, dtype=F32):
    half = rot // 2
    dh = e_a // 2
    inv = ROPE_THETA ** (-jnp.arange(half, dtype=F32) * 2.0 / rot)
    ang = pos.astype(F32)[:, None] * inv[None, :]
    cos, sin = jnp.cos(ang).astype(dtype), jnp.sin(ang).astype(dtype)
    t = pos.shape[0]
    ones = jnp.ones((t, dh - rot), dtype)
    zeros_h = jnp.zeros((t, half), dtype)
    zeros_r = jnp.zeros((t, dh - rot), dtype)
    c = jnp.concatenate([cos, cos, ones], axis=1)
    s1 = jnp.concatenate([-sin, zeros_h, zeros_r], axis=1)
    s2 = jnp.concatenate([zeros_h, sin, zeros_r], axis=1)
    rep = lambda a: jnp.concatenate([a, a], axis=1)
    return rep(c), rep(s1), rep(s2)


def _lambda(lp_ref, lam_init):
    lp = lp_ref[...]
    a = jnp.sum(lp[0:1] * lp[1:2], axis=-1, keepdims=True)
    b = jnp.sum(lp[2:3] * lp[3:4], axis=-1, keepdims=True)
    return jnp.exp(a) - jnp.exp(b) + lam_init


def _attn_prompt_kernel(q_ref, k_ref, vt_ref, lp_ref, swc_ref, o_ref, m_ref, acc_ref,
                        *, blk, n_sub, nh, dh, lam_init):
    i = pl.program_id(2)
    e_a = 2 * dh
    vr = vt_ref.shape[0] // nh
    sub = blk // n_sub
    feat = lax.broadcasted_iota(jnp.int32, (e_a, blk), 0)
    qs = []
    for hh in range(nh):
        qt = q_ref[:, hh * e_a:(hh + 1) * e_a].astype(F32).T
        qs.append(jnp.concatenate([jnp.where(feat < dh, qt, 0.0), jnp.where(feat >= dh, qt, 0.0)],
                                  axis=1).astype(BF16))

    m_ref[...] = jnp.full(m_ref.shape, -1e30, F32)
    acc_ref[...] = jnp.zeros(acc_ref.shape, F32)

    def step(off, diag):
        def scores(hh, t):
            s = _dot(k_ref[pl.ds(off + t * sub, sub), hh * e_a:(hh + 1) * e_a], qs[hh])
            if diag:
                key = lax.broadcasted_iota(jnp.int32, s.shape, 0) + t * sub
                qry = lax.broadcasted_iota(jnp.int32, s.shape, 1)
                qry = jnp.where(qry >= blk, qry - blk, qry)
                s = jnp.where(key <= qry, s, jnp.finfo(F32).min)
            return s

        ss = [[scores(hh, t) for t in range(n_sub)] for hh in range(nh)]
        for hh in range(nh):
            for t in range(n_sub):
                m = m_ref[hh]
                m_new = jnp.maximum(m, jnp.max(ss[hh][t], axis=0, keepdims=True))
                p = jnp.exp2(ss[hh][t] - m_new)
                acc_ref[hh] = jnp.exp2(m - m_new) * acc_ref[hh] + _dot(
                    vt_ref[hh * vr:(hh + 1) * vr, pl.ds(off + t * sub, sub)], p.astype(BF16))
                m_ref[hh] = m_new

    def body(j, carry):
        step(pl.multiple_of(j * blk, blk), False)
        return carry

    lax.fori_loop(0, i, body, 0)
    step(pl.multiple_of(i * blk, blk), True)
    lam = _lambda(lp_ref, lam_init)
    for hh in range(nh):
        acc = acc_ref[hh]
        o = acc[:e_a] / acc[e_a:e_a + 1]
        att = o[:, :blk] - lam * o[:, blk:]
        y = att * lax.rsqrt(jnp.mean(att * att, axis=0, keepdims=True) + SUBLN_EPS) * swc_ref[...]
        o_ref[:, hh * e_a:(hh + 1) * e_a] = (y * (1.0 - lam_init)).T.astype(o_ref.dtype)


def _attn_prompt(q, kb, vt, lam_p, subln_w, *, batch, seq, n_heads, e_a, blk, lam_init):
    t = q.shape[0]
    nq = seq // blk
    swc = subln_w.reshape(e_a, 1)
    n_sub = ATTN_SUB if blk % (ATTN_SUB * LANES) == 0 else 1
    nh = ATTN_HEADS if n_heads % ATTN_HEADS == 0 else 1
    kern = functools.partial(_attn_prompt_kernel, blk=blk, n_sub=n_sub, nh=nh, dh=e_a // 2, lam_init=lam_init)
    return pl.pallas_call(
        kern,
        grid=(batch, n_heads // nh, nq),
        in_specs=[pl.BlockSpec((blk, nh * e_a), lambda b, h, i: (b * nq + i, h)),
                  pl.BlockSpec((seq, nh * e_a), lambda b, h, i: (b, h)),
                  pl.BlockSpec((nh * (e_a + VT_PAD), seq), lambda b, h, i: (h, b)),
                  pl.BlockSpec(lam_p.shape, lambda b, h, i: (0, 0)),
                  pl.BlockSpec(swc.shape, lambda b, h, i: (0, 0))],
        out_specs=pl.BlockSpec((blk, nh * e_a), lambda b, h, i: (b * nq + i, h)),
        out_shape=jax.ShapeDtypeStruct((t, n_heads * e_a), BF16),
        scratch_shapes=[pltpu.VMEM((nh, 1, 2 * blk), F32), pltpu.VMEM((nh, e_a + VT_PAD, 2 * blk), F32)],
        compiler_params=_cparams("arbitrary", "arbitrary", "arbitrary"),
        name="attn_prompt",
    )(q, kb, vt, lam_p, swc)


def _attn_sample_kernel(pt_ref, q_ref, kn_ref, vn_ref, lp_ref, sw_ref, *rest,
                        n_pages, n_heads, e_a, lam_init):
    del pt_ref
    k_refs = rest[:n_pages]
    v_refs = rest[n_pages:2 * n_pages]
    o_ref = rest[2 * n_pages]
    dh = e_a // 2
    nm = max(2 * n_heads, BF16_ROWS)
    rows_pp = k_refs[0].shape[1]
    rowi = lax.broadcasted_iota(jnp.int32, (nm, e_a), 0)
    lane = lax.broadcasted_iota(jnp.int32, (nm, e_a), 1)

    def per_head_rows(ref):
        t = ref[0]
        out = jnp.zeros((nm, e_a), F32)
        for h in range(n_heads):
            out = jnp.where((rowi >> 1) == h, jnp.broadcast_to(t[h:h + 1].astype(F32), (nm, e_a)), out)
        return out

    qm_f = jnp.where((lane // dh) == (rowi & 1), per_head_rows(q_ref), 0.0)
    qm = qm_f.astype(BF16)
    colh = lax.broadcasted_iota(jnp.int32, (nm, rows_pp), 1) % n_heads
    valid = colh == (lax.broadcasted_iota(jnp.int32, (nm, rows_pp), 0) >> 1)
    ss = [jnp.where(valid, _dot_nt(qm, k_refs[p][0].astype(BF16)), jnp.finfo(F32).min)
          for p in range(n_pages)]
    kn = per_head_rows(kn_ref).astype(BF16).astype(F32)
    vn = per_head_rows(vn_ref).astype(BF16).astype(F32)
    s_new = jnp.sum(qm.astype(F32) * kn, axis=-1, keepdims=True)
    m = s_new
    for s in ss:
        m = jnp.maximum(m, jnp.max(s, axis=-1, keepdims=True))
    p_new = jnp.exp2(s_new - m)
    l = p_new
    acc = p_new.astype(BF16).astype(F32) * vn
    for p in range(n_pages):
        pr = jnp.exp2(ss[p] - m)
        l = l + jnp.sum(pr, axis=-1, keepdims=True)
        acc = acc + _dot(pr.astype(BF16), v_refs[p][0].astype(BF16))
    o = acc / l
    lam = _lambda(lp_ref, lam_init)
    sw = sw_ref[...]
    for h in range(n_heads):
        att = o[2 * h:2 * h + 1] - lam * o[2 * h + 1:2 * h + 2]
        o_ref[0, h:h + 1, :] = (_rms(att, sw, SUBLN_EPS) * (1.0 - lam_init)).astype(o_ref.dtype)


def _attn_sample(q, k_new, v_new, cache_k, cache_v, layer, page_table, lam_p, subln_w,
                 *, n_heads, e_a, lam_init):
    b, n = q.shape
    n_pages = page_table.shape[1]
    n_phys, page = cache_k.shape[1], cache_k.shape[2]
    ck = cache_k.reshape(cache_k.shape[0] * n_phys, page * n_heads, e_a)
    cv = cache_v.reshape(cache_v.shape[0] * n_phys, page * n_heads, e_a)
    base = layer * n_phys
    tok = pl.BlockSpec((1, n_heads, e_a), lambda i, pt: (i, 0, 0))
    page_specs = [pl.BlockSpec((1, page * n_heads, e_a), functools.partial(
        lambda i, pt, p: (base + pt[i, p], 0, 0), p=p)) for p in range(n_pages)]
    kern = functools.partial(_attn_sample_kernel, n_pages=n_pages, n_heads=n_heads, e_a=e_a,
                             lam_init=lam_init)
    grid_spec = pltpu.PrefetchScalarGridSpec(
        num_scalar_prefetch=1,
        grid=(b,),
        in_specs=[tok, tok, tok,
                  pl.BlockSpec(lam_p.shape, lambda i, pt: (0, 0)),
                  pl.BlockSpec(subln_w.shape, lambda i, pt: (0, 0))] + page_specs + page_specs,
        out_specs=tok,
    )
    tok3 = lambda a: a.reshape(b, n_heads, e_a)
    out = pl.pallas_call(
        kern,
        grid_spec=grid_spec,
        out_shape=jax.ShapeDtypeStruct((b, n_heads, e_a), F32),
        compiler_params=_cparams("arbitrary"),
        name="attn_sample",
    )(page_table, tok3(q).astype(F32), tok3(k_new), tok3(v_new), lam_p, subln_w,
      *([ck] * n_pages), *([cv] * n_pages))
    return out.reshape(b, n)


def _seq_hist(buf_ref, cur, tiles_per_seq, tm):
    first = (pl.program_id(0) % tiles_per_seq) == 0

    @pl.when(first)
    def _():
        buf_ref[0:HIST_ROWS, :] = jnp.zeros((HIST_ROWS, buf_ref.shape[1]), buf_ref.dtype)

    @pl.when(jnp.logical_not(first))
    def _():
        buf_ref[0:HIST_ROWS, :] = buf_ref[tm:tm + HIST_ROWS, :]

    buf_ref[HIST_ROWS:HIST_ROWS + tm, :] = cur


def _even_tail_seq(att_ref, bg_ref, cx_ref, cw_ref, w_ref, buf_ref, *, tm, tiles_per_seq):
    n_att = att_ref.shape[1]
    cx = cx_ref[...]
    _seq_hist(buf_ref, cx, tiles_per_seq, tm)
    cw = cw_ref[...]
    width = cw.shape[0]
    conv = cw[width - 1:width] * cx
    for j in range(width - 1):
        conv = conv + cw[j:j + 1] * buf_ref[pl.ds(HIST_ROWS - (width - 1) + j, tm), :]
    yb = (bg_ref[...] * conv).astype(BF16)
    return _dot(att_ref[...].astype(BF16), w_ref[:n_att, :]) + _dot(yb, w_ref[n_att:, :])


def _even_out_step_kernel(att_ref, bg_ref, cx_ref, x_ref, cw_ref, w_ref, *rest, n_att):
    hist_refs, o_ref = rest[:-1], rest[-1]
    cw = cw_ref[...]
    width = cw.shape[0]
    conv = cw[width - 1:width] * cx_ref[...]
    for j in range(width - 1):
        conv = conv + cw[j:j + 1] * hist_refs[j][...]
    yb = (bg_ref[...] * conv).astype(BF16)
    o_ref[...] = (x_ref[...] + _dot(att_ref[...].astype(BF16), w_ref[:n_att, :])
                  + _dot(yb, w_ref[n_att:, :]))


def _even_out_step(att, bg, cx, x, conv_w, w_out, hist, *, tm):
    t, d = x.shape
    n_att, w_b = att.shape[1], bg.shape[1]
    row = lambda width: pl.BlockSpec((tm, width), lambda i: (i, 0))
    return pl.pallas_call(
        functools.partial(_even_out_step_kernel, n_att=n_att),
        grid=(t // tm,),
        in_specs=[row(n_att), row(w_b), row(w_b), row(d), _const_spec(conv_w.shape), _const_spec(w_out.shape)]
                 + [row(w_b)] * len(hist),
        out_specs=row(d),
        out_shape=jax.ShapeDtypeStruct((t, d), F32),
        compiler_params=_cparams("arbitrary"),
        name="even_out",
    )(att, bg, cx, x, conv_w, w_out, *hist)


def _gdn_gates(xn, wz_ref, wba_ref, al_ref, dtb_ref, z_ref, bga_ref, *, n_heads):
    z_ref[...] = _dot(xn, wz_ref[...]).astype(z_ref.dtype)
    ba = _dot(xn, wba_ref[...])
    beta = _sigmoid(ba)
    g = -jnp.exp(al_ref[...]) * _softplus(ba + dtb_ref[...])
    lane = lax.broadcasted_iota(jnp.int32, ba.shape, 1)
    bga_ref[...] = jnp.where(lane < n_heads, beta, g)


def _gdn_qkv_chunk(conv, c0, q_ref, k_ref, v_ref, *, n_heads, dk):
    kdim = n_heads * dk
    act = _silu(conv)
    for i in range(conv.shape[1] // dk):
        t = act[:, i * dk:(i + 1) * dk]
        ch = c0 + i * dk
        if ch < 2 * kdim:
            t = t * lax.rsqrt(jnp.sum(t * t, axis=-1, keepdims=True) + EPS)
        if ch < kdim:
            q_ref[:, ch:ch + dk] = t * (dk ** -0.5)
        elif ch < 2 * kdim:
            k_ref[:, ch - kdim:ch - kdim + dk] = t
        else:
            v_ref[:, ch - 2 * kdim:ch - 2 * kdim + dk] = t


def _gdn_proj_seq_kernel(x_ref, g_ref, wqkv_ref, wz_ref, wba_ref, cw_ref, al_ref, dtb_ref,
                         q_ref, k_ref, v_ref, z_ref, bga_ref, tail_ref, buf_ref,
                         *, tm, tiles_per_seq, n_heads, dk, cwid):
    xn = _rms(x_ref[...], g_ref[...], EPS).astype(BF16)
    first = (pl.program_id(0) % tiles_per_seq) == 0

    @pl.when(first)
    def _():
        buf_ref[0:HIST_ROWS, :] = jnp.zeros((HIST_ROWS, buf_ref.shape[1]), buf_ref.dtype)

    @pl.when(jnp.logical_not(first))
    def _():
        buf_ref[0:HIST_ROWS, :] = buf_ref[tm:tm + HIST_ROWS, :]

    cw = cw_ref[...]
    width = cw.shape[0]
    n_chunks = wqkv_ref.shape[1] // cwid
    mm = lambda c: _dot(xn, wqkv_ref[:, c * cwid:(c + 1) * cwid])
    h_next = mm(0)
    for c in range(n_chunks):
        cs = slice(c * cwid, (c + 1) * cwid)
        hm = h_next
        if c + 1 < n_chunks:
            h_next = mm(c + 1)
        buf_ref[HIST_ROWS:HIST_ROWS + tm, cs] = hm
        conv = cw[width - 1:width, cs] * hm
        for j in range(width - 1):
            conv = conv + cw[j:j + 1, cs] * buf_ref[pl.ds(HIST_ROWS - (width - 1) + j, tm), cs]
        _gdn_qkv_chunk(conv, c * cwid, q_ref, k_ref, v_ref, n_heads=n_heads, dk=dk)
    tail_ref[0] = buf_ref[tm:tm + HIST_ROWS, :]
    _gdn_gates(xn, wz_ref, wba_ref, al_ref, dtb_ref, z_ref, bga_ref, n_heads=n_heads)


def _gdn_proj_step_kernel(x_ref, g_ref, wqkv_ref, wz_ref, wba_ref, cw_ref, al_ref, dtb_ref, *rest,
                          n_heads, dk, n_hist):
    hist_refs = rest[:n_hist]
    q_ref, k_ref, v_ref, z_ref, bga_ref, raw_ref = rest[n_hist:]
    xn = _rms(x_ref[...], g_ref[...], EPS).astype(BF16)
    hm = _dot(xn, wqkv_ref[...])
    raw_ref[...] = hm
    cw = cw_ref[...]
    width = cw.shape[0]
    conv = cw[width - 1:width] * hm
    for j in range(width - 1):
        conv = conv + cw[j:j + 1] * hist_refs[j][...]
    _gdn_qkv_chunk(conv, 0, q_ref, k_ref, v_ref, n_heads=n_heads, dk=dk)
    _gdn_gates(xn, wz_ref, wba_ref, al_ref, dtb_ref, z_ref, bga_ref, n_heads=n_heads)


def _gdn_proj(x, g, wqkv, wz, wba, conv_w, al_pad, dtb_pad, *, tm, n_heads, dk, seq=None, hist=None):
    t, d = x.shape
    qkv_c = wqkv.shape[1]
    vdim = wz.shape[1]
    row = lambda width: pl.BlockSpec((tm, width), lambda i: (i, 0))
    common = [row(d), _const_spec((1, d)), _const_spec(wqkv.shape), _const_spec(wz.shape),
              _const_spec(wba.shape), _const_spec(conv_w.shape), _const_spec(al_pad.shape),
              _const_spec(dtb_pad.shape)]
    outs = [row(vdim), row(vdim), row(vdim), row(vdim), row(LANES)]
    out_shapes = ([jax.ShapeDtypeStruct((t, vdim), F32)] * 3 + [jax.ShapeDtypeStruct((t, vdim), BF16)]
                  + [jax.ShapeDtypeStruct((t, LANES), F32)])
    if hist is None:
        tps = seq // tm
        kern = functools.partial(_gdn_proj_seq_kernel, tm=tm, tiles_per_seq=tps, n_heads=n_heads, dk=dk,
                                 cwid=2 * dk)
        extra, extra_specs = [], []
        outs.append(pl.BlockSpec((1, HIST_ROWS, qkv_c), lambda i: (i // tps, 0, 0)))
        out_shapes.append(jax.ShapeDtypeStruct((t // seq, HIST_ROWS, qkv_c), F32))
        scratch = [pltpu.VMEM((tm + HIST_ROWS, qkv_c), F32)]
    else:
        hist_arr, n_hist = hist
        kern = functools.partial(_gdn_proj_step_kernel, n_heads=n_heads, dk=dk, n_hist=n_hist)
        extra = [hist_arr] * n_hist
        extra_specs = [pl.BlockSpec((tm, qkv_c), functools.partial(lambda i, j: (i, j), j=j))
                       for j in range(n_hist)]
        outs.append(row(qkv_c))
        out_shapes.append(jax.ShapeDtypeStruct((t, qkv_c), F32))
        scratch = []
    return pl.pallas_call(
        kern,
        grid=(t // tm,),
        in_specs=common + extra_specs,
        out_specs=outs,
        out_shape=out_shapes,
        scratch_shapes=scratch,
        compiler_params=_cparams("arbitrary"),
        name="gdn_proj",
    )(x, g, wqkv, wz, wba, conv_w, al_pad, dtb_pad, *extra)


def _pair_blockdiag(a, b):
    z = jnp.zeros_like(a)
    return jnp.concatenate([jnp.concatenate([a, z], axis=1), jnp.concatenate([z, b], axis=1)], axis=0)


def _gdn_prep_kernel(q_ref, k_ref, v_ref, bga_ref, u_ref, w_ref, qe_ref, ke_ref, qk_ref, gc_ref,
                     *, n_heads, dk, chunk, n_sub):
    n_pairs = n_heads // 2
    rows_blk = chunk * n_sub
    bga = bga_ref[...]
    r = lax.broadcasted_iota(jnp.int32, (rows_blk, rows_blk), 0)
    c = lax.broadcasted_iota(jnp.int32, (rows_blk, rows_blk), 1)
    sh = chunk.bit_length() - 1
    tril = jnp.where(jnp.logical_and(r >= c, (r >> sh) == (c >> sh)), 1.0, 0.0).astype(BF16)
    b3 = _split3(bga)
    gc = _dot(tril, b3[0]) + (_dot(tril, b3[1]) + _dot(tril, b3[2]))
    gc_ref[...] = gc
    lane = lax.broadcasted_iota(jnp.int32, (rows_blk, LANES), 1)
    is_g = jnp.logical_and(lane >= n_heads, lane < 2 * n_heads)
    odd = ((lane - n_heads) & 1) == 1
    gce = jnp.where(jnp.logical_and(is_g, jnp.logical_not(odd)), gc, 0.0)
    gco = jnp.where(jnp.logical_and(is_g, odd), gc, 0.0)
    ar = lax.broadcasted_iota(jnp.int32, (BF16_ROWS, LANES), 0)
    al = lax.broadcasted_iota(jnp.int32, (BF16_ROWS, LANES), 1)
    asel = jnp.where(jnp.logical_and(jnp.logical_and(al >= n_heads, ar < n_pairs),
                                     ((al - n_heads) >> 1) == ar), 1.0, 0.0).astype(BF16)
    prow = lax.broadcasted_iota(jnp.int32, (chunk, 2 * chunk), 0)
    plane = lax.broadcasted_iota(jnp.int32, (chunk, 2 * chunk), 1)
    pcol = plane & (chunk - 1)
    lo_half = plane < chunk
    incl = prow >= pcol
    strict = prow > pcol

    def bd(m):
        return jnp.concatenate([jnp.where(lo_half, m, 0.0), jnp.where(lo_half, 0.0, m)], axis=0).astype(BF16)

    items = [(ci, p) for ci in range(n_sub) for p in range(n_pairs)]
    lmat, tinv, vbn, kbe = {}, {}, {}, {}
    for ci, p in items:
        rs = slice(ci * chunk, (ci + 1) * chunk)
        cs = slice(2 * p * dk, 2 * (p + 1) * dk)
        g3 = _split3(jnp.concatenate([gce[rs], gco[rs]], axis=0))
        grow = _dot_nt(asel, g3[0]) + (_dot_nt(asel, g3[1]) + _dot_nt(asel, g3[2]))
        k2, q2, v2 = k_ref[rs, cs], q_ref[rs, cs], v_ref[rs, cs]
        ha, hb = 2 * p, 2 * p + 1

        def nat(a, b, rows=chunk):
            return jnp.concatenate([jnp.broadcast_to(a, (rows, dk)), jnp.broadcast_to(b, (rows, dk))], axis=1)

        gca, gcb = gc[rs, n_heads + ha:n_heads + ha + 1], gc[rs, n_heads + hb:n_heads + hb + 1]
        bnat = nat(bga[rs, ha:ha + 1], bga[rs, hb:hb + 1])
        gnat = nat(gca, gcb)
        glast = nat(gca[chunk - 1:chunk], gcb[chunk - 1:chunk], rows=1)
        egn = jnp.exp(gnat)
        kb2 = k2 * bnat
        k2b = k2.astype(BF16)
        rhs_t = _pair_blockdiag(k2b[:, :dk], k2b[:, dk:])
        decay = jnp.where(incl, jnp.exp(jnp.where(incl, jnp.where(lo_half, gca, gcb) - grow[p:p + 1], 0.0)), 0.0)
        lm = jnp.where(strict, _dot_nt(kb2.astype(BF16), rhs_t) * decay, 0.0)
        qk_ref[rs, 2 * p * chunk:2 * (p + 1) * chunk] = jnp.where(
            incl, _dot_nt(q2.astype(BF16), rhs_t) * decay, 0.0).astype(BF16)
        qe_ref[rs, cs] = (q2 * egn).astype(BF16)
        ke_ref[rs, cs] = (k2 * jnp.exp(glast - gnat)).astype(BF16)
        lmat[ci, p] = lm
        tinv[ci, p] = jnp.where(prow == pcol, 1.0, 0.0) - jnp.where((prow >> 1) == (pcol >> 1), lm, 0.0)
        vbn[ci, p] = (v2 * bnat).astype(BF16)
        kbe[ci, p] = (kb2 * egn).astype(BF16)
    s = 1
    while 2 * (1 << s) <= chunk:
        joins = jnp.logical_and((prow >> (s + 1)) == (pcol >> (s + 1)), (prow >> s) != (pcol >> s))
        x = {it: _dot(tinv[it].astype(BF16), bd(jnp.where(joins, lmat[it], 0.0))) for it in items}
        tinv = {it: tinv[it] - _dot(x[it].astype(BF16), bd(tinv[it])) for it in items}
        s += 1
    for ci, p in items:
        rs = slice(ci * chunk, (ci + 1) * chunk)
        cs = slice(2 * p * dk, 2 * (p + 1) * dk)
        tb = tinv[ci, p].astype(BF16)
        u_ref[rs, cs] = _dot(tb, _pair_blockdiag(vbn[ci, p][:, :dk], vbn[ci, p][:, dk:]))
        w_ref[rs, cs] = _dot(tb, _pair_blockdiag(kbe[ci, p][:, :dk], kbe[ci, p][:, dk:])).astype(BF16)


def _gdn_prep(q, k, v, bga, *, n_heads, dk, n_sub):
    t, vdim = q.shape
    chunk = GDN_CHUNK
    assert 2 * chunk == LANES and n_heads % 2 == 0 and 2 * n_heads <= LANES
    rows = chunk * n_sub
    qk_w = n_heads * chunk
    row = lambda width: pl.BlockSpec((rows, width), lambda i: (i, 0))
    kern = functools.partial(_gdn_prep_kernel, n_heads=n_heads, dk=dk, chunk=chunk, n_sub=n_sub)
    return pl.pallas_call(
        kern,
        grid=(t // rows,),
        in_specs=[row(vdim), row(vdim), row(vdim), row(LANES)],
        out_specs=[row(vdim), row(vdim), row(vdim), row(vdim), row(qk_w), row(LANES)],
        out_shape=[jax.ShapeDtypeStruct((t, vdim), F32)] + [jax.ShapeDtypeStruct((t, vdim), BF16)] * 3
                  + [jax.ShapeDtypeStruct((t, qk_w), BF16), jax.ShapeDtypeStruct((t, LANES), F32)],
        compiler_params=_cparams("arbitrary"),
        name="gdn_prep",
    )(q, k, v, bga)


def _gdn_scan_kernel(u_ref, w_ref, qe_ref, ke_ref, qk_ref, gc_ref, o_ref, sout_ref, s_ref,
                     *, n_heads, dk, chunk, n_sub, nbat):
    j = pl.program_id(1)

    @pl.when(j == 0)
    def _():
        s_ref[...] = jnp.zeros(s_ref.shape, s_ref.dtype)

    sls = [slice(h * dk, (h + 1) * dk) for h in range(n_heads)]
    items = [(bi, h) for bi in range(nbat) for h in range(n_heads)]

    def body(c, carry):
        r0 = pl.multiple_of(c * chunk, chunk)
        rows = pl.ds(r0, chunk)
        st = {it: s_ref[it[0], it[1]] for it in items}
        sb = {it: st[it].astype(BF16) for it in items}
        ws = {(bi, h): _dot(jnp.concatenate([w_ref[bi, rows, sls[h]], qe_ref[bi, rows, sls[h]]], axis=0),
                            sb[bi, h]) for bi, h in items}
        vb = {(bi, h): (u_ref[bi, rows, sls[h]] - ws[bi, h][:chunk]).astype(BF16) for bi, h in items}
        for bi in range(nbat):
            for p in range(n_heads // 2):
                ha, hb = 2 * p, 2 * p + 1
                intra = _dot(qk_ref[bi, rows, 2 * p * chunk:2 * (p + 1) * chunk],
                             _pair_blockdiag(vb[bi, ha], vb[bi, hb]))
                o_ref[bi, rows, 2 * p * dk:2 * (p + 1) * dk] = (
                    jnp.concatenate([ws[bi, ha][chunk:], ws[bi, hb][chunk:]], axis=1) + intra)
        for bi in range(nbat):
            eg = jnp.exp(gc_ref[bi, pl.ds(r0 + chunk - 1, 1), :])
            for h in range(n_heads):
                s_ref[bi, h] = (st[bi, h] * eg[:, n_heads + h:n_heads + h + 1]
                                + _dot_tn(ke_ref[bi, rows, sls[h]], vb[bi, h]))
        return carry

    lax.fori_loop(0, n_sub, body, 0)

    @pl.when(j == pl.num_programs(1) - 1)
    def _():
        sout_ref[...] = s_ref[...]


def _gdn_scan(u, w, qe, ke, qk, gc, *, batch, seq, n_heads, dk, rows, nbat):
    t, vdim = u.shape
    chunk = GDN_CHUNK
    seq3 = lambda a: a.reshape(batch, seq, a.shape[1])
    row = lambda width: pl.BlockSpec((nbat, rows, width), lambda b, j: (b, j, 0))
    kern = functools.partial(_gdn_scan_kernel, n_heads=n_heads, dk=dk, chunk=chunk, n_sub=rows // chunk,
                             nbat=nbat)
    o, s_new = pl.pallas_call(
        kern,
        grid=(batch // nbat, seq // rows),
        in_specs=[row(vdim)] * 4 + [row(qk.shape[1]), row(LANES)],
        out_specs=[row(vdim), pl.BlockSpec((nbat, n_heads, dk, dk), lambda b, j: (b, 0, 0, 0))],
        out_shape=[jax.ShapeDtypeStruct((batch, seq, vdim), F32),
                   jax.ShapeDtypeStruct((batch, n_heads, dk, dk), F32)],
        scratch_shapes=[pltpu.VMEM((nbat, n_heads, dk, dk), F32)],
        compiler_params=_cparams("arbitrary", "arbitrary"),
        name="gdn_scan",
    )(seq3(u), seq3(w), seq3(qe), seq3(ke), seq3(qk), seq3(gc))
    return o.reshape(t, vdim), s_new


def _gdn_step_kernel(q_ref, k_ref, v_ref, bga_ref, s_ref, prev_ref, o_ref, sout_ref, *, n_heads, dk, nb):
    del prev_ref
    s_ref, sout_ref = s_ref.at[0], sout_ref.at[0]
    pad = jnp.zeros((dk - n_heads, dk), F32)

    def body(b, carry):
        qt = jnp.concatenate([q_ref[b], pad], axis=0).T
        kt = jnp.concatenate([k_ref[b], pad], axis=0).T
        vb = v_ref[b]
        bg = bga_ref[pl.ds(b, 1), :]
        eg = jnp.exp(bg)
        rows = []
        for h in range(n_heads):
            st = s_ref[b, h] * eg[:, n_heads + h:n_heads + h + 1]
            kc = kt[:, h:h + 1]
            ks = jnp.sum(kc * st, axis=0, keepdims=True)
            delta = (vb[h:h + 1, :] - ks) * bg[:, h:h + 1]
            st = st + kc * delta
            sout_ref[b, h] = st
            rows.append(jnp.sum(qt[:, h:h + 1] * st, axis=0, keepdims=True))
        o_ref[b] = jnp.concatenate(rows, axis=0)
        return carry

    for b in range(nb):
        body(b, 0)


def _gdn_step(q, k, v, bga, states, new_states, layer, *, n_heads, dk, nb):
    b = q.shape[0]
    q3, k3, v3 = (a.reshape(b, n_heads, dk) for a in (q, k, v))
    vec = pl.BlockSpec((nb, n_heads, dk), lambda i: (i, 0, 0))
    st = pl.BlockSpec((1, nb, n_heads, dk, dk), lambda i: (layer, i, 0, 0, 0))
    kern = functools.partial(_gdn_step_kernel, n_heads=n_heads, dk=dk, nb=nb)
    o, s_new = pl.pallas_call(
        kern,
        grid=(b // nb,),
        in_specs=[vec, vec, vec, pl.BlockSpec((nb, LANES), lambda i: (i, 0)),
                  st, pl.BlockSpec(memory_space=pl.ANY)],
        out_specs=[vec, st],
        input_output_aliases={5: 1},
        out_shape=[jax.ShapeDtypeStruct((b, n_heads, dk), F32), jax.ShapeDtypeStruct(new_states.shape, F32)],
        compiler_params=_cparams("arbitrary"),
        name="gdn_step",
    )(q3, k3, v3, bga, states, new_states)
    return o.reshape(b, n_heads * dk), s_new


def _gdn_tail(o_ref, z_ref, nw_ref, w_ref, *, n_heads, dk):
    nw = nw_ref[...]
    parts = []
    for h in range(n_heads):
        sl = slice(h * dk, (h + 1) * dk)
        parts.append((_rms(o_ref[:, sl], nw, EPS) * _silu(z_ref[:, sl].astype(F32))).astype(BF16))
    return _dot(jnp.concatenate(parts, axis=1), w_ref[...])


def _gdn_out_kernel(o_ref, z_ref, x_ref, nw_ref, w_ref, y_ref, *, n_heads, dk):
    y_ref[...] = x_ref[...] + _gdn_tail(o_ref, z_ref, nw_ref, w_ref, n_heads=n_heads, dk=dk)


def _gdn_out(o, z, x, norm_w, w_out, *, tm, n_heads, dk):
    t, d = x.shape
    vdim = o.shape[1]
    row = lambda width: pl.BlockSpec((tm, width), lambda i: (i, 0))
    kern = functools.partial(_gdn_out_kernel, n_heads=n_heads, dk=dk)
    return pl.pallas_call(
        kern,
        grid=(t // tm,),
        in_specs=[row(vdim), row(vdim), row(d), _const_spec(norm_w.shape), _const_spec(w_out.shape)],
        out_specs=row(d),
        out_shape=jax.ShapeDtypeStruct((t, d), F32),
        compiler_params=_cparams("arbitrary"),
        name="gdn_out",
    )(o, z, x, norm_w, w_out)


def _ffn_tail(x, conv, up, cb_ref, wd_ref, gf_ref, o_ref, final_norm):
    act = (_silu(conv + cb_ref[...]) * up).astype(BF16)
    y = x + _dot(act, wd_ref[0])
    if final_norm:
        y = _rms(y, gf_ref[...], EPS)
    o_ref[...] = y


def _ffn_seq_kernel(x_ref, g_ref, wg_ref, wu_ref, cw_ref, cb_ref, wd_ref, gf_ref, *rest,
                    tm, tiles_per_seq, final_norm, cwid, mix):
    x = x_ref[...]
    if mix[0] == "gdn":
        o_in, z_in, nw_ref, wo_ref, o_ref, tail_ref, buf_ref, act_ref = rest
        x = x + _gdn_tail(o_in, z_in, nw_ref, wo_ref, n_heads=mix[1], dk=mix[2])
    else:
        att_in, bg_in, cx_in, scw_ref, wo_ref, o_ref, tail_ref, buf_ref, act_ref, cbuf_ref = rest
        x = x + _even_tail_seq(att_in, bg_in, cx_in, scw_ref, wo_ref, cbuf_ref, tm=tm,
                               tiles_per_seq=tiles_per_seq)
    xn = _rms(x, g_ref[...], EPS).astype(BF16)
    first = (pl.program_id(0) % tiles_per_seq) == 0

    @pl.when(first)
    def _():
        buf_ref[0:HIST_ROWS, :] = jnp.zeros((HIST_ROWS, buf_ref.shape[1]), buf_ref.dtype)

    @pl.when(jnp.logical_not(first))
    def _():
        buf_ref[0:HIST_ROWS, :] = buf_ref[tm:tm + HIST_ROWS, :]

    cw = cw_ref[...]
    cb = cb_ref[...]
    width = cw.shape[0]
    n_chunks = wg_ref.shape[2] // cwid

    def mm(c):
        cs = slice(c * cwid, (c + 1) * cwid)
        return _dot(xn, wg_ref[0, :, cs]), _dot(xn, wu_ref[0, :, cs])

    nxt = mm(0)
    for c in range(n_chunks):
        cs = slice(c * cwid, (c + 1) * cwid)
        gate, up = nxt
        if c + 1 < n_chunks:
            nxt = mm(c + 1)
        buf_ref[HIST_ROWS:HIST_ROWS + tm, cs] = gate
        conv = cw[width - 1:width, cs] * gate
        for j in range(width - 1):
            conv = conv + cw[j:j + 1, cs] * buf_ref[pl.ds(HIST_ROWS - (width - 1) + j, tm), cs]
        act_ref[:, cs] = (_silu(conv + cb[:, cs]) * up).astype(BF16)
    tail_ref[0] = buf_ref[tm:tm + HIST_ROWS, :]
    y = x + _dot(act_ref[...], wd_ref[0])
    if final_norm:
        y = _rms(y, gf_ref[...], EPS)
    o_ref[...] = y


def _ffn_step_kernel(x_ref, g_ref, wg_ref, wu_ref, cw_ref, cb_ref, wd_ref, gf_ref, *rest,
                     n_hist, final_norm):
    hist_refs = rest[:n_hist]
    o_ref, raw_ref = rest[n_hist:]
    x = x_ref[...]
    xn = _rms(x, g_ref[...], EPS).astype(BF16)
    gate = _dot(xn, wg_ref[0])
    raw_ref[...] = gate
    cw = cw_ref[...]
    width = cw.shape[0]
    conv = cw[width - 1:width] * gate
    for j in range(width - 1):
        conv = conv + cw[j:j + 1] * hist_refs[j][...]
    _ffn_tail(x, conv, _dot(xn, wu_ref[0]), cb_ref, wd_ref, gf_ref, o_ref, final_norm)


def _ffn(x, g, w_up_all, layer, conv_w, conv_b, w_down_all, gf, *, tm, final_norm, seq=None, hist=None,
         mix=None):
    t, d = x.shape
    d_ff = conv_w.shape[1]
    row = lambda width: pl.BlockSpec((tm, width), lambda i: (i, 0))
    layer_blk = lambda shape, col: pl.BlockSpec((1,) + shape, lambda i: (layer, 0, col),
                                                pipeline_mode=pl.Buffered(1))
    common = [row(d), _const_spec((1, d)), layer_blk((d, d_ff), 0), layer_blk((d, d_ff), 1),
              _const_spec(conv_w.shape), _const_spec(conv_b.shape), layer_blk((d_ff, d), 0),
              _const_spec((1, d))]
    if hist is None:
        tps = seq // tm
        assert d_ff % (2 * LANES) == 0
        scratch = [pltpu.VMEM((tm + HIST_ROWS, d_ff), F32), pltpu.VMEM((tm, d_ff), BF16)]
        if mix[0] == "gdn":
            _, o_in, z_in, nw, w_out, n_heads, dk = mix
            extra = [o_in, z_in, nw, w_out]
            extra_specs = [row(o_in.shape[1]), row(z_in.shape[1]), _const_spec(nw.shape), _const_spec(w_out.shape)]
            mix_static = ("gdn", n_heads, dk)
        else:
            _, att, bg, cx, scw, w_out = mix
            extra = [att, bg, cx, scw, w_out]
            extra_specs = [row(att.shape[1]), row(bg.shape[1]), row(cx.shape[1]), _const_spec(scw.shape),
                           _const_spec(w_out.shape)]
            scratch.append(pltpu.VMEM((tm + HIST_ROWS, cx.shape[1]), F32))
            mix_static = ("even",)
        kern = functools.partial(_ffn_seq_kernel, tm=tm, tiles_per_seq=tps, final_norm=final_norm,
                                 cwid=2 * LANES, mix=mix_static)
        outs = [row(d), pl.BlockSpec((1, HIST_ROWS, d_ff), lambda i: (i // tps, 0, 0))]
        out_shapes = [jax.ShapeDtypeStruct((t, d), F32),
                      jax.ShapeDtypeStruct((t // seq, HIST_ROWS, d_ff), F32)]
    else:
        assert mix is None
        hist_arr, n_hist = hist
        kern = functools.partial(_ffn_step_kernel, n_hist=n_hist, final_norm=final_norm)
        extra = [hist_arr] * n_hist
        extra_specs = [pl.BlockSpec((tm, d_ff), functools.partial(lambda i, j: (i, j), j=j))
                       for j in range(n_hist)]
        outs = [row(d), row(d_ff)]
        out_shapes = [jax.ShapeDtypeStruct((t, d), F32), jax.ShapeDtypeStruct((t, d_ff), F32)]
        scratch = []
    return pl.pallas_call(
        kern,
        grid=(t // tm,),
        in_specs=common + extra_specs,
        out_specs=outs,
        out_shape=out_shapes,
        scratch_shapes=scratch,
        compiler_params=_cparams("arbitrary"),
        name="conv_ffn",
    )(x, g, w_up_all, w_up_all, conv_w, conv_b, w_down_all, gf, *extra)


def _tile(n, pref):
    t = min(n, pref)
    while n % t:
        t //= 2
    return t


def kernel(x_prompt, x_sample, cache_k, cache_v, state_sconv, state_gdn_conv, state_gdn, state_ffn_conv, page_table, norm_mix, norm_ffn, norm_final, w_in_even, w_out_even, lambda_q1, lambda_k1, lambda_q2, lambda_k2, subln_w, sconv_w, w_in_odd, gdn_conv_w, a_log, dt_bias, gdn_norm_w, w_out_odd, w_up, ffn_conv_w, ffn_conv_b, w_down):
    batch, seq, d = x_prompt.shape
    dec_b = x_sample.shape[0]
    depth = norm_mix.shape[0]
    n_heads_a, e_a = cache_k.shape[3], cache_k.shape[4]
    n_att = n_heads_a * e_a
    rot = (e_a // 2) // 4
    w_b = state_sconv.shape[-1]
    sconv = sconv_w.shape[1]
    page = cache_k.shape[2]
    past_len = page_table.shape[1] * page
    n_heads_c, dk = state_gdn.shape[2], state_gdn.shape[3]
    kdim = n_heads_c * dk
    qkv_c = gdn_conv_w.shape[-1]
    gdn_conv = gdn_conv_w.shape[1]
    d_ff = ffn_conv_w.shape[-1]
    ffn_conv = ffn_conv_w.shape[1]
    t_p = batch * seq

    tm_p = _tile(seq, 512)
    blk = _tile(seq, 512)
    scan_rows = _tile(seq, 256)

    xp = x_prompt.reshape(t_p, d)
    xs = x_sample.reshape(dec_b, d)
    rope_p = _rope_tables(jnp.arange(seq), e_a, rot)
    rope_s = _rope_tables(jnp.full((dec_b,), past_len), e_a, rot)
    gf = norm_final.reshape(1, d)
    w_up_b, w_down_b = w_up.astype(BF16), w_down.astype(BF16)

    ks, vs, scp, scs = [], [], [], []
    gcp, gcs, gsp, fcp, fcs = [], [], [], [], []
    kp = jnp.zeros((w_in_even.shape[0], t_p * n_heads_a, e_a), F32)
    vp = jnp.zeros_like(kp)
    gss = jnp.zeros(state_gdn.shape, F32)
    for l in range(depth):
        gm = norm_mix[l].reshape(1, d)
        if l % 2 == 0:
            e = l // 2
            lam_init = 0.8 - 0.6 * math.exp(-0.3 * l)
            w_in = w_in_even[e].astype(BF16)
            w_out = w_out_even[e].astype(BF16)
            lam_p = jnp.stack([lambda_q1[e], lambda_k1[e], lambda_q2[e], lambda_k2[e]])
            sw = subln_w[e].reshape(1, e_a)
            proj = functools.partial(_even_proj, n_heads=n_heads_a, e_a=e_a, w_b=w_b, half=rot // 2)
            q, kp, vp, kb, vt, bg, cx = proj(xp, gm, w_in, rope_p, tm=tm_p, stack=(kp, vp, e))
            att = _attn_prompt(q, kb, vt, lam_p, sw, batch=batch, seq=seq, n_heads=n_heads_a, e_a=e_a,
                               blk=blk, lam_init=lam_init)
            mix_p = ("even", att, bg, cx, sconv_w[e], w_out)
            cx3 = cx.reshape(batch, seq, w_b)
            if seq >= sconv - 1:
                scp.append(cx3[:, seq - (sconv - 1):])
            else:
                scp.append(jnp.concatenate([jnp.zeros((batch, sconv - 1 - seq, w_b), F32), cx3], axis=1))
            q, k, v, kb, vt, bg, cx = proj(xs, gm, w_in, rope_s, tm=dec_b)
            att = _attn_sample(q, k, v, cache_k, cache_v, e, page_table, lam_p, sw,
                               n_heads=n_heads_a, e_a=e_a, lam_init=lam_init)
            hist = [state_sconv[e, :, j] for j in range(sconv - 1)]
            xs = _even_out_step(att, bg, cx, xs, sconv_w[e], w_out, hist, tm=dec_b)
            ks.append(k.reshape(dec_b, 1, n_heads_a, e_a))
            vs.append(v.reshape(dec_b, 1, n_heads_a, e_a))
            scs.append(jnp.concatenate([state_sconv[e, :, 1:], cx[:, None, :]], axis=1))
        else:
            o = l // 2
            w_in = w_in_odd[o]
            wqkv = w_in[:, :qkv_c].astype(BF16)
            wz = w_in[:, qkv_c:qkv_c + kdim].astype(BF16)
            wba = jnp.pad(w_in[:, qkv_c + kdim:], ((0, 0), (0, LANES - 2 * n_heads_c))).astype(BF16)
            w_out = w_out_odd[o].astype(BF16)
            lane_pad = lambda a: jnp.pad(a, (n_heads_c, LANES - 2 * n_heads_c)).reshape(1, LANES)
            al_pad, dtb_pad = lane_pad(a_log[o]), lane_pad(dt_bias[o])
            nw = gdn_norm_w[o].reshape(1, dk)
            proj = functools.partial(_gdn_proj, n_heads=n_heads_c, dk=dk)
            q, k, v, z, bga, tail = proj(xp, gm, wqkv, wz, wba, gdn_conv_w[o], al_pad, dtb_pad,
                                         tm=tm_p, seq=seq)
            u, w, qe, ke, qk, gc = _gdn_prep(q, k, v, bga, n_heads=n_heads_c, dk=dk,
                                             n_sub=_tile(seq // GDN_CHUNK, 4))
            og, s_new = _gdn_scan(u, w, qe, ke, qk, gc, batch=batch, seq=seq, n_heads=n_heads_c, dk=dk,
                                  rows=scan_rows, nbat=_tile(batch, 4))
            mix_p = ("gdn", og, z, nw, w_out, n_heads_c, dk)
            gcp.append(tail[:, HIST_ROWS - (gdn_conv - 1):])
            gsp.append(s_new)
            hist_arr = state_gdn_conv[o].reshape(dec_b, (gdn_conv - 1) * qkv_c)
            q, k, v, z, bga, raw = proj(xs, gm, wqkv, wz, wba, gdn_conv_w[o], al_pad, dtb_pad,
                                        tm=dec_b, hist=(hist_arr, gdn_conv - 1))
            og, gss = _gdn_step(q, k, v, bga, state_gdn, gss, o, n_heads=n_heads_c, dk=dk, nb=_tile(dec_b, 8))
            xs = _gdn_out(og, z, xs, nw, w_out, tm=dec_b, n_heads=n_heads_c, dk=dk)
            gcs.append(jnp.concatenate([state_gdn_conv[o, :, 1:], raw[:, None, :]], axis=1))
        gn = norm_ffn[l].reshape(1, d)
        cb = ffn_conv_b[l].reshape(1, d_ff)
        last = l == depth - 1
        xp, tail = _ffn(xp, gn, w_up_b, l, ffn_conv_w[l], cb, w_down_b, gf, tm=tm_p, final_norm=last, seq=seq,
                        mix=mix_p)
        fcp.append(tail[:, HIST_ROWS - (ffn_conv - 1):])
        hist_arr = state_ffn_conv[l].reshape(dec_b, (ffn_conv - 1) * d_ff)
        xs, raw = _ffn(xs, gn, w_up_b, l, ffn_conv_w[l], cb, w_down_b, gf, tm=dec_b, final_norm=last,
                       hist=(hist_arr, ffn_conv - 1))
        fcs.append(jnp.concatenate([state_ffn_conv[l, :, 1:], raw[:, None, :]], axis=1))
    return (xp.reshape(batch, seq, d), xs.reshape(dec_b, 1, d),
            kp.reshape(-1, batch, seq, n_heads_a, e_a), vp.reshape(-1, batch, seq, n_heads_a, e_a),
            jnp.stack(ks), jnp.stack(vs),
            jnp.stack(scp), jnp.stack(scs),
            jnp.stack(gcp), jnp.stack(gcs), jnp.stack(gsp), gss,
            jnp.stack(fcp), jnp.stack(fcs))
```
